```python
import math
import jax, jax.numpy as jnp
from jax import lax
import numpy as np

D_MODEL = 4096
BATCH = 4
SEQ = 4096
DEPTH = 2

CTX_LEN = 256
GRID_W = 64
Q_BLOCK = 128
ROPE_THETA = 10000.0
NORM_EPS = 1e-6
MLP_HIDDEN = 4 * D_MODEL
N_MOD = 6
HALF = D_MODEL // 2

RW_HEAD = 64
RW_W = HALF
RW_HEADS = RW_W // RW_HEAD
DECAY_RANK = 96
AAA_RANK = 96
GATE_RANK = 256
GN_EPS = 64e-5
RW_IN = 3 * RW_W + GATE_RANK + 2 * DECAY_RANK + 2 * AAA_RANK

GQ_HEAD = 128
GQ_HEADS = HALF // GQ_HEAD
GQ_KV_HEADS = GQ_HEADS // 4
GQ_GROUP = GQ_HEADS // GQ_KV_HEADS
GQ_SCALE = GQ_HEAD ** -0.5
IN_EVEN = RW_IN + (GQ_HEADS + 2 * GQ_KV_HEADS) * GQ_HEAD

MLA_NOPE = 128
MLA_ROPE = 64
MLA_V = 128
MLA_HEADS = HALF // MLA_V
Q_LORA = 768
KV_LORA = 512
MLA_SCALE = (MLA_NOPE + MLA_ROPE) ** -0.5
MLA_IN = Q_LORA + KV_LORA + MLA_ROPE

DIFF_HEAD = 64
DIFF_V = 2 * DIFF_HEAD
DIFF_HEADS = HALF // DIFF_V
DIFF_SCALE = DIFF_HEAD ** -0.5
DIFF_IN = DIFF_HEADS * (4 * DIFF_HEAD + DIFF_V)
IN_ODD = MLA_IN + DIFF_IN

N_EVEN = (DEPTH + 1) // 2
N_ODD = DEPTH // 2

kernel_name = "hybrid_rwkv7_gqa_mla_diffattn_dit_trunk"


def _rms(x, g):
    xf = x.astype(jnp.float32)
    y = xf * lax.rsqrt(jnp.mean(xf * xf, axis=-1, keepdims=True) + NORM_EPS)
    return (y * g.astype(jnp.float32)).astype(x.dtype)


def _mlp(h, w1, w2):
    return jnp.square(jax.nn.relu(h @ w1)) @ w2


def _heads(z, n_heads):
    return z.reshape(z.shape[:-1] + (n_heads, z.shape[-1] // n_heads))


def _axial_rope_tables(n_tokens, rot_dim):
    n_rows = n_tokens // GRID_W
    row = jnp.repeat(jnp.arange(n_rows, dtype=jnp.float32), GRID_W)
    col = jnp.tile(jnp.arange(GRID_W, dtype=jnp.float32), n_rows)
    axis_dim = rot_dim // 2
    inv_freq = ROPE_THETA ** (-jnp.arange(0, axis_dim, 2, dtype=jnp.float32) / axis_dim)
    ang_r = row[:, None] * inv_freq
    ang_c = col[:, None] * inv_freq
    ang = jnp.concatenate([ang_r, ang_r, ang_c, ang_c], axis=-1)
    return jnp.cos(ang), jnp.sin(ang)


def _rotate_half(z):
    z1, z2 = jnp.split(z, 2, axis=-1)
    return jnp.concatenate([-z2, z1], axis=-1)


def _apply_rope(x, rope):
    cos, sin = rope
    half = x.shape[-1] // 2
    rot = jnp.concatenate([_rotate_half(x[..., :half]), _rotate_half(x[..., half:])], axis=-1)
    return x * cos.astype(x.dtype) + rot * sin.astype(x.dtype)


def _sweep_query_blocks(fn, *qs):
    T = qs[0].shape[-2]
    nb = T // Q_BLOCK

    def split(q):
        q = q.reshape(q.shape[:-2] + (nb, Q_BLOCK, q.shape[-1]))
        return jnp.moveaxis(q, -3, 0)

    out = lax.map(lambda blks: fn(*blks), tuple(split(q) for q in qs))
    out = jnp.moveaxis(out, 0, -3)
    return out.reshape(out.shape[:-3] + (T, out.shape[-1]))


def _attend(q, k, v, scale):
    s = jnp.einsum("bngqd,bnkd->bngqk", q, k).astype(jnp.float32) * scale
    p = jax.nn.softmax(s, axis=-1).astype(v.dtype)
    return jnp.einsum("bngqk,bnkd->bngqd", p, v)


def _centred_shift(p):
    prev = jnp.pad(p[:, :-1], ((0, 0), (1, 0), (0, 0)))
    nxt = jnp.pad(p[:, 1:], ((0, 0), (0, 1), (0, 0)))
    return 0.5 * (prev + nxt)


def _rwkv7_inputs(pr, mu, w0, w_up, a0, a_up, g_up, k_k, k_a):
    pr = pr + (_centred_shift(pr) - pr) * mu
    B, T, _ = pr.shape
    r = pr[..., :RW_W]
    k = pr[..., RW_W:2 * RW_W]
    v = pr[..., 2 * RW_W:3 * RW_W]
    o = 3 * RW_W
    gd = pr[..., o:o + GATE_RANK]
    o += GATE_RANK
    wd = pr[..., o:o + 2 * DECAY_RANK].reshape(B, T, 2, DECAY_RANK)
    o += 2 * DECAY_RANK
    ad = pr[..., o:o + 2 * AAA_RANK].reshape(B, T, 2, AAA_RANK)
    g = jax.nn.sigmoid(gd) @ g_up
    w_log = -jax.nn.softplus(-(w0 + jnp.einsum("btdr,drc->btdc", jnp.tanh(wd), w_up))) - 0.5
    decay = jnp.exp(-jnp.exp(w_log.astype(jnp.float32)))
    a = jax.nn.sigmoid(a0 + jnp.einsum("btdr,drc->btdc", ad, a_up))
    kk = _heads(k * k_k, RW_HEADS).astype(jnp.float32)
    kk = kk / jnp.maximum(jnp.linalg.norm(kk, axis=-1, keepdims=True), 1e-12)
    kk = kk.reshape(B, T, RW_W).astype(k.dtype)
    k_dir = k[:, :, None] * (1.0 + (a - 1.0) * k_a)
    return r, v, g, kk, decay, a, k_dir


def _rwkv7_scan(state0, feats, d, reverse):
    r, v, g, kk, decay, a, k_dir = feats
    B, T, _ = r.shape

    def tm(z):
        return jnp.moveaxis(_heads(z, RW_HEADS).astype(jnp.float32), 1, 0)

    def step(S, inp):
        r_t, w_t, k_t, v_t, kk_t, b_t = inp
        sa = jnp.einsum("bhvk,bhk->bhv", S, kk_t)
        S = S * w_t[..., None, :] - sa[..., :, None] * b_t[..., None, :] + v_t[..., :, None] * k_t[..., None, :]
        return S, jnp.einsum("bhvk,bhk->bhv", S, r_t)

    xs = tuple(tm(z) for z in (r, decay[:, :, d], k_dir[:, :, d], v, kk, kk * a[:, :, d]))
    S, y = lax.scan(step, state0, xs, reverse=reverse)
    return S, jnp.moveaxis(y, 0, 1).reshape(B, T, RW_W).astype(v.dtype)


def _rwkv7_out(y, feats, r_k, ln_w, ln_b):
    r, v, g, kk, decay, a, k_dir = feats
    B, T, _ = y.shape
    yh = _heads(y, RW_HEADS).astype(jnp.float32)
    mean = jnp.mean(yh, axis=-1, keepdims=True)
    var = jnp.mean(jnp.square(yh - mean), axis=-1, keepdims=True)
    yn = ((yh - mean) * lax.rsqrt(var + GN_EPS)).reshape(B, T, RW_W).astype(y.dtype) * ln_w + ln_b
    coef = jnp.sum(_heads(r, RW_HEADS)[:, :, None] * _heads(k_dir, RW_HEADS) * r_k, axis=(2, 4))
    bonus = (coef[..., None] * _heads(v, RW_HEADS)).reshape(B, T, RW_W)
    return (yn + bonus) * g


def _gqa_qkv(pa, q_g, k_g, rope):
    B, T, _ = pa.shape
    nq = GQ_HEADS * GQ_HEAD
    nkv = GQ_KV_HEADS * GQ_HEAD
    q = _rms(pa[..., :nq].reshape(B, T, GQ_HEADS, GQ_HEAD), q_g).transpose(0, 2, 1, 3)
    k = _rms(pa[..., nq:nq + nkv].reshape(B, T, GQ_KV_HEADS, GQ_HEAD), k_g).transpose(0, 2, 1, 3)
    v = pa[..., nq + nkv:].reshape(B, T, GQ_KV_HEADS, GQ_HEAD).transpose(0, 2, 1, 3)
    if rope is not None:
        q = _apply_rope(q, rope)
        k = _apply_rope(k, rope)
    return q.reshape(B, GQ_KV_HEADS, GQ_GROUP, T, GQ_HEAD), k, v


def _even_mixer(a_lat, a_ctx, rope, w_in, w_out, mu, w0, w_up, a0, a_up, g_up, k_k, k_a, r_k,
                ln_w, ln_b, q_g, k_g, need_ctx):
    p_lat = a_lat @ w_in
    p_ctx = a_ctx @ w_in
    rw_args = (mu, w0, w_up, a0, a_up, g_up, k_k, k_a)
    f_ctx = _rwkv7_inputs(p_ctx[..., :RW_IN], *rw_args)
    f_lat = _rwkv7_inputs(p_lat[..., :RW_IN], *rw_args)
    zero = jnp.zeros((a_lat.shape[0], RW_HEADS, RW_HEAD, RW_HEAD), jnp.float32)
    s_fwd, y_cf = _rwkv7_scan(zero, f_ctx, 0, False)
    s_bwd, y_cb = _rwkv7_scan(zero, f_ctx, 1, True)
    _, y_lf = _rwkv7_scan(s_fwd, f_lat, 0, False)
    _, y_lb = _rwkv7_scan(s_bwd, f_lat, 1, True)
    q_l, k_l, v_l = _gqa_qkv(p_lat[..., RW_IN:], q_g, k_g, rope)
    q_c, k_c, v_c = _gqa_qkv(p_ctx[..., RW_IN:], q_g, k_g, None)

    def merge(y_rw, feats, q, k, v):
        rw = _rwkv7_out(y_rw, feats, r_k, ln_w, ln_b)
        at = _sweep_query_blocks(lambda qb: _attend(qb, k, v, GQ_SCALE), q)
        B, N, G, T, dh = at.shape
        at = at.transpose(0, 3, 1, 2, 4).reshape(B, T, N * G * dh)
        return jnp.concatenate([rw, at], axis=-1) @ w_out

    o_lat = merge(y_lf + y_lb, f_lat, q_l, jnp.concatenate([k_c, k_l], axis=2),
                  jnp.concatenate([v_c, v_l], axis=2))
    o_ctx = merge(y_cf + y_cb, f_ctx, q_c, k_c, v_c) if need_ctx else None
    return o_lat, o_ctx


def _mla_qkv(p, q_norm, q_up, kv_norm, kv_up, rope):
    B, T, _ = p.shape
    c_q = _rms(p[..., :Q_LORA], q_norm)
    c_kv = _rms(p[..., Q_LORA:Q_LORA + KV_LORA], kv_norm)
    k_pe = p[..., Q_LORA + KV_LORA:MLA_IN][:, None]
    q = (c_q @ q_up).reshape(B, T, MLA_HEADS, MLA_NOPE + MLA_ROPE).transpose(0, 2, 1, 3)
    kv = (c_kv @ kv_up).reshape(B, T, MLA_HEADS, MLA_NOPE + MLA_V).transpose(0, 2, 1, 3)
    q_nope, q_pe = q[..., :MLA_NOPE], q[..., MLA_NOPE:]
    k_nope, v = kv[..., :MLA_NOPE], kv[..., MLA_NOPE:]
    if rope is not None:
        q_pe = _apply_rope(q_pe, rope)
        k_pe = _apply_rope(k_pe, rope)
    q = jnp.concatenate([q_nope, q_pe], axis=-1)[:, :, None]
    k = jnp.concatenate([k_nope, jnp.broadcast_to(k_pe, (B, MLA_HEADS, T, MLA_ROPE))], axis=-1)
    return q, k, v


def _diff_qkv(p, rope):
    B, T, _ = p.shape
    qk_w = DIFF_HEADS * 2 * DIFF_HEAD
    q = p[..., :qk_w].reshape(B, T, DIFF_HEADS, 2, DIFF_HEAD).transpose(0, 2, 3, 1, 4)
    k = p[..., qk_w:2 * qk_w].reshape(B, T, DIFF_HEADS, 2, DIFF_HEAD).transpose(0, 2, 3, 1, 4)
    v = p[..., 2 * qk_w:].reshape(B, T, DIFF_HEADS, DIFF_V).transpose(0, 2, 1, 3)
    if rope is not None:
        q = _apply_rope(q, rope)
        k = _apply_rope(k, rope)
    return q, k, v


def _diff_attend(q1, q2, k, v, lam):
    s1 = jnp.einsum("bhqd,bhkd->bhqk", q1, k[:, :, 0]).astype(jnp.float32) * DIFF_SCALE
    s2 = jnp.einsum("bhqd,bhkd->bhqk", q2, k[:, :, 1]).astype(jnp.float32) * DIFF_SCALE
    p = jax.nn.softmax(s1, axis=-1) - lam * jax.nn.softmax(s2, axis=-1)
    return jnp.einsum("bhqk,bhkd->bhqd", p.astype(v.dtype), v)


def _odd_mixer(a_lat, a_ctx, rope, w_in, w_out, q_norm, q_up, kv_norm, kv_up, lq1, lk1, lq2, lk2,
               subln, layer_idx, need_ctx):
    p_lat = a_lat @ w_in
    p_ctx = a_ctx @ w_in
    lam_init = 0.8 - 0.6 * math.exp(-0.3 * layer_idx)
    lam = (jnp.exp(jnp.sum(lq1 * lk1).astype(jnp.float32))
           - jnp.exp(jnp.sum(lq2 * lk2).astype(jnp.float32)) + lam_init)
    mla_args = (q_norm, q_up, kv_norm, kv_up)
    mq_l, mk_l, mv_l = _mla_qkv(p_lat[..., :MLA_IN], *mla_args, rope)
    mq_c, mk_c, mv_c = _mla_qkv(p_ctx[..., :MLA_IN], *mla_args, None)
    dq_l, dk_l, dv_l = _diff_qkv(p_lat[..., MLA_IN:], rope)
    dq_c, dk_c, dv_c = _diff_qkv(p_ctx[..., MLA_IN:], None)

    def merge(mq, mk, mv, dq, dk, dv):
        m = _sweep_query_blocks(lambda qb: _attend(qb, mk, mv, MLA_SCALE), mq)
        d = _sweep_query_blocks(lambda a, b: _diff_attend(a, b, dk, dv, lam), dq[:, :, 0], dq[:, :, 1])
        d = _rms(d, subln) * (1.0 - lam_init)
        B, H, _, T, dv_ = m.shape
        m = m[:, :, 0].transpose(0, 2, 1, 3).reshape(B, T, H * dv_)
        d = d.transpose(0, 2, 1, 3).reshape(B, T, DIFF_HEADS * DIFF_V)
        return jnp.concatenate([m, d], axis=-1) @ w_out

    o_lat = merge(mq_l, jnp.concatenate([mk_c, mk_l], axis=2), jnp.concatenate([mv_c, mv_l], axis=2),
                  dq_l, jnp.concatenate([dk_c, dk_l], axis=3), jnp.concatenate([dv_c, dv_l], axis=2))
    o_ctx = merge(mq_c, mk_c, mv_c, dq_c, dk_c, dv_c) if need_ctx else None
    return o_lat, o_ctx


def setup_inputs(seed: int = 0) -> dict:
    key = jax.random.key(seed)
    ks = iter(jax.random.split(key, 64))
    D = D_MODEL

    def nrm(shape, scale):
        return jax.random.normal(next(ks), shape, jnp.float32) * scale

    def gain(shape):
        return 1.0 + nrm(shape, 0.05)

    return {
        "x": nrm((BATCH, SEQ, D), 1.0),
        "c": nrm((BATCH, D), 1.0),
        "ctx": nrm((BATCH, CTX_LEN, D), 1.0),
        "c_ctx": nrm((D,), 1.0),
        "ada_w": nrm((DEPTH, D, N_MOD * D), 0.5 * D ** -0.5),
        "ada_b": nrm((DEPTH, N_MOD * D), 0.01),
        "norm1_g": gain((DEPTH, D)),
        "norm2_g": gain((DEPTH, D)),
        "mlp_w1": nrm((DEPTH, D, MLP_HIDDEN), D ** -0.5),
        "mlp_w2": nrm((DEPTH, MLP_HIDDEN, D), MLP_HIDDEN ** -0.5),
        "final_g": gain((D,)),
        "ev_w_in": nrm((N_EVEN, D, IN_EVEN), D ** -0.5),
        "ev_w_out": nrm((N_EVEN, RW_W + GQ_HEADS * GQ_HEAD, D), D ** -0.5),
        "rw_mu": jax.random.uniform(next(ks), (N_EVEN, RW_IN), jnp.float32),
        "rw_w0": jax.random.uniform(next(ks), (N_EVEN, 2, RW_W), jnp.float32, -2.0, 1.0),
        "rw_w_up": nrm((N_EVEN, 2, DECAY_RANK, RW_W), DECAY_RANK ** -0.5),
        "rw_a0": nrm((N_EVEN, 2, RW_W), 0.5),
        "rw_a_up": nrm((N_EVEN, 2, AAA_RANK, RW_W), AAA_RANK ** -0.5),
        "rw_g_up": nrm((N_EVEN, GATE_RANK, RW_W), GATE_RANK ** -0.5),
        "rw_k_k": 0.85 + nrm((N_EVEN, RW_W), 0.05),
        "rw_k_a": gain((N_EVEN, RW_W)),
        "rw_r_k": nrm((N_EVEN, RW_HEADS, RW_HEAD), 0.1),
        "rw_ln_w": gain((N_EVEN, RW_W)),
        "rw_ln_b": nrm((N_EVEN, RW_W), 0.01),
        "gq_q_norm": gain((N_EVEN, GQ_HEAD)),
        "gq_k_norm": gain((N_EVEN, GQ_HEAD)),
        "od_w_in": nrm((N_ODD, D, IN_ODD), D ** -0.5),
        "od_w_out": nrm((N_ODD, MLA_HEADS * MLA_V + DIFF_HEADS * DIFF_V, D), D ** -0.5),
        "mla_q_norm": gain((N_ODD, Q_LORA)),
        "mla_q_up": nrm((N_ODD, Q_LORA, MLA_HEADS * (MLA_NOPE + MLA_ROPE)), Q_LORA ** -0.5),
        "mla_kv_norm": gain((N_ODD, KV_LORA)),
        "mla_kv_up": nrm((N_ODD, KV_LORA, MLA_HEADS * (MLA_NOPE + MLA_V)), KV_LORA ** -0.5),
        "diff_lq1": nrm((N_ODD, DIFF_HEAD), 0.1),
        "diff_lk1": nrm((N_ODD, DIFF_HEAD), 0.1),
        "diff_lq2": nrm((N_ODD, DIFF_HEAD), 0.1),
        "diff_lk2": nrm((N_ODD, DIFF_HEAD), 0.1),
        "diff_subln": gain((N_ODD, DIFF_V)),
    }


def reference(x, c, ctx, c_ctx, ada_w, ada_b, norm1_g, norm2_g, mlp_w1, mlp_w2, final_g,
              ev_w_in, ev_w_out, rw_mu, rw_w0, rw_w_up, rw_a0, rw_a_up, rw_g_up, rw_k_k, rw_k_a,
              rw_r_k, rw_ln_w, rw_ln_b, gq_q_norm, gq_k_norm,
              od_w_in, od_w_out, mla_q_norm, mla_q_up, mla_kv_norm, mla_kv_up,
              diff_lq1, diff_lk1, diff_lq2, diff_lk2, diff_subln):
    n_lat = x.shape[1]
    rope_gq = _axial_rope_tables(n_lat, GQ_HEAD)
    rope_64 = _axial_rope_tables(n_lat, MLA_ROPE)
    h = x
    s = ctx
    for i in range(DEPTH):
        last = i == DEPTH - 1
        mod = (jax.nn.silu(c) @ ada_w[i] + ada_b[i])[:, None, :]
        mod_c = jax.nn.silu(c_ctx) @ ada_w[i] + ada_b[i]
        sh1, sc1, g1, sh2, sc2, g2 = jnp.split(mod, N_MOD, axis=-1)
        csh1, csc1, cg1, csh2, csc2, cg2 = jnp.split(mod_c, N_MOD, axis=-1)
        a_lat = _rms(h, norm1_g[i]) * (1.0 + sc1) + sh1
        a_ctx = _rms(s, norm1_g[i]) * (1.0 + csc1) + csh1
        j = i // 2
        if i % 2 == 0:
            o_lat, o_ctx = _even_mixer(a_lat, a_ctx, rope_gq, ev_w_in[j], ev_w_out[j], rw_mu[j], rw_w0[j],
                                       rw_w_up[j], rw_a0[j], rw_a_up[j], rw_g_up[j], rw_k_k[j], rw_k_a[j],
                                       rw_r_k[j], rw_ln_w[j], rw_ln_b[j], gq_q_norm[j], gq_k_norm[j],
                                       not last)
        else:
            o_lat, o_ctx = _odd_mixer(a_lat, a_ctx, rope_64, od_w_in[j], od_w_out[j], mla_q_norm[j],
                                      mla_q_up[j], mla_kv_norm[j], mla_kv_up[j], diff_lq1[j], diff_lk1[j],
                                      diff_lq2[j], diff_lk2[j], diff_subln[j], i, not last)
        h = h + g1 * o_lat
        h = h + g2 * _mlp(_rms(h, norm2_g[i]) * (1.0 + sc2) + sh2, mlp_w1[i], mlp_w2[i])
        if not last:
            s = s + cg1 * o_ctx
            s = s + cg2 * _mlp(_rms(s, norm2_g[i]) * (1.0 + csc2) + csh2, mlp_w1[i], mlp_w2[i])
    return _rms(h, final_g)
```

```python
import functools
import math

import jax
import jax.numpy as jnp
from jax import lax
from jax.experimental import pallas as pl
from jax.experimental.pallas import tpu as pltpu

D_MODEL = 4096
BATCH = 4
SEQ = 4096
DEPTH = 2
CTX_LEN = 256
GRID_W = 64
ROPE_THETA = 10000.0
NORM_EPS = 1e-6
MLP_HIDDEN = 4 * D_MODEL
N_MOD = 6
HALF = D_MODEL // 2

RW_HEAD = 64
RW_W = HALF
RW_HEADS = RW_W // RW_HEAD
DECAY_RANK = 96
AAA_RANK = 96
GATE_RANK = 256
GN_EPS = 64e-5
RW_IN = 3 * RW_W + GATE_RANK + 2 * DECAY_RANK + 2 * AAA_RANK

GQ_HEAD = 128
GQ_HEADS = HALF // GQ_HEAD
GQ_KV_HEADS = GQ_HEADS // 4
GQ_GROUP = GQ_HEADS // GQ_KV_HEADS
GQ_SCALE = GQ_HEAD ** -0.5
IN_EVEN = RW_IN + (GQ_HEADS + 2 * GQ_KV_HEADS) * GQ_HEAD

MLA_NOPE = 128
MLA_ROPE = 64
MLA_V = 128
MLA_HEADS = HALF // MLA_V
Q_LORA = 768
KV_LORA = 512
MLA_SCALE = (MLA_NOPE + MLA_ROPE) ** -0.5
MLA_IN = Q_LORA + KV_LORA + MLA_ROPE

DIFF_HEAD = 64
DIFF_V = 2 * DIFF_HEAD
DIFF_HEADS = HALF // DIFF_V
DIFF_SCALE = DIFF_HEAD ** -0.5
DIFF_IN = DIFF_HEADS * (4 * DIFF_HEAD + DIFF_V)
IN_ODD = MLA_IN + DIFF_IN
IN_ODD_PADDED = 7680

N_LAT_ROWS = BATCH * SEQ
N_CTX_ROWS = BATCH * CTX_LEN
N_ROWS = N_LAT_ROWS + N_CTX_ROWS
T_ALL = CTX_LEN + SEQ
CTX_BLOCK0 = N_LAT_ROWS // CTX_LEN

V7X_VMEM_LIMIT_BYTES = 56 * 1024 * 1024

BF16 = jnp.bfloat16
F32 = jnp.float32


def _params(*sem):
    return pltpu.CompilerParams(dimension_semantics=sem, vmem_limit_bytes=V7X_VMEM_LIMIT_BYTES)


def _row_group(row_tile, tile_rows):
    start = row_tile * tile_rows
    return jnp.where(start < N_LAT_ROWS, 1 + start // SEQ, 0)


def _norm_kernel(x_ref, g_ref, *rest, modulate):
    if modulate:
        sc_ref, sh_ref, o_ref = rest
    else:
        (o_ref,) = rest
    x = x_ref[...]
    ms = jnp.mean(x * x, axis=-1, keepdims=True)
    y = x * lax.rsqrt(ms + NORM_EPS) * g_ref[...]
    if modulate:
        y = y * (1.0 + sc_ref[0]) + sh_ref[0]
    o_ref[...] = y.astype(o_ref.dtype)


def _norm(x, g, sc=None, sh=None, *, rows, out_dtype, tile=256):
    d = x.shape[1]
    modulate = sc is not None
    in_specs = [pl.BlockSpec((tile, d), lambda i: (i, 0)),
                pl.BlockSpec((1, d), lambda i: (0, 0))]
    args = [x, g.reshape(1, d)]
    if modulate:
        mod_spec = pl.BlockSpec((1, 1, d), lambda i: (_row_group(i, tile), 0, 0))
        in_specs += [mod_spec, mod_spec]
        args += [sc, sh]
    return pl.pallas_call(
        functools.partial(_norm_kernel, modulate=modulate),
        grid=(rows // tile,),
        in_specs=in_specs,
        out_specs=pl.BlockSpec((tile, d), lambda i: (i, 0)),
        out_shape=jax.ShapeDtypeStruct((rows, d), out_dtype),
        compiler_params=_params("parallel"),
        name="rmsnorm_mod" if modulate else "rmsnorm",
    )(*args)


def _mm_kernel(a_ref, w_ref, *rest, epilogue, nk):
    if epilogue == "gated_residual":
        res_ref, gate_ref, o_ref, acc_ref = rest
    else:
        o_ref, acc_ref = rest
    k = pl.program_id(2)
    part = jnp.dot(a_ref[...].astype(BF16), w_ref[...].astype(BF16), preferred_element_type=F32)

    @pl.when(k == 0)
    def _():
        acc_ref[...] = part

    @pl.when(k > 0)
    def _():
        acc_ref[...] += part

    @pl.when(k == nk - 1)
    def _():
        acc = acc_ref[...]
        if epilogue == "relu2":
            acc = jnp.square(jnp.maximum(acc, 0.0))
        elif epilogue == "gated_residual":
            acc = res_ref[...] + gate_ref[0] * acc
        o_ref[...] = acc.astype(o_ref.dtype)


def _matmul(a, w, *, out_dtype, tm, tn, tk, rows=None, epilogue=None, res=None, gate=None, name):
    kdim, n = w.shape
    m = a.shape[0] if rows is None else rows
    assert a.shape[1] == kdim and m % tm == 0 and n % tn == 0 and kdim % tk == 0
    nk = kdim // tk
    in_specs = [pl.BlockSpec((tm, tk), lambda i, j, k: (i, k)),
                pl.BlockSpec((tk, tn), lambda i, j, k: (k, j))]
    args = [a, w]
    if epilogue == "gated_residual":
        in_specs += [pl.BlockSpec((tm, tn), lambda i, j, k: (i, j)),
                     pl.BlockSpec((1, 1, tn), lambda i, j, k: (_row_group(i, tm), 0, j))]
        args += [res, gate]
    return pl.pallas_call(
        functools.partial(_mm_kernel, epilogue=epilogue, nk=nk),
        grid=(m // tm, n // tn, nk),
        in_specs=in_specs,
        out_specs=pl.BlockSpec((tm, tn), lambda i, j, k: (i, j)),
        out_shape=jax.ShapeDtypeStruct((m, n), out_dtype),
        scratch_shapes=[pltpu.VMEM((tm, tn), F32)],
        compiler_params=_params("parallel", "parallel", "arbitrary"),
        name=name,
    )(*args)


def _nt_dot(a, b):
    return lax.dot_general(a, b, (((1,), (1,)), ((), ())), preferred_element_type=F32)


def _softmax_parts(s_list):
    m = functools.reduce(jnp.maximum, [jnp.max(s, axis=-1, keepdims=True) for s in s_list])
    p_list = [jnp.exp(s - m) for s in s_list]
    l = functools.reduce(jnp.add, [jnp.sum(p, axis=-1, keepdims=True) for p in p_list])
    return p_list, l


def _pv(p_list, v_refs):
    return functools.reduce(jnp.add, [jnp.dot(p.astype(BF16), v_ref[...], preferred_element_type=F32)
                                      for p, v_ref in zip(p_list, v_refs)])


def _gqa_kernel(q_ref, *rest, with_lat):
    if with_lat:
        kc_ref, vc_ref, kl_ref, vl_ref, o_ref = rest
        k_refs, v_refs = (kc_ref, kl_ref), (vc_ref, vl_ref)
    else:
        kc_ref, vc_ref, o_ref = rest
        k_refs, v_refs = (kc_ref,), (vc_ref,)
    for g in range(GQ_GROUP):
        q = q_ref[:, g * GQ_HEAD:(g + 1) * GQ_HEAD]
        s_list = [_nt_dot(q, k_ref[...]) * GQ_SCALE for k_ref in k_refs]
        p_list, l = _softmax_parts(s_list)
        o_ref[:, g * GQ_HEAD:(g + 1) * GQ_HEAD] = (_pv(p_list, v_refs) / l).astype(o_ref.dtype)


def _gqa_attention(q, k, v, *, latent, tq=256):
    gw = GQ_GROUP * GQ_HEAD
    ctx_kv = pl.BlockSpec((CTX_LEN, GQ_HEAD), lambda b, n, i: (CTX_BLOCK0 + b, n))
    if latent:
        per_b = SEQ // tq
        lat_kv = pl.BlockSpec((SEQ, GQ_HEAD), lambda b, n, i: (b, n))
        in_specs = [pl.BlockSpec((tq, gw), lambda b, n, i: (b * per_b + i, n)), ctx_kv, ctx_kv, lat_kv, lat_kv]
        args = (q, k, v, k, v)
        rows = N_LAT_ROWS
    else:
        per_b = CTX_LEN // tq
        q_blk0 = N_LAT_ROWS // tq
        in_specs = [pl.BlockSpec((tq, gw), lambda b, n, i: (q_blk0 + b * per_b + i, n)), ctx_kv, ctx_kv]
        args = (q, k, v)
        rows = N_CTX_ROWS
    return pl.pallas_call(
        functools.partial(_gqa_kernel, with_lat=latent),
        grid=(BATCH, GQ_KV_HEADS, per_b),
        in_specs=in_specs,
        out_specs=pl.BlockSpec((tq, gw), lambda b, n, i: (b * per_b + i, n)),
        out_shape=jax.ShapeDtypeStruct((rows, GQ_HEADS * GQ_HEAD), BF16),
        compiler_params=_params("parallel", "parallel", "arbitrary"),
        name="gqa_attention_lat" if latent else "gqa_attention_ctx",
    )(*args)


def _mla_kernel(qn_ref, qp_ref, knc_ref, kpc_ref, vc_ref, knl_ref, kpl_ref, vl_ref, o_ref):
    qn = qn_ref[...]
    qp = qp_ref[...]
    s_c = (_nt_dot(qn, knc_ref[...]) + _nt_dot(qp, kpc_ref[...])) * MLA_SCALE
    s_l = (_nt_dot(qn, knl_ref[...]) + _nt_dot(qp, kpl_ref[...])) * MLA_SCALE
    p_list, l = _softmax_parts([s_c, s_l])
    o_ref[...] = (_pv(p_list, (vc_ref, vl_ref)) / l).astype(o_ref.dtype)


def _mla_attention(q_nope, q_pe, kv, k_pe, *, tq=256):
    per_b = SEQ // tq
    in_specs = [
        pl.BlockSpec((tq, MLA_NOPE), lambda b, h, i: (b * per_b + i, h)),
        pl.BlockSpec((None, tq, MLA_ROPE), lambda b, h, i: (h, b * per_b + i, 0)),
        pl.BlockSpec((CTX_LEN, MLA_NOPE), lambda b, h, i: (CTX_BLOCK0 + b, h)),
        pl.BlockSpec((CTX_LEN, MLA_ROPE), lambda b, h, i: (CTX_BLOCK0 + b, 0)),
        pl.BlockSpec((CTX_LEN, MLA_V), lambda b, h, i: (CTX_BLOCK0 + b, MLA_HEADS + h)),
        pl.BlockSpec((SEQ, MLA_NOPE), lambda b, h, i: (b, h)),
        pl.BlockSpec((SEQ, MLA_ROPE), lambda b, h, i: (b, 0)),
        pl.BlockSpec((SEQ, MLA_V), lambda b, h, i: (b, MLA_HEADS + h)),
    ]
    return pl.pallas_call(
        _mla_kernel,
        grid=(BATCH, MLA_HEADS, per_b),
        in_specs=in_specs,
        out_specs=pl.BlockSpec((tq, MLA_V), lambda b, h, i: (b * per_b + i, h)),
        out_shape=jax.ShapeDtypeStruct((N_LAT_ROWS, MLA_HEADS * MLA_V), BF16),
        compiler_params=_params("parallel", "parallel", "arbitrary"),
        name="mla_attention",
    )(q_nope, q_pe, kv, k_pe, kv, kv, k_pe, kv)


def _diff_kernel(lam_ref, q_ref, kc_ref, vc_ref, kl_ref, vl_ref, g_ref, o_ref, *, out_scale):
    lam = lam_ref[0]
    q = q_ref[...]
    first = lax.broadcasted_iota(jnp.int32, q.shape, 1) < DIFF_HEAD
    zero = jnp.zeros_like(q)
    q1 = jnp.where(first, q, zero)
    q2 = jnp.where(first, zero, q)
    kc = kc_ref[...]
    kl = kl_ref[...]
    (p1c, p1l), l1 = _softmax_parts([_nt_dot(q1, kc) * DIFF_SCALE, _nt_dot(q1, kl) * DIFF_SCALE])
    (p2c, p2l), l2 = _softmax_parts([_nt_dot(q2, kc) * DIFF_SCALE, _nt_dot(q2, kl) * DIFF_SCALE])
    r1 = 1.0 / l1
    r2 = lam / l2
    o = _pv([p1c * r1 - p2c * r2, p1l * r1 - p2l * r2], (vc_ref, vl_ref))
    ms = jnp.mean(o * o, axis=-1, keepdims=True)
    o = o * lax.rsqrt(ms + NORM_EPS) * g_ref[...]
    o_ref[...] = (o * out_scale).astype(o_ref.dtype)


def _diff_attention(lam, q, k, v, subln, out_scale, *, tq=256):
    per_b = SEQ // tq
    ctx_kv = pl.BlockSpec((CTX_LEN, DIFF_V), lambda b, h, i: (CTX_BLOCK0 + b, h))
    lat_kv = pl.BlockSpec((SEQ, DIFF_V), lambda b, h, i: (b, h))
    in_specs = [
        pl.BlockSpec(memory_space=pltpu.SMEM),
        pl.BlockSpec((tq, DIFF_V), lambda b, h, i: (b * per_b + i, h)),
        ctx_kv, ctx_kv, lat_kv, lat_kv,
        pl.BlockSpec((1, DIFF_V), lambda b, h, i: (0, 0)),
    ]
    return pl.pallas_call(
        functools.partial(_diff_kernel, out_scale=out_scale),
        grid=(BATCH, DIFF_HEADS, per_b),
        in_specs=in_specs,
        out_specs=pl.BlockSpec((tq, DIFF_V), lambda b, h, i: (b * per_b + i, h)),
        out_shape=jax.ShapeDtypeStruct((N_LAT_ROWS, DIFF_HEADS * DIFF_V), BF16),
        compiler_params=_params("parallel", "parallel", "arbitrary"),
        name="diff_attention",
    )(lam, q, k, v, k, v, subln.reshape(1, DIFF_V))


RW_CHUNK = 128
RW_HEAD_BLOCK = 16


def _rwkv_kernel(r_ref, kk_ref, w_ref, k_ref, b_ref, vt_ref, yt_ref, s_ref):
    d = pl.program_id(0)
    c = pl.program_id(3)

    @pl.when(c == 0)
    def _():
        s_ref[...] = jnp.zeros_like(s_ref)

    yt_ref[...] = jnp.zeros_like(yt_ref)
    lane_t = lax.broadcasted_iota(jnp.int32, (1, 1, RW_CHUNK), 2)

    def step(i, carry):
        t = jnp.where(d == 0, i, RW_CHUNK - 1 - i)
        row = lambda ref: ref[0, 0, :, pl.ds(t, 1), :]
        at_t = lane_t == t
        s = s_ref[...]
        sa = jnp.sum(s * row(kk_ref), axis=-1, keepdims=True)
        v_col = jnp.sum(jnp.where(at_t, vt_ref[0], 0.0), axis=-1, keepdims=True)
        s = s * row(w_ref) - sa * row(b_ref) + v_col * row(k_ref)
        s_ref[...] = s
        y = jnp.sum(s * row(r_ref), axis=-1, keepdims=True)
        yt_ref[0, 0] = jnp.where(at_t, y, yt_ref[0, 0])
        return carry

    lax.fori_loop(0, RW_CHUNK, step, 0)


def _rwkv_scan(r, kk, w, k, b, vt):
    hb = RW_HEAD_BLOCK
    n_ctx = CTX_LEN // RW_CHUNK
    n_all = T_ALL // RW_CHUNK

    def chunk(d, c):
        bwd = jnp.where(c < n_ctx, n_ctx - 1 - c, n_all - 1 - (c - n_ctx))
        return jnp.where(d == 0, c, bwd)

    shared = pl.BlockSpec((1, 1, hb, RW_CHUNK, RW_HEAD), lambda d, bb, h, c: (0, bb, h, chunk(d, c), 0))
    per_dir = pl.BlockSpec((1, 1, hb, RW_CHUNK, RW_HEAD), lambda d, bb, h, c: (d, bb, h, chunk(d, c), 0))
    vt_spec = pl.BlockSpec((1, hb, RW_HEAD, RW_CHUNK), lambda d, bb, h, c: (bb, h, 0, chunk(d, c)))
    return pl.pallas_call(
        _rwkv_kernel,
        grid=(2, BATCH, RW_HEADS // hb, n_all),
        in_specs=[shared, shared, per_dir, per_dir, per_dir, vt_spec],
        out_specs=pl.BlockSpec((1, 1, hb, RW_HEAD, RW_CHUNK), lambda d, bb, h, c: (d, bb, h, 0, chunk(d, c))),
        out_shape=jax.ShapeDtypeStruct((2, BATCH, RW_HEADS, RW_HEAD, T_ALL), F32),
        scratch_shapes=[pltpu.VMEM((hb, RW_HEAD, RW_HEAD), F32)],
        compiler_params=_params("parallel", "parallel", "parallel", "arbitrary"),
        name="rwkv7_scan",
    )(r, kk, w, k, b, vt)


def _axial_rope_tables(n_tokens, rot_dim):
    n_rows = n_tokens // GRID_W
    row = jnp.repeat(jnp.arange(n_rows, dtype=F32), GRID_W)
    col = jnp.tile(jnp.arange(GRID_W, dtype=F32), n_rows)
    axis_dim = rot_dim // 2
    inv_freq = ROPE_THETA ** (-jnp.arange(0, axis_dim, 2, dtype=F32) / axis_dim)
    ang_r = row[:, None] * inv_freq
    ang_c = col[:, None] * inv_freq
    ang = jnp.concatenate([ang_r, ang_r, ang_c, ang_c], axis=-1)
    return jnp.cos(ang), jnp.sin(ang)


def _rotate_half(z):
    z1, z2 = jnp.split(z, 2, axis=-1)
    return jnp.concatenate([-z2, z1], axis=-1)


def _apply_rope(x, cos, sin):
    half = x.shape[-1] // 2
    rot = jnp.concatenate([_rotate_half(x[..., :half]), _rotate_half(x[..., half:])], axis=-1)
    return x * cos + rot * sin


def _rms_f32(x, g):
    return x * lax.rsqrt(jnp.mean(x * x, axis=-1, keepdims=True) + NORM_EPS) * g


def _rope_lat(x, rope):
    cos, sin = rope
    heads, dim = x.shape[1:]
    lat = x[:N_LAT_ROWS].reshape(BATCH, SEQ, heads, dim)
    lat = _apply_rope(lat, cos[None, :, None, :], sin[None, :, None, :]).reshape(N_LAT_ROWS, heads, dim)
    return lat if x.shape[0] == N_LAT_ROWS else jnp.concatenate([lat, x[N_LAT_ROWS:]], axis=0)


def _centred_shift_rows(p):
    def shift(z):
        prev = jnp.pad(z[:, :-1], ((0, 0), (1, 0), (0, 0)))
        nxt = jnp.pad(z[:, 1:], ((0, 0), (0, 1), (0, 0)))
        return 0.5 * (prev + nxt)
    c = p.shape[-1]
    lat = shift(p[:N_LAT_ROWS].reshape(BATCH, SEQ, c)).reshape(N_LAT_ROWS, c)
    ctx = shift(p[N_LAT_ROWS:].reshape(BATCH, CTX_LEN, c)).reshape(N_CTX_ROWS, c)
    return jnp.concatenate([lat, ctx], axis=0)


def _to_head_major(z):
    lat = z[:N_LAT_ROWS].reshape(BATCH, SEQ, RW_HEADS, RW_HEAD)
    ctx = z[N_LAT_ROWS:].reshape(BATCH, CTX_LEN, RW_HEADS, RW_HEAD)
    return jnp.concatenate([ctx, lat], axis=1).transpose(0, 2, 1, 3)


def _small_mm(a, w, name):
    return _matmul(a.astype(BF16), w.astype(BF16), out_dtype=F32, tm=1024, tn=w.shape[1], tk=w.shape[0], name=name)


def _rwkv_mixer(pr, mu, w0, w_up, a0, a_up, g_up, k_k, k_a, r_k, ln_w, ln_b):
    pr = pr + (_centred_shift_rows(pr) - pr) * mu
    r = pr[:, :RW_W]
    k = pr[:, RW_W:2 * RW_W]
    v = pr[:, 2 * RW_W:3 * RW_W]
    o = 3 * RW_W
    gd = pr[:, o:o + GATE_RANK]
    o += GATE_RANK
    wd = pr[:, o:o + 2 * DECAY_RANK].reshape(N_ROWS, 2, DECAY_RANK)
    o += 2 * DECAY_RANK
    ad = pr[:, o:o + 2 * AAA_RANK].reshape(N_ROWS, 2, AAA_RANK)
    g = _small_mm(jax.nn.sigmoid(gd), g_up, "rwkv_gate_up")
    wu = jnp.stack([_small_mm(jnp.tanh(wd[:, d]), w_up[d], "rwkv_decay_up") for d in range(2)])
    au = jnp.stack([_small_mm(ad[:, d], a_up[d], "rwkv_aaa_up") for d in range(2)])
    w_log = -jax.nn.softplus(-(w0[:, None, :] + wu)) - 0.5
    decay = jnp.exp(-jnp.exp(w_log))
    a = jax.nn.sigmoid(a0[:, None, :] + au)
    kk = (k * k_k).reshape(N_ROWS, RW_HEADS, RW_HEAD)
    kk = kk / jnp.maximum(jnp.linalg.norm(kk, axis=-1, keepdims=True), 1e-12)
    kk = kk.reshape(N_ROWS, RW_W)
    k_dir = k[None] * (1.0 + (a - 1.0) * k_a)
    b_dir = kk[None] * a

    hm = _to_head_major
    vt = hm(v).transpose(0, 1, 3, 2)
    yt = _rwkv_scan(hm(r)[None], hm(kk)[None],
                    jnp.stack([hm(decay[0]), hm(decay[1])]),
                    jnp.stack([hm(k_dir[0]), hm(k_dir[1])]),
                    jnp.stack([hm(b_dir[0]), hm(b_dir[1])]), vt)
    y = (yt[0] + yt[1]).transpose(0, 3, 1, 2)
    y = jnp.concatenate([y[:, CTX_LEN:].reshape(N_LAT_ROWS, RW_HEADS, RW_HEAD),
                         y[:, :CTX_LEN].reshape(N_CTX_ROWS, RW_HEADS, RW_HEAD)], axis=0)

    mean = jnp.mean(y, axis=-1, keepdims=True)
    var = jnp.mean(jnp.square(y - mean), axis=-1, keepdims=True)
    yn = ((y - mean) * lax.rsqrt(var + GN_EPS)).reshape(N_ROWS, RW_W) * ln_w + ln_b
    rh = r.reshape(N_ROWS, RW_HEADS, RW_HEAD)
    coef = jnp.sum(rh[None] * k_dir.reshape(2, N_ROWS, RW_HEADS, RW_HEAD) * r_k, axis=(0, 3))
    bonus = (coef[..., None] * v.reshape(N_ROWS, RW_HEADS, RW_HEAD)).reshape(N_ROWS, RW_W)
    return (yn + bonus) * g


def _even_mixer(a_all, rope, w_in, rw_args, q_g, k_g, need_ctx):
    p = _matmul(a_all, w_in.astype(BF16), out_dtype=F32, tm=1024, tn=896, tk=D_MODEL, name="even_w_in")
    rw = _rwkv_mixer(p[:, :RW_IN], *rw_args).astype(BF16)
    pa = p[:, RW_IN:]
    nq = GQ_HEADS * GQ_HEAD
    nkv = GQ_KV_HEADS * GQ_HEAD
    q = _rms_f32(pa[:, :nq].reshape(N_ROWS, GQ_HEADS, GQ_HEAD), q_g)
    k = _rms_f32(pa[:, nq:nq + nkv].reshape(N_ROWS, GQ_KV_HEADS, GQ_HEAD), k_g)
    v = pa[:, nq + nkv:].astype(BF16)
    q = _rope_lat(q, rope).reshape(N_ROWS, nq).astype(BF16)
    k = _rope_lat(k, rope).reshape(N_ROWS, nkv).astype(BF16)
    at = _gqa_attention(q, k, v, latent=True)
    if need_ctx:
        at = jnp.concatenate([at, _gqa_attention(q, k, v, latent=False)], axis=0)
        return jnp.concatenate([rw, at], axis=-1)
    return jnp.concatenate([rw[:N_LAT_ROWS], at], axis=-1)


def _odd_mixer(a_all, rope, w_in, q_norm, q_up, kv_norm, kv_up, lq1, lk1, lq2, lk2, subln, layer_idx):
    w_in = jnp.pad(w_in.astype(BF16), ((0, 0), (0, IN_ODD_PADDED - IN_ODD)))
    p = _matmul(a_all, w_in, out_dtype=F32, tm=1024, tn=768, tk=D_MODEL, name="odd_w_in")
    lam_init = 0.8 - 0.6 * math.exp(-0.3 * layer_idx)
    lam = (jnp.exp(jnp.sum(lq1 * lk1).astype(F32)) - jnp.exp(jnp.sum(lq2 * lk2).astype(F32)) + lam_init)

    c_q = _rms_f32(p[:N_LAT_ROWS, :Q_LORA], q_norm).astype(BF16)
    c_kv = _rms_f32(p[:, Q_LORA:Q_LORA + KV_LORA], kv_norm).astype(BF16)
    k_pe = p[:, Q_LORA + KV_LORA:MLA_IN]
    q_up_h = q_up.reshape(Q_LORA, MLA_HEADS, MLA_NOPE + MLA_ROPE)
    q_up_r = jnp.concatenate([q_up_h[:, :, :MLA_NOPE].reshape(Q_LORA, -1),
                              q_up_h[:, :, MLA_NOPE:].reshape(Q_LORA, -1)], axis=1).astype(BF16)
    kv_up_h = kv_up.reshape(KV_LORA, MLA_HEADS, MLA_NOPE + MLA_V)
    kv_up_r = jnp.concatenate([kv_up_h[:, :, :MLA_NOPE].reshape(KV_LORA, -1),
                               kv_up_h[:, :, MLA_NOPE:].reshape(KV_LORA, -1)], axis=1).astype(BF16)
    q_all = _matmul(c_q, q_up_r, out_dtype=F32, tm=1024, tn=1536, tk=Q_LORA, name="mla_q_up")
    kv = _matmul(c_kv, kv_up_r, out_dtype=BF16, tm=1024, tn=2048, tk=KV_LORA, name="mla_kv_up")
    q_nope = q_all[:, :MLA_HEADS * MLA_NOPE].astype(BF16)
    q_pe = q_all[:, MLA_HEADS * MLA_NOPE:].reshape(N_LAT_ROWS, MLA_HEADS, MLA_ROPE)
    q_pe = _rope_lat(q_pe, rope).transpose(1, 0, 2).astype(BF16)
    k_pe = _rope_lat(k_pe.reshape(N_ROWS, 1, MLA_ROPE), rope).reshape(N_ROWS, MLA_ROPE).astype(BF16)
    m_out = _mla_attention(q_nope, q_pe, kv, k_pe)

    pd = p[:, MLA_IN:IN_ODD]
    qk_w = DIFF_HEADS * 2 * DIFF_HEAD
    dq = pd[:N_LAT_ROWS, :qk_w].reshape(N_LAT_ROWS, DIFF_HEADS * 2, DIFF_HEAD)
    dk = pd[:, qk_w:2 * qk_w].reshape(N_ROWS, DIFF_HEADS * 2, DIFF_HEAD)
    dv = pd[:, 2 * qk_w:].astype(BF16)
    dq = _rope_lat(dq, rope).reshape(N_LAT_ROWS, qk_w).astype(BF16)
    dk = _rope_lat(dk, rope).reshape(N_ROWS, qk_w).astype(BF16)
    d_out = _diff_attention(lam.reshape(1), dq, dk, dv, subln, 1.0 - lam_init)
    return jnp.concatenate([m_out, d_out], axis=-1)


def kernel(x, c, ctx, c_ctx, ada_w, ada_b, norm1_g, norm2_g, mlp_w1, mlp_w2, final_g, ev_w_in, ev_w_out, rw_mu, rw_w0, rw_w_up, rw_a0, rw_a_up, rw_g_up, rw_k_k, rw_k_a, rw_r_k, rw_ln_w, rw_ln_b, gq_q_norm, gq_k_norm, od_w_in, od_w_out, mla_q_norm, mla_q_up, mla_kv_norm, mla_kv_up, diff_lq1, diff_lk1, diff_lq2, diff_lk2, diff_subln):
    rope_gq = _axial_rope_tables(SEQ, GQ_HEAD)
    rope_64 = _axial_rope_tables(SEQ, MLA_ROPE)
    h = jnp.concatenate([x.reshape(N_LAT_ROWS, D_MODEL), ctx.reshape(N_CTX_ROWS, D_MODEL)], axis=0)
    cond = jnp.concatenate([c_ctx[None], c, jnp.zeros((3, D_MODEL), F32)], axis=0)
    cond = jax.nn.silu(cond).astype(BF16)
    for i in range(DEPTH):
        last = i == DEPTH - 1
        mod = _matmul(cond, ada_w[i], out_dtype=F32, tm=8, tn=2048, tk=1024, name="adaln_mod")
        mod = (mod + ada_b[i])[:BATCH + 1].reshape(BATCH + 1, N_MOD, 1, D_MODEL)
        sh1, sc1, g1, sh2, sc2, g2 = (mod[:, m] for m in range(N_MOD))
        a_all = _norm(h, norm1_g[i], sc1, sh1, rows=N_ROWS, out_dtype=BF16)
        j = i // 2
        if i % 2 == 0:
            rw_args = (rw_mu[j], rw_w0[j], rw_w_up[j], rw_a0[j], rw_a_up[j], rw_g_up[j], rw_k_k[j], rw_k_a[j],
                       rw_r_k[j], rw_ln_w[j], rw_ln_b[j])
            mix = _even_mixer(a_all, rope_gq, ev_w_in[j], rw_args, gq_q_norm[j], gq_k_norm[j], not last)
            w_out = ev_w_out[j]
        else:
            if not last:
                raise NotImplementedError("context rows of an odd layer are only needed when a layer follows")
            mix = _odd_mixer(a_all, rope_64, od_w_in[j], mla_q_norm[j], mla_q_up[j], mla_kv_norm[j],
                             mla_kv_up[j], diff_lq1[j], diff_lk1[j], diff_lq2[j], diff_lk2[j], diff_subln[j], i)
            w_out = od_w_out[j]
        rows = N_LAT_ROWS if last else N_ROWS
        h = _matmul(mix, w_out.astype(BF16), out_dtype=F32, tm=1024, tn=1024, tk=2048, rows=rows,
                    epilogue="gated_residual", res=h, gate=g1, name="mixer_w_out")
        a2 = _norm(h, norm2_g[i], sc2, sh2, rows=rows, out_dtype=BF16)
        hid = _matmul(a2, mlp_w1[i].astype(BF16), out_dtype=BF16, tm=1024, tn=1024, tk=D_MODEL,
                      epilogue="relu2", name="mlp_w1")
        h = _matmul(hid, mlp_w2[i].astype(BF16), out_dtype=F32, tm=1024, tn=1024, tk=2048,
                    epilogue="gated_residual", res=h, gate=g2, name="mlp_w2")
    out = _norm(h, final_g, rows=N_LAT_ROWS, out_dtype=F32)
    return out.reshape(BATCH, SEQ, D_MODEL)
```

```python
import functools
import math

import jax
import jax.numpy as jnp
from jax import lax
from jax.experimental import pallas as pl
from jax.experimental.pallas import tpu as pltpu

D_MODEL = 4096
BATCH = 4
SEQ = 4096
DEPTH = 2
CTX_LEN = 256
GRID_W = 64
ROPE_THETA = 10000.0
NORM_EPS = 1e-6
MLP_HIDDEN = 4 * D_MODEL
N_MOD = 6
HALF = D_MODEL // 2

RW_HEAD = 64
RW_W = HALF
RW_HEADS = RW_W // RW_HEAD
DECAY_RANK = 96
AAA_RANK = 96
GATE_RANK = 256
GN_EPS = 64e-5
RW_IN = 3 * RW_W + GATE_RANK + 2 * DECAY_RANK + 2 * AAA_RANK

GQ_HEAD = 128
GQ_HEADS = HALF // GQ_HEAD
GQ_KV_HEADS = GQ_HEADS // 4
GQ_GROUP = GQ_HEADS // GQ_KV_HEADS
GQ_SCALE = GQ_HEAD ** -0.5
IN_EVEN = RW_IN + (GQ_HEADS + 2 * GQ_KV_HEADS) * GQ_HEAD

MLA_NOPE = 128
MLA_ROPE = 64
MLA_V = 128
MLA_HEADS = HALF // MLA_V
Q_LORA = 768
KV_LORA = 512
MLA_SCALE = (MLA_NOPE + MLA_ROPE) ** -0.5
MLA_IN = Q_LORA + KV_LORA + MLA_ROPE

DIFF_HEAD = 64
DIFF_V = 2 * DIFF_HEAD
DIFF_HEADS = HALF // DIFF_V
DIFF_SCALE = DIFF_HEAD ** -0.5
DIFF_IN = DIFF_HEADS * (4 * DIFF_HEAD + DIFF_V)
IN_ODD = MLA_IN + DIFF_IN
IN_ODD_PADDED = 7680

N_LAT_ROWS = BATCH * SEQ
N_CTX_ROWS = BATCH * CTX_LEN
N_ROWS = N_LAT_ROWS + N_CTX_ROWS
T_ALL = CTX_LEN + SEQ
CTX_BLOCK0 = N_LAT_ROWS // CTX_LEN

V7X_VMEM_LIMIT_BYTES = 56 * 1024 * 1024

BF16 = jnp.bfloat16
F32 = jnp.float32


def _params(*sem):
    return pltpu.CompilerParams(dimension_semantics=sem, vmem_limit_bytes=V7X_VMEM_LIMIT_BYTES)


def _row_group(row_tile, tile_rows):
    start = row_tile * tile_rows
    return jnp.where(start < N_LAT_ROWS, 1 + start // SEQ, 0)


def _norm_kernel(x_ref, g_ref, *rest, modulate):
    if modulate:
        sc_ref, sh_ref, o_ref = rest
    else:
        (o_ref,) = rest
    x = x_ref[...]
    ms = jnp.mean(x * x, axis=-1, keepdims=True)
    y = x * lax.rsqrt(ms + NORM_EPS) * g_ref[...]
    if modulate:
        y = y * (1.0 + sc_ref[0]) + sh_ref[0]
    o_ref[...] = y.astype(o_ref.dtype)


def _norm(x, g, sc=None, sh=None, *, rows, out_dtype, tile=256):
    d = x.shape[1]
    modulate = sc is not None
    in_specs = [pl.BlockSpec((tile, d), lambda i: (i, 0)),
                pl.BlockSpec((1, d), lambda i: (0, 0))]
    args = [x, g.reshape(1, d)]
    if modulate:
        mod_spec = pl.BlockSpec((1, 1, d), lambda i: (_row_group(i, tile), 0, 0))
        in_specs += [mod_spec, mod_spec]
        args += [sc, sh]
    return pl.pallas_call(
        functools.partial(_norm_kernel, modulate=modulate),
        grid=(rows // tile,),
        in_specs=in_specs,
        out_specs=pl.BlockSpec((tile, d), lambda i: (i, 0)),
        out_shape=jax.ShapeDtypeStruct((rows, d), out_dtype),
        compiler_params=_params("parallel"),
        name="rmsnorm_mod" if modulate else "rmsnorm",
    )(*args)


def _mm_kernel(a_ref, w_ref, *rest, epilogue, nk):
    if epilogue == "gated_residual":
        res_ref, gate_ref, o_ref, acc_ref = rest
    else:
        o_ref, acc_ref = rest
    k = pl.program_id(2)
    part = jnp.dot(a_ref[...].astype(BF16), w_ref[...].astype(BF16), preferred_element_type=F32)

    @pl.when(k == 0)
    def _():
        acc_ref[...] = part

    @pl.when(k > 0)
    def _():
        acc_ref[...] += part

    @pl.when(k == nk - 1)
    def _():
        acc = acc_ref[...]
        if epilogue == "relu2":
            acc = jnp.square(jnp.maximum(acc, 0.0))
        elif epilogue == "gated_residual":
            acc = res_ref[...] + gate_ref[0] * acc
        o_ref[...] = acc.astype(o_ref.dtype)


def _matmul(a, w, *, out_dtype, tm, tn, tk, rows=None, epilogue=None, res=None, gate=None, name):
    kdim, n = w.shape
    m = a.shape[0] if rows is None else rows
    assert a.shape[1] == kdim and m % tm == 0 and n % tn == 0 and kdim % tk == 0
    nk = kdim // tk
    in_specs = [pl.BlockSpec((tm, tk), lambda i, j, k: (i, k)),
                pl.BlockSpec((tk, tn), lambda i, j, k: (k, j))]
    args = [a, w]
    if epilogue == "gated_residual":
        in_specs += [pl.BlockSpec((tm, tn), lambda i, j, k: (i, j)),
                     pl.BlockSpec((1, 1, tn), lambda i, j, k: (_row_group(i, tm), 0, j))]
        args += [res, gate]
    return pl.pallas_call(
        functools.partial(_mm_kernel, epilogue=epilogue, nk=nk),
        grid=(m // tm, n // tn, nk),
        in_specs=in_specs,
        out_specs=pl.BlockSpec((tm, tn), lambda i, j, k: (i, j)),
        out_shape=jax.ShapeDtypeStruct((m, n), out_dtype),
        scratch_shapes=[pltpu.VMEM((tm, tn), F32)],
        compiler_params=_params("parallel", "parallel", "arbitrary"),
        name=name,
    )(*args)


def _nt_dot(a, b):
    return lax.dot_general(a, b, (((1,), (1,)), ((), ())), preferred_element_type=F32)


def _softmax_parts(s_list):
    m = functools.reduce(jnp.maximum, [jnp.max(s, axis=-1, keepdims=True) for s in s_list])
    p_list = [jnp.exp(s - m) for s in s_list]
    l = functools.reduce(jnp.add, [jnp.sum(p, axis=-1, keepdims=True) for p in p_list])
    return p_list, l


def _pv(p_list, v_refs):
    return functools.reduce(jnp.add, [jnp.dot(p.astype(BF16), v_ref[...], preferred_element_type=F32)
                                      for p, v_ref in zip(p_list, v_refs)])


def _gqa_kernel(q_ref, *rest, with_lat):
    if with_lat:
        kc_ref, vc_ref, kl_ref, vl_ref, o_ref = rest
        k_refs, v_refs = (kc_ref, kl_ref), (vc_ref, vl_ref)
    else:
        kc_ref, vc_ref, o_ref = rest
        k_refs, v_refs = (kc_ref,), (vc_ref,)
    for g in range(GQ_GROUP):
        q = q_ref[:, g * GQ_HEAD:(g + 1) * GQ_HEAD]
        s_list = [_nt_dot(q, k_ref[...]) * GQ_SCALE for k_ref in k_refs]
        p_list, l = _softmax_parts(s_list)
        o_ref[:, g * GQ_HEAD:(g + 1) * GQ_HEAD] = (_pv(p_list, v_refs) / l).astype(o_ref.dtype)


def _gqa_attention(q, k, v, *, latent, tq=256):
    gw = GQ_GROUP * GQ_HEAD
    ctx_kv = pl.BlockSpec((CTX_LEN, GQ_HEAD), lambda b, n, i: (CTX_BLOCK0 + b, n))
    if latent:
        per_b = SEQ // tq
        lat_kv = pl.BlockSpec((SEQ, GQ_HEAD), lambda b, n, i: (b, n))
        in_specs = [pl.BlockSpec((tq, gw), lambda b, n, i: (b * per_b + i, n)), ctx_kv, ctx_kv, lat_kv, lat_kv]
        args = (q, k, v, k, v)
        rows = N_LAT_ROWS
    else:
        per_b = CTX_LEN // tq
        q_blk0 = N_LAT_ROWS // tq
        in_specs = [pl.BlockSpec((tq, gw), lambda b, n, i: (q_blk0 + b * per_b + i, n)), ctx_kv, ctx_kv]
        args = (q, k, v)
        rows = N_CTX_ROWS
    return pl.pallas_call(
        functools.partial(_gqa_kernel, with_lat=latent),
        grid=(BATCH, GQ_KV_HEADS, per_b),
        in_specs=in_specs,
        out_specs=pl.BlockSpec((tq, gw), lambda b, n, i: (b * per_b + i, n)),
        out_shape=jax.ShapeDtypeStruct((rows, GQ_HEADS * GQ_HEAD), BF16),
        compiler_params=_params("parallel", "parallel", "arbitrary"),
        name="gqa_attention_lat" if latent else "gqa_attention_ctx",
    )(*args)


def _mla_kernel(qn_ref, qp_ref, knc_ref, kpc_ref, vc_ref, knl_ref, kpl_ref, vl_ref, o_ref):
    qn = qn_ref[...]
    qp = qp_ref[...]
    s_c = (_nt_dot(qn, knc_ref[...]) + _nt_dot(qp, kpc_ref[...])) * MLA_SCALE
    s_l = (_nt_dot(qn, knl_ref[...]) + _nt_dot(qp, kpl_ref[...])) * MLA_SCALE
    p_list, l = _softmax_parts([s_c, s_l])
    o_ref[...] = (_pv(p_list, (vc_ref, vl_ref)) / l).astype(o_ref.dtype)


def _mla_attention(q_nope, q_pe, kv, k_pe, *, tq=256):
    per_b = SEQ // tq
    in_specs = [
        pl.BlockSpec((tq, MLA_NOPE), lambda b, h, i: (b * per_b + i, h)),
        pl.BlockSpec((None, tq, MLA_ROPE), lambda b, h, i: (h, b * per_b + i, 0)),
        pl.BlockSpec((CTX_LEN, MLA_NOPE), lambda b, h, i: (CTX_BLOCK0 + b, h)),
        pl.BlockSpec((CTX_LEN, MLA_ROPE), lambda b, h, i: (CTX_BLOCK0 + b, 0)),
        pl.BlockSpec((CTX_LEN, MLA_V), lambda b, h, i: (CTX_BLOCK0 + b, MLA_HEADS + h)),
        pl.BlockSpec((SEQ, MLA_NOPE), lambda b, h, i: (b, h)),
        pl.BlockSpec((SEQ, MLA_ROPE), lambda b, h, i: (b, 0)),
        pl.BlockSpec((SEQ, MLA_V), lambda b, h, i: (b, MLA_HEADS + h)),
    ]
    return pl.pallas_call(
        _mla_kernel,
        grid=(BATCH, MLA_HEADS, per_b),
        in_specs=in_specs,
        out_specs=pl.BlockSpec((tq, MLA_V), lambda b, h, i: (b * per_b + i, h)),
        out_shape=jax.ShapeDtypeStruct((N_LAT_ROWS, MLA_HEADS * MLA_V), BF16),
        compiler_params=_params("parallel", "parallel", "arbitrary"),
        name="mla_attention",
    )(q_nope, q_pe, kv, k_pe, kv, kv, k_pe, kv)


def _diff_kernel(lam_ref, q_ref, kc_ref, vc_ref, kl_ref, vl_ref, g_ref, o_ref, *, out_scale):
    lam = lam_ref[0]
    q = q_ref[...]
    first = lax.broadcasted_iota(jnp.int32, q.shape, 1) < DIFF_HEAD
    zero = jnp.zeros_like(q)
    q1 = jnp.where(first, q, zero)
    q2 = jnp.where(first, zero, q)
    kc = kc_ref[...]
    kl = kl_ref[...]
    (p1c, p1l), l1 = _softmax_parts([_nt_dot(q1, kc) * DIFF_SCALE, _nt_dot(q1, kl) * DIFF_SCALE])
    (p2c, p2l), l2 = _softmax_parts([_nt_dot(q2, kc) * DIFF_SCALE, _nt_dot(q2, kl) * DIFF_SCALE])
    r1 = 1.0 / l1
    r2 = lam / l2
    o = _pv([p1c * r1 - p2c * r2, p1l * r1 - p2l * r2], (vc_ref, vl_ref))
    ms = jnp.mean(o * o, axis=-1, keepdims=True)
    o = o * lax.rsqrt(ms + NORM_EPS) * g_ref[...]
    o_ref[...] = (o * out_scale).astype(o_ref.dtype)


def _diff_attention(lam, q, k, v, subln, out_scale, *, tq=256):
    per_b = SEQ // tq
    ctx_kv = pl.BlockSpec((CTX_LEN, DIFF_V), lambda b, h, i: (CTX_BLOCK0 + b, h))
    lat_kv = pl.BlockSpec((SEQ, DIFF_V), lambda b, h, i: (b, h))
    in_specs = [
        pl.BlockSpec(memory_space=pltpu.SMEM),
        pl.BlockSpec((tq, DIFF_V), lambda b, h, i: (b * per_b + i, h)),
        ctx_kv, ctx_kv, lat_kv, lat_kv,
        pl.BlockSpec((1, DIFF_V), lambda b, h, i: (0, 0)),
    ]
    return pl.pallas_call(
        functools.partial(_diff_kernel, out_scale=out_scale),
        grid=(BATCH, DIFF_HEADS, per_b),
        in_specs=in_specs,
        out_specs=pl.BlockSpec((tq, DIFF_V), lambda b, h, i: (b * per_b + i, h)),
        out_shape=jax.ShapeDtypeStruct((N_LAT_ROWS, DIFF_HEADS * DIFF_V), BF16),
        compiler_params=_params("parallel", "parallel", "arbitrary"),
        name="diff_attention",
    )(lam, q, k, v, k, v, subln.reshape(1, DIFF_V))


RW_CHUNK = 128
RW_PAIRS = RW_HEADS // 2
LANES = 2 * RW_HEAD
SUB = 8
RW_UNROLL = 8
RW_GROUP = 8


def _rwkv_kernel(r_ref, kk_ref, v_ref, w_ref, k_ref, b_ref, y_ref, s_ref):
    d = pl.program_id(0)
    c = pl.program_id(2)

    @pl.when(c == 0)
    def _():
        s_ref[...] = jnp.zeros_like(s_ref)

    v_hi = lax.broadcasted_iota(jnp.int32, (SUB, SUB, LANES), 0)
    v_lo = lax.broadcasted_iota(jnp.int32, (SUB, SUB, LANES), 1)
    lane3 = lax.broadcasted_iota(jnp.int32, (SUB, SUB, LANES), 2)
    dup = jnp.where((lane3 & (RW_HEAD - 1)) == v_hi * SUB + v_lo, 1.0, 0.0)
    blk_r = lax.broadcasted_iota(jnp.int32, (LANES, LANES), 0) // RW_HEAD
    blk_c = lax.broadcasted_iota(jnp.int32, (LANES, LANES), 1) // RW_HEAD
    head_ones = jnp.where(blk_r == blk_c, 1.0, 0.0).astype(BF16)
    slot_of_lane = lax.broadcasted_iota(jnp.int32, (SUB, LANES), 1) & (RW_HEAD - 1)
    step_dir = jnp.where(d == 0, 1, -1)

    y_ref[...] = jnp.zeros_like(y_ref)

    def row(ref, p, t, *lead):
        return ref[(*lead, p, pl.ds(t, SUB, stride=0), slice(None))]

    def as_rows(tile):
        return tile.reshape(RW_HEAD, LANES).astype(BF16)

    def head_sums(states, pairs, t_done, t_next):
        rows = []
        for p in pairs:
            rows.append(as_rows(states[p] * row(r_ref, p, t_done)))
            if t_next is not None:
                rows.append(as_rows(states[p] * row(kk_ref, p, t_next)))
                rows.append(as_rows(dup * row(v_ref, p, t_next)))
        return jnp.dot(jnp.concatenate(rows, axis=0), head_ones, preferred_element_type=F32)

    def slab(sums, q, n_slabs, i):
        lo = (q * n_slabs + i) * RW_HEAD
        return sums[lo:lo + RW_HEAD].reshape(SUB, SUB, LANES)

    def put_y(p, y_b, t, valid):
        window = t // RW_HEAD
        keep = (slot_of_lane == t % RW_HEAD) & valid
        y_ref[0, p, window] = jnp.where(keep, y_b, y_ref[0, p, window])

    groups = [range(g0, g0 + RW_GROUP) for g0 in range(0, RW_PAIRS, RW_GROUP)]

    def block(g, carry):
        first = g * RW_UNROLL
        first = jnp.where(d == 0, first, RW_CHUNK - 1 - first)
        times = [first + u * step_dir for u in range(RW_UNROLL)]
        t_before = jnp.clip(first - step_dir, 0, RW_CHUNK - 1)
        states = [s_ref[p] for p in range(RW_PAIRS)]
        sums = [head_sums(states, grp, t_before, times[0]) for grp in groups]
        for gi, grp in enumerate(groups):
            for q, p in enumerate(grp):
                put_y(p, slab(sums[gi], q, 3, 0), t_before, g > 0)
        for u in range(RW_UNROLL):
            t = times[u]
            for gi, grp in enumerate(groups):
                for q, p in enumerate(grp):
                    sa = slab(sums[gi], q, 3, 1)
                    vb = slab(sums[gi], q, 3, 2)
                    states[p] = (states[p] * row(w_ref, p, t, 0) - sa * row(b_ref, p, t, 0)
                                 + vb * row(k_ref, p, t, 0))
                if u + 1 < RW_UNROLL:
                    sums[gi] = head_sums(states, grp, t, times[u + 1])
                    for q, p in enumerate(grp):
                        put_y(p, slab(sums[gi], q, 3, 0), t, True)
        for p in range(RW_PAIRS):
            s_ref[p] = states[p]
        return carry

    lax.fori_loop(0, RW_CHUNK // RW_UNROLL, block, 0)

    t_last = jnp.where(d == 0, RW_CHUNK - 1, 0)
    states = [s_ref[p] for p in range(RW_PAIRS)]
    for grp in groups:
        last = head_sums(states, grp, t_last, None)
        for q, p in enumerate(grp):
            put_y(p, slab(last, q, 1, 0), t_last, True)


def _rwkv_scan(r, kk, v, w, k, b):
    n_ctx = CTX_LEN // RW_CHUNK
    n_lat = SEQ // RW_CHUNK
    ctx0 = N_LAT_ROWS // RW_CHUNK

    def chunk(d, bb, c):
        j = jnp.where(d == 0, c, jnp.where(c < n_ctx, n_ctx - 1 - c, n_ctx + n_lat - 1 - (c - n_ctx)))
        return jnp.where(j < n_ctx, ctx0 + bb * n_ctx + j, bb * n_lat + j - n_ctx)

    shared = pl.BlockSpec((RW_PAIRS, RW_CHUNK, LANES), lambda d, bb, c: (0, chunk(d, bb, c), 0))
    per_dir = pl.BlockSpec((1, RW_PAIRS, RW_CHUNK, LANES), lambda d, bb, c: (d, 0, chunk(d, bb, c), 0))
    return pl.pallas_call(
        _rwkv_kernel,
        grid=(2, BATCH, n_ctx + n_lat),
        in_specs=[shared, shared, shared, per_dir, per_dir, per_dir],
        out_specs=pl.BlockSpec((1, RW_PAIRS, RW_CHUNK // RW_HEAD, SUB, SUB, LANES),
                               lambda d, bb, c: (d, 0, chunk(d, bb, c), 0, 0, 0)),
        out_shape=jax.ShapeDtypeStruct((2, RW_PAIRS, N_ROWS // RW_HEAD, SUB, SUB, LANES), F32),
        scratch_shapes=[pltpu.VMEM((RW_PAIRS, RW_HEAD // SUB, SUB, LANES), F32)],
        compiler_params=_params("parallel", "parallel", "arbitrary"),
        name="rwkv7_scan",
    )(r, kk, v, w, k, b)


def _axial_rope_tables(n_tokens, rot_dim):
    n_rows = n_tokens // GRID_W
    row = jnp.repeat(jnp.arange(n_rows, dtype=F32), GRID_W)
    col = jnp.tile(jnp.arange(GRID_W, dtype=F32), n_rows)
    axis_dim = rot_dim // 2
    inv_freq = ROPE_THETA ** (-jnp.arange(0, axis_dim, 2, dtype=F32) / axis_dim)
    ang_r = row[:, None] * inv_freq
    ang_c = col[:, None] * inv_freq
    ang = jnp.concatenate([ang_r, ang_r, ang_c, ang_c], axis=-1)
    return jnp.cos(ang), jnp.sin(ang)


def _rotate_half(z):
    z1, z2 = jnp.split(z, 2, axis=-1)
    return jnp.concatenate([-z2, z1], axis=-1)


def _apply_rope(x, cos, sin):
    half = x.shape[-1] // 2
    rot = jnp.concatenate([_rotate_half(x[..., :half]), _rotate_half(x[..., half:])], axis=-1)
    return x * cos + rot * sin


def _rms_f32(x, g):
    return x * lax.rsqrt(jnp.mean(x * x, axis=-1, keepdims=True) + NORM_EPS) * g


def _rope_lat(x, rope):
    cos, sin = rope
    heads, dim = x.shape[1:]
    lat = x[:N_LAT_ROWS].reshape(BATCH, SEQ, heads, dim)
    lat = _apply_rope(lat, cos[None, :, None, :], sin[None, :, None, :]).reshape(N_LAT_ROWS, heads, dim)
    return lat if x.shape[0] == N_LAT_ROWS else jnp.concatenate([lat, x[N_LAT_ROWS:]], axis=0)


def _centred_shift_rows(p):
    def shift(z):
        prev = jnp.pad(z[:, :-1], ((0, 0), (1, 0), (0, 0)))
        nxt = jnp.pad(z[:, 1:], ((0, 0), (0, 1), (0, 0)))
        return 0.5 * (prev + nxt)
    c = p.shape[-1]
    lat = shift(p[:N_LAT_ROWS].reshape(BATCH, SEQ, c)).reshape(N_LAT_ROWS, c)
    ctx = shift(p[N_LAT_ROWS:].reshape(BATCH, CTX_LEN, c)).reshape(N_CTX_ROWS, c)
    return jnp.concatenate([lat, ctx], axis=0)


def _small_mm(a, w, name):
    return _matmul(a.astype(BF16), w.astype(BF16), out_dtype=F32, tm=1024, tn=w.shape[1], tk=w.shape[0], name=name)


def _rwkv_mixer(pr, mu, w0, w_up, a0, a_up, g_up, k_k, k_a, r_k, ln_w, ln_b):
    pr = pr + (_centred_shift_rows(pr) - pr) * mu
    r = pr[:, :RW_W]
    k = pr[:, RW_W:2 * RW_W]
    v = pr[:, 2 * RW_W:3 * RW_W]
    o = 3 * RW_W
    gd = pr[:, o:o + GATE_RANK]
    o += GATE_RANK
    wd = pr[:, o:o + 2 * DECAY_RANK].reshape(N_ROWS, 2, DECAY_RANK)
    o += 2 * DECAY_RANK
    ad = pr[:, o:o + 2 * AAA_RANK].reshape(N_ROWS, 2, AAA_RANK)
    g = _small_mm(jax.nn.sigmoid(gd), g_up, "rwkv_gate_up")
    wu = jnp.stack([_small_mm(jnp.tanh(wd[:, d]), w_up[d], "rwkv_decay_up") for d in range(2)])
    au = jnp.stack([_small_mm(ad[:, d], a_up[d], "rwkv_aaa_up") for d in range(2)])
    w_log = -jax.nn.softplus(-(w0[:, None, :] + wu)) - 0.5
    decay = jnp.exp(-jnp.exp(w_log))
    a = jax.nn.sigmoid(a0[:, None, :] + au)
    kk = (k * k_k).reshape(N_ROWS, RW_HEADS, RW_HEAD)
    kk = kk / jnp.maximum(jnp.linalg.norm(kk, axis=-1, keepdims=True), 1e-12)
    kk = kk.reshape(N_ROWS, RW_W)
    k_dir = k[None] * (1.0 + (a - 1.0) * k_a)
    b_dir = kk[None] * a

    def pair_major(z):
        return jnp.swapaxes(z.reshape(z.shape[:-1] + (RW_PAIRS, LANES)), -3, -2)

    y2 = _rwkv_scan(pair_major(r), pair_major(kk), pair_major(v), pair_major(decay), pair_major(k_dir),
                    pair_major(b_dir))
    y = (y2[0] + y2[1]).reshape(RW_PAIRS, N_ROWS // RW_HEAD, RW_HEAD, 2, RW_HEAD)
    y = y.transpose(1, 4, 0, 3, 2).reshape(N_ROWS, RW_HEADS, RW_HEAD)

    mean = jnp.mean(y, axis=-1, keepdims=True)
    var = jnp.mean(jnp.square(y - mean), axis=-1, keepdims=True)
    yn = ((y - mean) * lax.rsqrt(var + GN_EPS)).reshape(N_ROWS, RW_W) * ln_w + ln_b
    rh = r.reshape(N_ROWS, RW_HEADS, RW_HEAD)
    coef = jnp.sum(rh[None] * k_dir.reshape(2, N_ROWS, RW_HEADS, RW_HEAD) * r_k, axis=(0, 3))
    bonus = (coef[..., None] * v.reshape(N_ROWS, RW_HEADS, RW_HEAD)).reshape(N_ROWS, RW_W)
    return (yn + bonus) * g


def _even_mixer(a_all, rope, w_in, rw_args, q_g, k_g, need_ctx):
    p = _matmul(a_all, w_in.astype(BF16), out_dtype=F32, tm=1024, tn=896, tk=D_MODEL, name="even_w_in")
    rw = _rwkv_mixer(p[:, :RW_IN], *rw_args).astype(BF16)
    pa = p[:, RW_IN:]
    nq = GQ_HEADS * GQ_HEAD
    nkv = GQ_KV_HEADS * GQ_HEAD
    q = _rms_f32(pa[:, :nq].reshape(N_ROWS, GQ_HEADS, GQ_HEAD), q_g)
    k = _rms_f32(pa[:, nq:nq + nkv].reshape(N_ROWS, GQ_KV_HEADS, GQ_HEAD), k_g)
    v = pa[:, nq + nkv:].astype(BF16)
    q = _rope_lat(q, rope).reshape(N_ROWS, nq).astype(BF16)
    k = _rope_lat(k, rope).reshape(N_ROWS, nkv).astype(BF16)
    at = _gqa_attention(q, k, v, latent=True)
    if need_ctx:
        at = jnp.concatenate([at, _gqa_attention(q, k, v, latent=False)], axis=0)
        return jnp.concatenate([rw, at], axis=-1)
    return jnp.concatenate([rw[:N_LAT_ROWS], at], axis=-1)


def _odd_mixer(a_all, rope, w_in, q_norm, q_up, kv_norm, kv_up, lq1, lk1, lq2, lk2, subln, layer_idx):
    w_in = jnp.pad(w_in.astype(BF16), ((0, 0), (0, IN_ODD_PADDED - IN_ODD)))
    p = _matmul(a_all, w_in, out_dtype=F32, tm=1024, tn=768, tk=D_MODEL, name="odd_w_in")
    lam_init = 0.8 - 0.6 * math.exp(-0.3 * layer_idx)
    lam = (jnp.exp(jnp.sum(lq1 * lk1).astype(F32)) - jnp.exp(jnp.sum(lq2 * lk2).astype(F32)) + lam_init)

    c_q = _rms_f32(p[:N_LAT_ROWS, :Q_LORA], q_norm).astype(BF16)
    c_kv = _rms_f32(p[:, Q_LORA:Q_LORA + KV_LORA], kv_norm).astype(BF16)
    k_pe = p[:, Q_LORA + KV_LORA:MLA_IN]
    q_up_h = q_up.reshape(Q_LORA, MLA_HEADS, MLA_NOPE + MLA_ROPE)
    q_up_r = jnp.concatenate([q_up_h[:, :, :MLA_NOPE].reshape(Q_LORA, -1),
                              q_up_h[:, :, MLA_NOPE:].reshape(Q_LORA, -1)], axis=1).astype(BF16)
    kv_up_h = kv_up.reshape(KV_LORA, MLA_HEADS, MLA_NOPE + MLA_V)
    kv_up_r = jnp.concatenate([kv_up_h[:, :, :MLA_NOPE].reshape(KV_LORA, -1),
                               kv_up_h[:, :, MLA_NOPE:].reshape(KV_LORA, -1)], axis=1).astype(BF16)
    q_all = _matmul(c_q, q_up_r, out_dtype=F32, tm=1024, tn=1536, tk=Q_LORA, name="mla_q_up")
    kv = _matmul(c_kv, kv_up_r, out_dtype=BF16, tm=1024, tn=2048, tk=KV_LORA, name="mla_kv_up")
    q_nope = q_all[:, :MLA_HEADS * MLA_NOPE].astype(BF16)
    q_pe = q_all[:, MLA_HEADS * MLA_NOPE:].reshape(N_LAT_ROWS, MLA_HEADS, MLA_ROPE)
    q_pe = _rope_lat(q_pe, rope).transpose(1, 0, 2).astype(BF16)
    k_pe = _rope_lat(k_pe.reshape(N_ROWS, 1, MLA_ROPE), rope).reshape(N_ROWS, MLA_ROPE).astype(BF16)
    m_out = _mla_attention(q_nope, q_pe, kv, k_pe)

    pd = p[:, MLA_IN:IN_ODD]
    qk_w = DIFF_HEADS * 2 * DIFF_HEAD
    dq = pd[:N_LAT_ROWS, :qk_w].reshape(N_LAT_ROWS, DIFF_HEADS * 2, DIFF_HEAD)
    dk = pd[:, qk_w:2 * qk_w].reshape(N_ROWS, DIFF_HEADS * 2, DIFF_HEAD)
    dv = pd[:, 2 * qk_w:].astype(BF16)
    dq = _rope_lat(dq, rope).reshape(N_LAT_ROWS, qk_w).astype(BF16)
    dk = _rope_lat(dk, rope).reshape(N_ROWS, qk_w).astype(BF16)
    d_out = _diff_attention(lam.reshape(1), dq, dk, dv, subln, 1.0 - lam_init)
    return jnp.concatenate([m_out, d_out], axis=-1)


def kernel(x, c, ctx, c_ctx, ada_w, ada_b, norm1_g, norm2_g, mlp_w1, mlp_w2, final_g, ev_w_in, ev_w_out, rw_mu, rw_w0, rw_w_up, rw_a0, rw_a_up, rw_g_up, rw_k_k, rw_k_a, rw_r_k, rw_ln_w, rw_ln_b, gq_q_norm, gq_k_norm, od_w_in, od_w_out, mla_q_norm, mla_q_up, mla_kv_norm, mla_kv_up, diff_lq1, diff_lk1, diff_lq2, diff_lk2, diff_subln):
    rope_gq = _axial_rope_tables(SEQ, GQ_HEAD)
    rope_64 = _axial_rope_tables(SEQ, MLA_ROPE)
    h = jnp.concatenate([x.reshape(N_LAT_ROWS, D_MODEL), ctx.reshape(N_CTX_ROWS, D_MODEL)], axis=0)
    cond = jnp.concatenate([c_ctx[None], c, jnp.zeros((3, D_MODEL), F32)], axis=0)
    cond = jax.nn.silu(cond).astype(BF16)
    for i in range(DEPTH):
        last = i == DEPTH - 1
        mod = _matmul(cond, ada_w[i], out_dtype=F32, tm=8, tn=2048, tk=1024, name="adaln_mod")
        mod = (mod + ada_b[i])[:BATCH + 1].reshape(BATCH + 1, N_MOD, 1, D_MODEL)
        sh1, sc1, g1, sh2, sc2, g2 = (mod[:, m] for m in range(N_MOD))
        a_all = _norm(h, norm1_g[i], sc1, sh1, rows=N_ROWS, out_dtype=BF16)
        j = i // 2
        if i % 2 == 0:
            rw_args = (rw_mu[j], rw_w0[j], rw_w_up[j], rw_a0[j], rw_a_up[j], rw_g_up[j], rw_k_k[j], rw_k_a[j],
                       rw_r_k[j], rw_ln_w[j], rw_ln_b[j])
            mix = _even_mixer(a_all, rope_gq, ev_w_in[j], rw_args, gq_q_norm[j], gq_k_norm[j], not last)
            w_out = ev_w_out[j]
        else:
            if not last:
                raise NotImplementedError("context rows of an odd layer are only needed when a layer follows")
            mix = _odd_mixer(a_all, rope_64, od_w_in[j], mla_q_norm[j], mla_q_up[j], mla_kv_norm[j],
                             mla_kv_up[j], diff_lq1[j], diff_lk1[j], diff_lq2[j], diff_lk2[j], diff_subln[j], i)
            w_out = od_w_out[j]
        rows = N_LAT_ROWS if last else N_ROWS
        h = _matmul(mix, w_out.astype(BF16), out_dtype=F32, tm=1024, tn=1024, tk=2048, rows=rows,
                    epilogue="gated_residual", res=h, gate=g1, name="mixer_w_out")
        a2 = _norm(h, norm2_g[i], sc2, sh2, rows=rows, out_dtype=BF16)
        hid = _matmul(a2, mlp_w1[i].astype(BF16), out_dtype=BF16, tm=1024, tn=1024, tk=D_MODEL,
                      epilogue="relu2", name="mlp_w1")
        h = _matmul(hid, mlp_w2[i].astype(BF16), out_dtype=F32, tm=1024, tn=1024, tk=2048,
                    epilogue="gated_residual", res=h, gate=g2, name="mlp_w2")
    out = _norm(h, final_g, rows=N_LAT_ROWS, out_dtype=F32)
    return out.reshape(BATCH, SEQ, D_MODEL)
```

```python
import functools
import math

import jax
import jax.numpy as jnp
from jax import lax
from jax.experimental import pallas as pl
from jax.experimental.pallas import tpu as pltpu

D_MODEL = 4096
BATCH = 4
SEQ = 4096
DEPTH = 2
CTX_LEN = 256
GRID_W = 64
ROPE_THETA = 10000.0
NORM_EPS = 1e-6
MLP_HIDDEN = 4 * D_MODEL
N_MOD = 6
HALF = D_MODEL // 2

RW_HEAD = 64
RW_W = HALF
RW_HEADS = RW_W // RW_HEAD
DECAY_RANK = 96
AAA_RANK = 96
GATE_RANK = 256
GN_EPS = 64e-5
RW_IN = 3 * RW_W + GATE_RANK + 2 * DECAY_RANK + 2 * AAA_RANK

GQ_HEAD = 128
GQ_HEADS = HALF // GQ_HEAD
GQ_KV_HEADS = GQ_HEADS // 4
GQ_GROUP = GQ_HEADS // GQ_KV_HEADS
GQ_SCALE = GQ_HEAD ** -0.5
IN_EVEN = RW_IN + (GQ_HEADS + 2 * GQ_KV_HEADS) * GQ_HEAD

MLA_NOPE = 128
MLA_ROPE = 64
MLA_V = 128
MLA_HEADS = HALF // MLA_V
Q_LORA = 768
KV_LORA = 512
MLA_SCALE = (MLA_NOPE + MLA_ROPE) ** -0.5
MLA_IN = Q_LORA + KV_LORA + MLA_ROPE

DIFF_HEAD = 64
DIFF_V = 2 * DIFF_HEAD
DIFF_HEADS = HALF // DIFF_V
DIFF_SCALE = DIFF_HEAD ** -0.5
DIFF_IN = DIFF_HEADS * (4 * DIFF_HEAD + DIFF_V)
IN_ODD = MLA_IN + DIFF_IN
IN_ODD_PADDED = 7680

N_LAT_ROWS = BATCH * SEQ
N_CTX_ROWS = BATCH * CTX_LEN
N_ROWS = N_LAT_ROWS + N_CTX_ROWS
T_ALL = CTX_LEN + SEQ
CTX_BLOCK0 = N_LAT_ROWS // CTX_LEN

V7X_VMEM_LIMIT_BYTES = 56 * 1024 * 1024

BF16 = jnp.bfloat16
F32 = jnp.float32


def _params(*sem):
    return pltpu.CompilerParams(dimension_semantics=sem, vmem_limit_bytes=V7X_VMEM_LIMIT_BYTES)


def _row_group(row_tile, tile_rows):
    start = row_tile * tile_rows
    return jnp.where(start < N_LAT_ROWS, 1 + start // SEQ, 0)


def _norm_kernel(x_ref, g_ref, *rest, modulate):
    if modulate:
        sc_ref, sh_ref, o_ref = rest
    else:
        (o_ref,) = rest
    x = x_ref[...]
    ms = jnp.mean(x * x, axis=-1, keepdims=True)
    y = x * lax.rsqrt(ms + NORM_EPS) * g_ref[...]
    if modulate:
        y = y * (1.0 + sc_ref[0]) + sh_ref[0]
    o_ref[...] = y.astype(o_ref.dtype)


def _norm(x, g, sc=None, sh=None, *, rows, out_dtype, tile=256):
    d = x.shape[1]
    modulate = sc is not None
    in_specs = [pl.BlockSpec((tile, d), lambda i: (i, 0)),
                pl.BlockSpec((1, d), lambda i: (0, 0))]
    args = [x, g.reshape(1, d)]
    if modulate:
        mod_spec = pl.BlockSpec((1, 1, d), lambda i: (_row_group(i, tile), 0, 0))
        in_specs += [mod_spec, mod_spec]
        args += [sc, sh]
    return pl.pallas_call(
        functools.partial(_norm_kernel, modulate=modulate),
        grid=(rows // tile,),
        in_specs=in_specs,
        out_specs=pl.BlockSpec((tile, d), lambda i: (i, 0)),
        out_shape=jax.ShapeDtypeStruct((rows, d), out_dtype),
        compiler_params=_params("parallel"),
        name="rmsnorm_mod" if modulate else "rmsnorm",
    )(*args)


def _mm_kernel(a_ref, w_ref, *rest, epilogue, nk):
    if epilogue == "gated_residual":
        res_ref, gate_ref, o_ref, acc_ref = rest
    else:
        o_ref, acc_ref = rest
    k = pl.program_id(2)
    part = jnp.dot(a_ref[...].astype(BF16), w_ref[...].astype(BF16), preferred_element_type=F32)

    @pl.when(k == 0)
    def _():
        acc_ref[...] = part

    @pl.when(k > 0)
    def _():
        acc_ref[...] += part

    @pl.when(k == nk - 1)
    def _():
        acc = acc_ref[...]
        if epilogue == "relu2":
            acc = jnp.square(jnp.maximum(acc, 0.0))
        elif epilogue == "gated_residual":
            acc = res_ref[...] + gate_ref[0] * acc
        o_ref[...] = acc.astype(o_ref.dtype)


def _matmul(a, w, *, out_dtype, tm, tn, tk, rows=None, epilogue=None, res=None, gate=None, name):
    kdim, n = w.shape
    m = a.shape[0] if rows is None else rows
    assert a.shape[1] == kdim and m % tm == 0 and n % tn == 0 and kdim % tk == 0
    nk = kdim // tk
    in_specs = [pl.BlockSpec((tm, tk), lambda i, j, k: (i, k)),
                pl.BlockSpec((tk, tn), lambda i, j, k: (k, j))]
    args = [a, w]
    if epilogue == "gated_residual":
        in_specs += [pl.BlockSpec((tm, tn), lambda i, j, k: (i, j)),
                     pl.BlockSpec((1, 1, tn), lambda i, j, k: (_row_group(i, tm), 0, j))]
        args += [res, gate]
    return pl.pallas_call(
        functools.partial(_mm_kernel, epilogue=epilogue, nk=nk),
        grid=(m // tm, n // tn, nk),
        in_specs=in_specs,
        out_specs=pl.BlockSpec((tm, tn), lambda i, j, k: (i, j)),
        out_shape=jax.ShapeDtypeStruct((m, n), out_dtype),
        scratch_shapes=[pltpu.VMEM((tm, tn), F32)],
        compiler_params=_params("parallel", "parallel", "arbitrary"),
        name=name,
    )(*args)


def _nt_dot(a, b):
    return lax.dot_general(a, b, (((1,), (1,)), ((), ())), preferred_element_type=F32)


LOG2E = math.log2(math.e)


def _fill_values_and_ones(v_ref, vo_ref):
    d = v_ref.shape[1]
    vo_ref[:, :d] = v_ref[...]
    vo_ref[:, d:] = jnp.ones_like(v_ref)


def _softmax_times_values(s_list, vo_refs):
    d = vo_refs[0].shape[1] // 2
    m = functools.reduce(jnp.maximum, [jnp.max(s, axis=-1, keepdims=True) for s in s_list])
    acc = functools.reduce(jnp.add, [jnp.dot(jnp.exp2(s - m).astype(BF16), vo_ref[...], preferred_element_type=F32)
                                     for s, vo_ref in zip(s_list, vo_refs)])
    return acc[:, :d] / acc[:, d:]


def _attend_each(items):
    outs = []
    pending = None
    for q, k_refs, vo_refs in items:
        s_list = [_nt_dot(q, k_ref[...]) for k_ref in k_refs]
        if pending is not None:
            outs.append(_softmax_times_values(*pending))
        pending = (s_list, vo_refs)
    outs.append(_softmax_times_values(*pending))
    return outs


def _gqa_kernel(q_ref, *rest, with_lat):
    if with_lat:
        kc_ref, vc_ref, kl_ref, vl_ref, o_ref, voc_ref, vol_ref = rest
        k_refs, v_refs, vo_refs = (kc_ref, kl_ref), (vc_ref, vl_ref), (voc_ref, vol_ref)
    else:
        kc_ref, vc_ref, o_ref, voc_ref = rest
        k_refs, v_refs, vo_refs = (kc_ref,), (vc_ref,), (voc_ref,)

    @pl.when(pl.program_id(2) == 0)
    def _():
        for v_ref, vo_ref in zip(v_refs, vo_refs):
            _fill_values_and_ones(v_ref, vo_ref)

    outs = _attend_each([(q_ref[:, g * GQ_HEAD:(g + 1) * GQ_HEAD], k_refs, vo_refs) for g in range(GQ_GROUP)])
    for g, o in enumerate(outs):
        o_ref[:, g * GQ_HEAD:(g + 1) * GQ_HEAD] = o.astype(o_ref.dtype)


def _gqa_attention(q, k, v, *, latent, tq=256):
    gw = GQ_GROUP * GQ_HEAD
    ctx_kv = pl.BlockSpec((CTX_LEN, GQ_HEAD), lambda b, n, i: (CTX_BLOCK0 + b, n))
    scratch = [pltpu.VMEM((CTX_LEN, 2 * GQ_HEAD), BF16)]
    if latent:
        per_b = SEQ // tq
        lat_kv = pl.BlockSpec((SEQ, GQ_HEAD), lambda b, n, i: (b, n))
        in_specs = [pl.BlockSpec((tq, gw), lambda b, n, i: (b * per_b + i, n)), ctx_kv, ctx_kv, lat_kv, lat_kv]
        args = (q, k, v, k, v)
        rows = N_LAT_ROWS
        scratch.append(pltpu.VMEM((SEQ, 2 * GQ_HEAD), BF16))
    else:
        per_b = CTX_LEN // tq
        q_blk0 = N_LAT_ROWS // tq
        in_specs = [pl.BlockSpec((tq, gw), lambda b, n, i: (q_blk0 + b * per_b + i, n)), ctx_kv, ctx_kv]
        args = (q, k, v)
        rows = N_CTX_ROWS
    return pl.pallas_call(
        functools.partial(_gqa_kernel, with_lat=latent),
        grid=(BATCH, GQ_KV_HEADS, per_b),
        in_specs=in_specs,
        out_specs=pl.BlockSpec((tq, gw), lambda b, n, i: (b * per_b + i, n)),
        out_shape=jax.ShapeDtypeStruct((rows, GQ_HEADS * GQ_HEAD), BF16),
        scratch_shapes=scratch,
        compiler_params=_params("parallel", "parallel", "arbitrary"),
        name="gqa_attention_lat" if latent else "gqa_attention_ctx",
    )(*args)


def _mla_kernel(qn_ref, qp_ref, knc_ref, kpc_ref, vc_ref, knl_ref, kpl_ref, vl_ref, o_ref,
                kc_ref, kl_ref, voc_ref, vol_ref):
    heads = range(MLA_HEADS_PER_STEP)
    cols = [slice(j * MLA_NOPE, (j + 1) * MLA_NOPE) for j in heads]

    @pl.when(pl.program_id(2) == 0)
    def _():
        for kn_ref, kp_ref, v_ref, k_ref, vo_ref in ((knc_ref, kpc_ref, vc_ref, kc_ref, voc_ref),
                                                     (knl_ref, kpl_ref, vl_ref, kl_ref, vol_ref)):
            for j in heads:
                k_ref[j, :, :MLA_NOPE] = kn_ref[:, cols[j]]
                k_ref[j, :, MLA_NOPE:] = kp_ref[...]
                vo_ref[j, :, :MLA_V] = v_ref[:, cols[j]]
                vo_ref[j, :, MLA_V:] = jnp.ones_like(kp_ref)

    items = [(jnp.concatenate([qn_ref[:, cols[j]], qp_ref[j]], axis=1), (kc_ref.at[j], kl_ref.at[j]),
              (voc_ref.at[j], vol_ref.at[j])) for j in heads]
    for j, o in enumerate(_attend_each(items)):
        o_ref[:, cols[j]] = o.astype(o_ref.dtype)


MLA_HEADS_PER_STEP = 2


def _mla_attention(q_nope, q_pe, kv, k_pe, *, tq=256):
    per_b = SEQ // tq
    hs = MLA_HEADS_PER_STEP
    w = hs * MLA_NOPE
    v_blk0 = MLA_HEADS // hs
    in_specs = [
        pl.BlockSpec((tq, w), lambda b, h, i: (b * per_b + i, h)),
        pl.BlockSpec((hs, tq, MLA_NOPE), lambda b, h, i: (h, b * per_b + i, 0)),
        pl.BlockSpec((CTX_LEN, w), lambda b, h, i: (CTX_BLOCK0 + b, h)),
        pl.BlockSpec((CTX_LEN, MLA_NOPE), lambda b, h, i: (CTX_BLOCK0 + b, 0)),
        pl.BlockSpec((CTX_LEN, w), lambda b, h, i: (CTX_BLOCK0 + b, v_blk0 + h)),
        pl.BlockSpec((SEQ, w), lambda b, h, i: (b, h)),
        pl.BlockSpec((SEQ, MLA_NOPE), lambda b, h, i: (b, 0)),
        pl.BlockSpec((SEQ, w), lambda b, h, i: (b, v_blk0 + h)),
    ]
    return pl.pallas_call(
        _mla_kernel,
        grid=(BATCH, MLA_HEADS // hs, per_b),
        in_specs=in_specs,
        out_specs=pl.BlockSpec((tq, w), lambda b, h, i: (b * per_b + i, h)),
        out_shape=jax.ShapeDtypeStruct((N_LAT_ROWS, MLA_HEADS * MLA_V), BF16),
        scratch_shapes=[pltpu.VMEM((hs, CTX_LEN, 2 * MLA_NOPE), BF16), pltpu.VMEM((hs, SEQ, 2 * MLA_NOPE), BF16),
                        pltpu.VMEM((hs, CTX_LEN, 2 * MLA_V), BF16), pltpu.VMEM((hs, SEQ, 2 * MLA_V), BF16)],
        compiler_params=_params("parallel", "parallel", "arbitrary"),
        name="mla_attention",
    )(q_nope, q_pe, kv, k_pe, kv, kv, k_pe, kv)


def _diff_kernel(lam_ref, q_ref, kc_ref, vc_ref, kl_ref, vl_ref, g_ref, o_ref, voc_ref, vol_ref, *, out_scale):
    heads = range(DIFF_HEADS_PER_STEP)
    cols = [slice(j * DIFF_V, (j + 1) * DIFF_V) for j in heads]

    @pl.when(pl.program_id(2) == 0)
    def _():
        for v_ref, vo_ref in ((vc_ref, voc_ref), (vl_ref, vol_ref)):
            for j in heads:
                vo_ref[j, :, :DIFF_V] = v_ref[:, cols[j]]
                vo_ref[j, :, DIFF_V:] = jnp.ones((v_ref.shape[0], DIFF_V), BF16)

    first = lax.broadcasted_iota(jnp.int32, (q_ref.shape[0], DIFF_V), 1) < DIFF_HEAD
    items = []
    for j in heads:
        q = q_ref[:, cols[j]]
        zero = jnp.zeros_like(q)
        keys = (kc_ref[:, cols[j]], kl_ref[:, cols[j]])
        for qh in (jnp.where(first, q, zero), jnp.where(first, zero, q)):
            items.append((qh, keys, (voc_ref.at[j], vol_ref.at[j])))
    outs = _attend_each(items)
    for j in heads:
        o = outs[2 * j] - lam_ref[0] * outs[2 * j + 1]
        ms = jnp.mean(o * o, axis=-1, keepdims=True)
        o = o * lax.rsqrt(ms + NORM_EPS) * g_ref[...]
        o_ref[:, cols[j]] = (o * out_scale).astype(o_ref.dtype)


DIFF_HEADS_PER_STEP = 2


def _diff_attention(lam, q, k, v, subln, out_scale, *, tq=256):
    per_b = SEQ // tq
    hs = DIFF_HEADS_PER_STEP
    w = hs * DIFF_V
    ctx_kv = pl.BlockSpec((CTX_LEN, w), lambda b, h, i: (CTX_BLOCK0 + b, h))
    lat_kv = pl.BlockSpec((SEQ, w), lambda b, h, i: (b, h))
    in_specs = [
        pl.BlockSpec(memory_space=pltpu.SMEM),
        pl.BlockSpec((tq, w), lambda b, h, i: (b * per_b + i, h)),
        ctx_kv, ctx_kv, lat_kv, lat_kv,
        pl.BlockSpec((1, DIFF_V), lambda b, h, i: (0, 0)),
    ]
    return pl.pallas_call(
        functools.partial(_diff_kernel, out_scale=out_scale),
        grid=(BATCH, DIFF_HEADS // hs, per_b),
        in_specs=in_specs,
        out_specs=pl.BlockSpec((tq, w), lambda b, h, i: (b * per_b + i, h)),
        out_shape=jax.ShapeDtypeStruct((N_LAT_ROWS, DIFF_HEADS * DIFF_V), BF16),
        scratch_shapes=[pltpu.VMEM((hs, CTX_LEN, 2 * DIFF_V), BF16), pltpu.VMEM((hs, SEQ, 2 * DIFF_V), BF16)],
        compiler_params=_params("parallel", "parallel", "arbitrary"),
        name="diff_attention",
    )(lam, q, k, v, k, v, subln.reshape(1, DIFF_V))


RW_CHUNK = 128
RW_PAIRS = RW_HEADS // 2
LANES = 2 * RW_HEAD
SUB = 8
RW_UNROLL = 8
RW_GROUP = 8


def _rwkv_kernel(r_in, kk_in, v_in, w_in, k_in, b_in, y_out, s_ref, r_ref, kk_ref, v_ref, w_ref, k_ref, b_ref,
                 y_ref):
    d = pl.program_id(0)
    c = pl.program_id(2)

    @pl.when(c == 0)
    def _():
        s_ref[...] = jnp.zeros_like(s_ref)

    for p in range(RW_PAIRS):
        cols = slice(p * LANES, (p + 1) * LANES)
        for src, dst in ((r_in, r_ref), (kk_in, kk_ref), (v_in, v_ref)):
            dst[p] = src[:, cols]
        for src, dst in ((w_in, w_ref), (k_in, k_ref), (b_in, b_ref)):
            dst[0, p] = src[0, :, cols]

    v_hi = lax.broadcasted_iota(jnp.int32, (SUB, SUB, LANES), 0)
    v_lo = lax.broadcasted_iota(jnp.int32, (SUB, SUB, LANES), 1)
    lane3 = lax.broadcasted_iota(jnp.int32, (SUB, SUB, LANES), 2)
    dup = jnp.where((lane3 & (RW_HEAD - 1)) == v_hi * SUB + v_lo, 1.0, 0.0)
    blk_r = lax.broadcasted_iota(jnp.int32, (LANES, LANES), 0) // RW_HEAD
    blk_c = lax.broadcasted_iota(jnp.int32, (LANES, LANES), 1) // RW_HEAD
    head_ones = jnp.where(blk_r == blk_c, 1.0, 0.0).astype(BF16)
    slot_of_lane = lax.broadcasted_iota(jnp.int32, (SUB, LANES), 1) & (RW_HEAD - 1)
    step_dir = jnp.where(d == 0, 1, -1)

    y_ref[...] = jnp.zeros_like(y_ref)

    def row(ref, p, t, *lead):
        return ref[(*lead, p, pl.ds(t, SUB, stride=0), slice(None))]

    def as_rows(tile):
        return tile.reshape(RW_HEAD, LANES).astype(BF16)

    def head_sums(states, pairs, t_done, t_next):
        rows = []
        for p in pairs:
            rows.append(as_rows(states[p] * row(r_ref, p, t_done)))
            if t_next is not None:
                rows.append(as_rows(states[p] * row(kk_ref, p, t_next)))
                rows.append(as_rows(dup * row(v_ref, p, t_next)))
        return jnp.dot(jnp.concatenate(rows, axis=0), head_ones, preferred_element_type=F32)

    def slab(sums, q, n_slabs, i):
        lo = (q * n_slabs + i) * RW_HEAD
        return sums[lo:lo + RW_HEAD].reshape(SUB, SUB, LANES)

    def put_y(p, y_b, t, valid):
        window = t // RW_HEAD
        keep = (slot_of_lane == t % RW_HEAD) & valid
        y_ref[0, p, window] = jnp.where(keep, y_b, y_ref[0, p, window])

    groups = [range(g0, g0 + RW_GROUP) for g0 in range(0, RW_PAIRS, RW_GROUP)]

    def block(g, carry):
        first = g * RW_UNROLL
        first = jnp.where(d == 0, first, RW_CHUNK - 1 - first)
        times = [first + u * step_dir for u in range(RW_UNROLL)]
        t_before = jnp.clip(first - step_dir, 0, RW_CHUNK - 1)
        states = [s_ref[p] for p in range(RW_PAIRS)]
        sums = [head_sums(states, grp, t_before, times[0]) for grp in groups]
        for gi, grp in enumerate(groups):
            for q, p in enumerate(grp):
                put_y(p, slab(sums[gi], q, 3, 0), t_before, g > 0)
        for u in range(RW_UNROLL):
            t = times[u]
            for gi, grp in enumerate(groups):
                for q, p in enumerate(grp):
                    sa = slab(sums[gi], q, 3, 1)
                    vb = slab(sums[gi], q, 3, 2)
                    states[p] = (states[p] * row(w_ref, p, t, 0) - sa * row(b_ref, p, t, 0)
                                 + vb * row(k_ref, p, t, 0))
                if u + 1 < RW_UNROLL:
                    sums[gi] = head_sums(states, grp, t, times[u + 1])
                    for q, p in enumerate(grp):
                        put_y(p, slab(sums[gi], q, 3, 0), t, True)
        for p in range(RW_PAIRS):
            s_ref[p] = states[p]
        return carry

    lax.fori_loop(0, RW_CHUNK // RW_UNROLL, block, 0)

    t_last = jnp.where(d == 0, RW_CHUNK - 1, 0)
    states = [s_ref[p] for p in range(RW_PAIRS)]
    for grp in groups:
        last = head_sums(states, grp, t_last, None)
        for q, p in enumerate(grp):
            put_y(p, slab(last, q, 1, 0), t_last, True)

    first_head = lax.broadcasted_iota(jnp.int32, (RW_HEAD, LANES), 1) < RW_HEAD
    for p in range(0, RW_PAIRS, 2):
        for window in range(RW_CHUNK // RW_HEAD):
            both = jnp.concatenate([y_ref[0, p, window].reshape(RW_HEAD, LANES),
                                    y_ref[0, p + 1, window].reshape(RW_HEAD, LANES)], axis=0)
            by_row = both.T
            head0, head1 = by_row[:RW_HEAD], by_row[RW_HEAD:]
            rows = slice(window * RW_HEAD, (window + 1) * RW_HEAD)
            y_out[0, rows, p * LANES:(p + 1) * LANES] = jnp.where(
                first_head, head0, pltpu.roll(head1, RW_HEAD, axis=1))
            y_out[0, rows, (p + 1) * LANES:(p + 2) * LANES] = jnp.where(
                first_head, pltpu.roll(head0, RW_HEAD, axis=1), head1)


def _rwkv_scan(r, kk, v, w, k, b):
    n_ctx = CTX_LEN // RW_CHUNK
    n_lat = SEQ // RW_CHUNK
    ctx0 = N_LAT_ROWS // RW_CHUNK

    def chunk(d, bb, c):
        j = jnp.where(d == 0, c, jnp.where(c < n_ctx, n_ctx - 1 - c, n_ctx + n_lat - 1 - (c - n_ctx)))
        return jnp.where(j < n_ctx, ctx0 + bb * n_ctx + j, bb * n_lat + j - n_ctx)

    shared = pl.BlockSpec((RW_CHUNK, RW_W), lambda d, bb, c: (chunk(d, bb, c), 0))
    per_dir = pl.BlockSpec((1, RW_CHUNK, RW_W), lambda d, bb, c: (d, chunk(d, bb, c), 0))
    staged = pltpu.VMEM((RW_PAIRS, RW_CHUNK, LANES), F32)
    staged_dir = pltpu.VMEM((1, RW_PAIRS, RW_CHUNK, LANES), F32)
    return pl.pallas_call(
        _rwkv_kernel,
        grid=(2, BATCH, n_ctx + n_lat),
        in_specs=[shared, shared, shared, per_dir, per_dir, per_dir],
        out_specs=per_dir,
        out_shape=jax.ShapeDtypeStruct((2, N_ROWS, RW_W), F32),
        scratch_shapes=[pltpu.VMEM((RW_PAIRS, RW_HEAD // SUB, SUB, LANES), F32),
                        staged, staged, staged, staged_dir, staged_dir, staged_dir,
                        pltpu.VMEM((1, RW_PAIRS, RW_CHUNK // RW_HEAD, SUB, SUB, LANES), F32)],
        compiler_params=_params("parallel", "parallel", "arbitrary"),
        name="rwkv7_scan",
    )(r, kk, v, w, k, b)


def _axial_rope_tables(n_tokens, rot_dim):
    n_rows = n_tokens // GRID_W
    row = jnp.repeat(jnp.arange(n_rows, dtype=F32), GRID_W)
    col = jnp.tile(jnp.arange(GRID_W, dtype=F32), n_rows)
    axis_dim = rot_dim // 2
    inv_freq = ROPE_THETA ** (-jnp.arange(0, axis_dim, 2, dtype=F32) / axis_dim)
    ang_r = row[:, None] * inv_freq
    ang_c = col[:, None] * inv_freq
    ang = jnp.concatenate([ang_r, ang_r, ang_c, ang_c], axis=-1)
    return jnp.cos(ang), jnp.sin(ang)


def _rotate_half(z):
    z1, z2 = jnp.split(z, 2, axis=-1)
    return jnp.concatenate([-z2, z1], axis=-1)


def _apply_rope(x, cos, sin):
    half = x.shape[-1] // 2
    rot = jnp.concatenate([_rotate_half(x[..., :half]), _rotate_half(x[..., half:])], axis=-1)
    return x * cos + rot * sin


def _rms_f32(x, g):
    return x * lax.rsqrt(jnp.mean(x * x, axis=-1, keepdims=True) + NORM_EPS) * g


def _rope_lat(x, rope):
    cos, sin = rope
    heads, dim = x.shape[1:]
    lat = x[:N_LAT_ROWS].reshape(BATCH, SEQ, heads, dim)
    lat = _apply_rope(lat, cos[None, :, None, :], sin[None, :, None, :]).reshape(N_LAT_ROWS, heads, dim)
    return lat if x.shape[0] == N_LAT_ROWS else jnp.concatenate([lat, x[N_LAT_ROWS:]], axis=0)


def _centred_shift_rows(p):
    def shift(z):
        prev = jnp.pad(z[:, :-1], ((0, 0), (1, 0), (0, 0)))
        nxt = jnp.pad(z[:, 1:], ((0, 0), (0, 1), (0, 0)))
        return 0.5 * (prev + nxt)
    c = p.shape[-1]
    lat = shift(p[:N_LAT_ROWS].reshape(BATCH, SEQ, c)).reshape(N_LAT_ROWS, c)
    ctx = shift(p[N_LAT_ROWS:].reshape(BATCH, CTX_LEN, c)).reshape(N_CTX_ROWS, c)
    return jnp.concatenate([lat, ctx], axis=0)


def _small_mm(a, w, name):
    return _matmul(a.astype(BF16), w.astype(BF16), out_dtype=F32, tm=1024, tn=w.shape[1], tk=w.shape[0], name=name)


def _rwkv_mixer(pr, mu, w0, w_up, a0, a_up, g_up, k_k, k_a, r_k, ln_w, ln_b):
    pr = pr + (_centred_shift_rows(pr) - pr) * mu
    r = pr[:, :RW_W]
    k = pr[:, RW_W:2 * RW_W]
    v = pr[:, 2 * RW_W:3 * RW_W]
    o = 3 * RW_W
    gd = pr[:, o:o + GATE_RANK]
    o += GATE_RANK
    wd = pr[:, o:o + 2 * DECAY_RANK].reshape(N_ROWS, 2, DECAY_RANK)
    o += 2 * DECAY_RANK
    ad = pr[:, o:o + 2 * AAA_RANK].reshape(N_ROWS, 2, AAA_RANK)
    g = _small_mm(jax.nn.sigmoid(gd), g_up, "rwkv_gate_up")
    wu = jnp.stack([_small_mm(jnp.tanh(wd[:, d]), w_up[d], "rwkv_decay_up") for d in range(2)])
    au = jnp.stack([_small_mm(ad[:, d], a_up[d], "rwkv_aaa_up") for d in range(2)])
    w_log = -jax.nn.softplus(-(w0[:, None, :] + wu)) - 0.5
    decay = jnp.exp(-jnp.exp(w_log))
    a = jax.nn.sigmoid(a0[:, None, :] + au)
    kk = (k * k_k).reshape(N_ROWS, RW_HEADS, RW_HEAD)
    kk = kk / jnp.maximum(jnp.linalg.norm(kk, axis=-1, keepdims=True), 1e-12)
    kk = kk.reshape(N_ROWS, RW_W)
    k_dir = k[None] * (1.0 + (a - 1.0) * k_a)
    b_dir = kk[None] * a

    y2 = _rwkv_scan(r, kk, v, decay, k_dir, b_dir)
    y = (y2[0] + y2[1]).reshape(N_ROWS, RW_HEADS, RW_HEAD)

    mean = jnp.mean(y, axis=-1, keepdims=True)
    var = jnp.mean(jnp.square(y - mean), axis=-1, keepdims=True)
    yn = ((y - mean) * lax.rsqrt(var + GN_EPS)).reshape(N_ROWS, RW_W) * ln_w + ln_b
    rh = r.reshape(N_ROWS, RW_HEADS, RW_HEAD)
    coef = jnp.sum(rh[None] * k_dir.reshape(2, N_ROWS, RW_HEADS, RW_HEAD) * r_k, axis=(0, 3))
    bonus = (coef[..., None] * v.reshape(N_ROWS, RW_HEADS, RW_HEAD)).reshape(N_ROWS, RW_W)
    return (yn + bonus) * g


def _even_mixer(a_all, rope, w_in, rw_args, q_g, k_g, need_ctx):
    p = _matmul(a_all, w_in.astype(BF16), out_dtype=F32, tm=1024, tn=896, tk=D_MODEL, name="even_w_in")
    rw = _rwkv_mixer(p[:, :RW_IN], *rw_args).astype(BF16)
    pa = p[:, RW_IN:]
    nq = GQ_HEADS * GQ_HEAD
    nkv = GQ_KV_HEADS * GQ_HEAD
    q = _rms_f32(pa[:, :nq].reshape(N_ROWS, GQ_HEADS, GQ_HEAD), q_g)
    k = _rms_f32(pa[:, nq:nq + nkv].reshape(N_ROWS, GQ_KV_HEADS, GQ_HEAD), k_g)
    v = pa[:, nq + nkv:].astype(BF16)
    q = (_rope_lat(q, rope) * (GQ_SCALE * LOG2E)).reshape(N_ROWS, nq).astype(BF16)
    k = _rope_lat(k, rope).reshape(N_ROWS, nkv).astype(BF16)
    at = _gqa_attention(q, k, v, latent=True)
    if need_ctx:
        at = jnp.concatenate([at, _gqa_attention(q, k, v, latent=False)], axis=0)
        return jnp.concatenate([rw, at], axis=-1)
    return jnp.concatenate([rw[:N_LAT_ROWS], at], axis=-1)


def _odd_mixer(a_all, rope, w_in, q_norm, q_up, kv_norm, kv_up, lq1, lk1, lq2, lk2, subln, layer_idx):
    w_in = jnp.pad(w_in.astype(BF16), ((0, 0), (0, IN_ODD_PADDED - IN_ODD)))
    p = _matmul(a_all, w_in, out_dtype=F32, tm=1024, tn=768, tk=D_MODEL, name="odd_w_in")
    lam_init = 0.8 - 0.6 * math.exp(-0.3 * layer_idx)
    lam = (jnp.exp(jnp.sum(lq1 * lk1).astype(F32)) - jnp.exp(jnp.sum(lq2 * lk2).astype(F32)) + lam_init)

    c_q = _rms_f32(p[:N_LAT_ROWS, :Q_LORA], q_norm).astype(BF16)
    c_kv = _rms_f32(p[:, Q_LORA:Q_LORA + KV_LORA], kv_norm).astype(BF16)
    k_pe = p[:, Q_LORA + KV_LORA:MLA_IN]
    q_up_h = q_up.reshape(Q_LORA, MLA_HEADS, MLA_NOPE + MLA_ROPE)
    q_up_r = jnp.concatenate([q_up_h[:, :, :MLA_NOPE].reshape(Q_LORA, -1),
                              q_up_h[:, :, MLA_NOPE:].reshape(Q_LORA, -1)], axis=1).astype(BF16)
    kv_up_h = kv_up.reshape(KV_LORA, MLA_HEADS, MLA_NOPE + MLA_V)
    kv_up_r = jnp.concatenate([kv_up_h[:, :, :MLA_NOPE].reshape(KV_LORA, -1),
                               kv_up_h[:, :, MLA_NOPE:].reshape(KV_LORA, -1)], axis=1).astype(BF16)
    q_all = _matmul(c_q, q_up_r, out_dtype=F32, tm=1024, tn=1536, tk=Q_LORA, name="mla_q_up")
    kv = _matmul(c_kv, kv_up_r, out_dtype=BF16, tm=1024, tn=2048, tk=KV_LORA, name="mla_kv_up")
    q_all = q_all * (MLA_SCALE * LOG2E)
    q_nope = q_all[:, :MLA_HEADS * MLA_NOPE].astype(BF16)
    q_pe = q_all[:, MLA_HEADS * MLA_NOPE:].reshape(N_LAT_ROWS, MLA_HEADS, MLA_ROPE)
    q_pe = _rope_lat(q_pe, rope).transpose(1, 0, 2).astype(BF16)
    q_pe = jnp.pad(q_pe, ((0, 0), (0, 0), (0, MLA_NOPE - MLA_ROPE)))
    k_pe = _rope_lat(k_pe.reshape(N_ROWS, 1, MLA_ROPE), rope).reshape(N_ROWS, MLA_ROPE).astype(BF16)
    k_pe = jnp.pad(k_pe, ((0, 0), (0, MLA_NOPE - MLA_ROPE)))
    m_out = _mla_attention(q_nope, q_pe, kv, k_pe)

    pd = p[:, MLA_IN:IN_ODD]
    qk_w = DIFF_HEADS * 2 * DIFF_HEAD
    dq = pd[:N_LAT_ROWS, :qk_w].reshape(N_LAT_ROWS, DIFF_HEADS * 2, DIFF_HEAD)
    dk = pd[:, qk_w:2 * qk_w].reshape(N_ROWS, DIFF_HEADS * 2, DIFF_HEAD)
    dv = pd[:, 2 * qk_w:].astype(BF16)
    dq = (_rope_lat(dq, rope) * (DIFF_SCALE * LOG2E)).reshape(N_LAT_ROWS, qk_w).astype(BF16)
    dk = _rope_lat(dk, rope).reshape(N_ROWS, qk_w).astype(BF16)
    d_out = _diff_attention(lam.reshape(1), dq, dk, dv, subln, 1.0 - lam_init)
    return jnp.concatenate([m_out, d_out], axis=-1)


def kernel(x, c, ctx, c_ctx, ada_w, ada_b, norm1_g, norm2_g, mlp_w1, mlp_w2, final_g, ev_w_in, ev_w_out, rw_mu, rw_w0, rw_w_up, rw_a0, rw_a_up, rw_g_up, rw_k_k, rw_k_a, rw_r_k, rw_ln_w, rw_ln_b, gq_q_norm, gq_k_norm, od_w_in, od_w_out, mla_q_norm, mla_q_up, mla_kv_norm, mla_kv_up, diff_lq1, diff_lk1, diff_lq2, diff_lk2, diff_subln):
    rope_gq = _axial_rope_tables(SEQ, GQ_HEAD)
    rope_64 = _axial_rope_tables(SEQ, MLA_ROPE)
    h = jnp.concatenate([x.reshape(N_LAT_ROWS, D_MODEL), ctx.reshape(N_CTX_ROWS, D_MODEL)], axis=0)
    cond = jnp.concatenate([c_ctx[None], c, jnp.zeros((3, D_MODEL), F32)], axis=0)
    cond = jax.nn.silu(cond).astype(BF16)
    for i in range(DEPTH):
        last = i == DEPTH - 1
        mod = _matmul(cond, ada_w[i], out_dtype=F32, tm=8, tn=2048, tk=1024, name="adaln_mod")
        mod = (mod + ada_b[i])[:BATCH + 1].reshape(BATCH + 1, N_MOD, 1, D_MODEL)
        sh1, sc1, g1, sh2, sc2, g2 = (mod[:, m] for m in range(N_MOD))
        a_all = _norm(h, norm1_g[i], sc1, sh1, rows=N_ROWS, out_dtype=BF16)
        j = i // 2
        if i % 2 == 0:
            rw_args = (rw_mu[j], rw_w0[j], rw_w_up[j], rw_a0[j], rw_a_up[j], rw_g_up[j], rw_k_k[j], rw_k_a[j],
                       rw_r_k[j], rw_ln_w[j], rw_ln_b[j])
            mix = _even_mixer(a_all, rope_gq, ev_w_in[j], rw_args, gq_q_norm[j], gq_k_norm[j], not last)
            w_out = ev_w_out[j]
        else:
            if not last:
                raise NotImplementedError("context rows of an odd layer are only needed when a layer follows")
            mix = _odd_mixer(a_all, rope_64, od_w_in[j], mla_q_norm[j], mla_q_up[j], mla_kv_norm[j],
                             mla_kv_up[j], diff_lq1[j], diff_lk1[j], diff_lq2[j], diff_lk2[j], diff_subln[j], i)
            w_out = od_w_out[j]
        rows = N_LAT_ROWS if last else N_ROWS
        h = _matmul(mix, w_out.astype(BF16), out_dtype=F32, tm=1024, tn=1024, tk=2048, rows=rows,
                    epilogue="gated_residual", res=h, gate=g1, name="mixer_w_out")
        a2 = _norm(h, norm2_g[i], sc2, sh2, rows=rows, out_dtype=BF16)
        hid = _matmul(a2, mlp_w1[i].astype(BF16), out_dtype=BF16, tm=1024, tn=1024, tk=D_MODEL,
                      epilogue="relu2", name="mlp_w1")
        h = _matmul(hid, mlp_w2[i].astype(BF16), out_dtype=F32, tm=1024, tn=1024, tk=2048,
                    epilogue="gated_residual", res=h, gate=g2, name="mlp_w2")
    out = _norm(h, final_g, rows=N_LAT_ROWS, out_dtype=F32)
    return out.reshape(BATCH, SEQ, D_MODEL)
```

```python
import functools
import math

import jax
import jax.numpy as jnp
from jax import lax
from jax.experimental import pallas as pl
from jax.experimental.pallas import tpu as pltpu

D_MODEL = 4096
BATCH = 4
SEQ = 4096
DEPTH = 2
CTX_LEN = 256
GRID_W = 64
ROPE_THETA = 10000.0
NORM_EPS = 1e-6
MLP_HIDDEN = 4 * D_MODEL
N_MOD = 6
HALF = D_MODEL // 2

RW_HEAD = 64
RW_W = HALF
RW_HEADS = RW_W // RW_HEAD
DECAY_RANK = 96
AAA_RANK = 96
GATE_RANK = 256
GN_EPS = 64e-5
RW_IN = 3 * RW_W + GATE_RANK + 2 * DECAY_RANK + 2 * AAA_RANK

GQ_HEAD = 128
GQ_HEADS = HALF // GQ_HEAD
GQ_KV_HEADS = GQ_HEADS // 4
GQ_GROUP = GQ_HEADS // GQ_KV_HEADS
GQ_SCALE = GQ_HEAD ** -0.5
IN_EVEN = RW_IN + (GQ_HEADS + 2 * GQ_KV_HEADS) * GQ_HEAD

MLA_NOPE = 128
MLA_ROPE = 64
MLA_V = 128
MLA_HEADS = HALF // MLA_V
Q_LORA = 768
KV_LORA = 512
MLA_SCALE = (MLA_NOPE + MLA_ROPE) ** -0.5
MLA_IN = Q_LORA + KV_LORA + MLA_ROPE

DIFF_HEAD = 64
DIFF_V = 2 * DIFF_HEAD
DIFF_HEADS = HALF // DIFF_V
DIFF_SCALE = DIFF_HEAD ** -0.5
DIFF_IN = DIFF_HEADS * (4 * DIFF_HEAD + DIFF_V)
IN_ODD = MLA_IN + DIFF_IN
IN_ODD_PADDED = 7680

N_LAT_ROWS = BATCH * SEQ
N_CTX_ROWS = BATCH * CTX_LEN
N_ROWS = N_LAT_ROWS + N_CTX_ROWS
T_ALL = CTX_LEN + SEQ
CTX_BLOCK0 = N_LAT_ROWS // CTX_LEN

V7X_VMEM_LIMIT_BYTES = 56 * 1024 * 1024

BF16 = jnp.bfloat16
F32 = jnp.float32


def _params(*sem):
    return pltpu.CompilerParams(dimension_semantics=sem, vmem_limit_bytes=V7X_VMEM_LIMIT_BYTES)


def _row_group(row_tile, tile_rows):
    start = row_tile * tile_rows
    return jnp.where(start < N_LAT_ROWS, 1 + start // SEQ, 0)


def _norm_kernel(x_ref, g_ref, *rest, modulate):
    if modulate:
        sc_ref, sh_ref, o_ref = rest
    else:
        (o_ref,) = rest
    x = x_ref[...]
    ms = jnp.mean(x * x, axis=-1, keepdims=True)
    y = x * lax.rsqrt(ms + NORM_EPS) * g_ref[...]
    if modulate:
        y = y * (1.0 + sc_ref[0]) + sh_ref[0]
    o_ref[...] = y.astype(o_ref.dtype)


def _norm(x, g, sc=None, sh=None, *, rows, out_dtype, tile=256):
    d = x.shape[1]
    modulate = sc is not None
    in_specs = [pl.BlockSpec((tile, d), lambda i: (i, 0)),
                pl.BlockSpec((1, d), lambda i: (0, 0))]
    args = [x, g.reshape(1, d)]
    if modulate:
        mod_spec = pl.BlockSpec((1, 1, d), lambda i: (_row_group(i, tile), 0, 0))
        in_specs += [mod_spec, mod_spec]
        args += [sc, sh]
    return pl.pallas_call(
        functools.partial(_norm_kernel, modulate=modulate),
        grid=(rows // tile,),
        in_specs=in_specs,
        out_specs=pl.BlockSpec((tile, d), lambda i: (i, 0)),
        out_shape=jax.ShapeDtypeStruct((rows, d), out_dtype),
        compiler_params=_params("parallel"),
        name="rmsnorm_mod" if modulate else "rmsnorm",
    )(*args)


def _mm_kernel(a_ref, w_ref, *rest, epilogue, nk):
    if epilogue == "gated_residual":
        res_ref, gate_ref, o_ref, acc_ref = rest
    else:
        o_ref, acc_ref = rest
    k = pl.program_id(2)
    part = jnp.dot(a_ref[...].astype(BF16), w_ref[...].astype(BF16), preferred_element_type=F32)

    @pl.when(k == 0)
    def _():
        acc_ref[...] = part

    @pl.when(k > 0)
    def _():
        acc_ref[...] += part

    @pl.when(k == nk - 1)
    def _():
        acc = acc_ref[...]
        if epilogue == "relu2":
            acc = jnp.square(jnp.maximum(acc, 0.0))
        elif epilogue == "gated_residual":
            acc = res_ref[...] + gate_ref[0] * acc
        o_ref[...] = acc.astype(o_ref.dtype)


def _matmul(a, w, *, out_dtype, tm, tn, tk, rows=None, epilogue=None, res=None, gate=None, name):
    kdim, n = w.shape
    m = a.shape[0] if rows is None else rows
    assert a.shape[1] == kdim and m % tm == 0 and n % tn == 0 and kdim % tk == 0
    nk = kdim // tk
    in_specs = [pl.BlockSpec((tm, tk), lambda i, j, k: (i, k)),
                pl.BlockSpec((tk, tn), lambda i, j, k: (k, j))]
    args = [a, w]
    if epilogue == "gated_residual":
        in_specs += [pl.BlockSpec((tm, tn), lambda i, j, k: (i, j)),
                     pl.BlockSpec((1, 1, tn), lambda i, j, k: (_row_group(i, tm), 0, j))]
        args += [res, gate]
    return pl.pallas_call(
        functools.partial(_mm_kernel, epilogue=epilogue, nk=nk),
        grid=(m // tm, n // tn, nk),
        in_specs=in_specs,
        out_specs=pl.BlockSpec((tm, tn), lambda i, j, k: (i, j)),
        out_shape=jax.ShapeDtypeStruct((m, n), out_dtype),
        scratch_shapes=[pltpu.VMEM((tm, tn), F32)],
        compiler_params=_params("parallel", "parallel", "arbitrary"),
        name=name,
    )(*args)


def _nt_dot(a, b):
    return lax.dot_general(a, b, (((1,), (1,)), ((), ())), preferred_element_type=F32)


LOG2E = math.log2(math.e)


def _fill_values_and_ones(v_ref, vo_ref):
    d = v_ref.shape[1]
    vo_ref[:, :d] = v_ref[...]
    vo_ref[:, d:] = jnp.ones_like(v_ref)


def _softmax_times_values(s_list, vo_refs):
    d = vo_refs[0].shape[1] // 2
    m = functools.reduce(jnp.maximum, [jnp.max(s, axis=-1, keepdims=True) for s in s_list])
    acc = functools.reduce(jnp.add, [jnp.dot(jnp.exp2(s - m).astype(BF16), vo_ref[...], preferred_element_type=F32)
                                     for s, vo_ref in zip(s_list, vo_refs)])
    return acc[:, :d] / acc[:, d:]


def _attend_each(items):
    outs = []
    pending = None
    for q, k_refs, vo_refs in items:
        s_list = [_nt_dot(q, k_ref[...]) for k_ref in k_refs]
        if pending is not None:
            outs.append(_softmax_times_values(*pending))
        pending = (s_list, vo_refs)
    outs.append(_softmax_times_values(*pending))
    return outs


def _gqa_kernel(q_ref, *rest, with_lat):
    if with_lat:
        kc_ref, vc_ref, kl_ref, vl_ref, o_ref, voc_ref, vol_ref = rest
        k_refs, v_refs, vo_refs = (kc_ref, kl_ref), (vc_ref, vl_ref), (voc_ref, vol_ref)
    else:
        kc_ref, vc_ref, o_ref, voc_ref = rest
        k_refs, v_refs, vo_refs = (kc_ref,), (vc_ref,), (voc_ref,)

    @pl.when(pl.program_id(2) == 0)
    def _():
        for v_ref, vo_ref in zip(v_refs, vo_refs):
            _fill_values_and_ones(v_ref, vo_ref)

    outs = _attend_each([(q_ref[:, g * GQ_HEAD:(g + 1) * GQ_HEAD], k_refs, vo_refs) for g in range(GQ_GROUP)])
    for g, o in enumerate(outs):
        o_ref[:, g * GQ_HEAD:(g + 1) * GQ_HEAD] = o.astype(o_ref.dtype)


def _gqa_attention(q, k, v, *, latent, tq=256):
    gw = GQ_GROUP * GQ_HEAD
    ctx_kv = pl.BlockSpec((CTX_LEN, GQ_HEAD), lambda b, n, i: (CTX_BLOCK0 + b, n))
    scratch = [pltpu.VMEM((CTX_LEN, 2 * GQ_HEAD), BF16)]
    if latent:
        per_b = SEQ // tq
        lat_kv = pl.BlockSpec((SEQ, GQ_HEAD), lambda b, n, i: (b, n))
        in_specs = [pl.BlockSpec((tq, gw), lambda b, n, i: (b * per_b + i, n)), ctx_kv, ctx_kv, lat_kv, lat_kv]
        args = (q, k, v, k, v)
        rows = N_LAT_ROWS
        scratch.append(pltpu.VMEM((SEQ, 2 * GQ_HEAD), BF16))
    else:
        per_b = CTX_LEN // tq
        q_blk0 = N_LAT_ROWS // tq
        in_specs = [pl.BlockSpec((tq, gw), lambda b, n, i: (q_blk0 + b * per_b + i, n)), ctx_kv, ctx_kv]
        args = (q, k, v)
        rows = N_CTX_ROWS
    return pl.pallas_call(
        functools.partial(_gqa_kernel, with_lat=latent),
        grid=(BATCH, GQ_KV_HEADS, per_b),
        in_specs=in_specs,
        out_specs=pl.BlockSpec((tq, gw), lambda b, n, i: (b * per_b + i, n)),
        out_shape=jax.ShapeDtypeStruct((rows, GQ_HEADS * GQ_HEAD), BF16),
        scratch_shapes=scratch,
        compiler_params=_params("parallel", "parallel", "arbitrary"),
        name="gqa_attention_lat" if latent else "gqa_attention_ctx",
    )(*args)


def _mla_kernel(qn_ref, qp_ref, knc_ref, kpc_ref, vc_ref, knl_ref, kpl_ref, vl_ref, o_ref,
                kc_ref, kl_ref, voc_ref, vol_ref):
    heads = range(MLA_HEADS_PER_STEP)
    cols = [slice(j * MLA_NOPE, (j + 1) * MLA_NOPE) for j in heads]

    @pl.when(pl.program_id(2) == 0)
    def _():
        for kn_ref, kp_ref, v_ref, k_ref, vo_ref in ((knc_ref, kpc_ref, vc_ref, kc_ref, voc_ref),
                                                     (knl_ref, kpl_ref, vl_ref, kl_ref, vol_ref)):
            for j in heads:
                k_ref[j, :, :MLA_NOPE] = kn_ref[:, cols[j]]
                k_ref[j, :, MLA_NOPE:] = kp_ref[...]
                vo_ref[j, :, :MLA_V] = v_ref[:, cols[j]]
                vo_ref[j, :, MLA_V:] = jnp.ones_like(kp_ref)

    items = [(jnp.concatenate([qn_ref[:, cols[j]], qp_ref[j]], axis=1), (kc_ref.at[j], kl_ref.at[j]),
              (voc_ref.at[j], vol_ref.at[j])) for j in heads]
    for j, o in enumerate(_attend_each(items)):
        o_ref[:, cols[j]] = o.astype(o_ref.dtype)


MLA_HEADS_PER_STEP = 2


def _mla_attention(q_nope, q_pe, kv, k_pe, *, tq=256):
    per_b = SEQ // tq
    hs = MLA_HEADS_PER_STEP
    w = hs * MLA_NOPE
    v_blk0 = MLA_HEADS // hs
    in_specs = [
        pl.BlockSpec((tq, w), lambda b, h, i: (b * per_b + i, h)),
        pl.BlockSpec((hs, tq, MLA_NOPE), lambda b, h, i: (h, b * per_b + i, 0)),
        pl.BlockSpec((CTX_LEN, w), lambda b, h, i: (CTX_BLOCK0 + b, h)),
        pl.BlockSpec((CTX_LEN, MLA_NOPE), lambda b, h, i: (CTX_BLOCK0 + b, 0)),
        pl.BlockSpec((CTX_LEN, w), lambda b, h, i: (CTX_BLOCK0 + b, v_blk0 + h)),
        pl.BlockSpec((SEQ, w), lambda b, h, i: (b, h)),
        pl.BlockSpec((SEQ, MLA_NOPE), lambda b, h, i: (b, 0)),
        pl.BlockSpec((SEQ, w), lambda b, h, i: (b, v_blk0 + h)),
    ]
    return pl.pallas_call(
        _mla_kernel,
        grid=(BATCH, MLA_HEADS // hs, per_b),
        in_specs=in_specs,
        out_specs=pl.BlockSpec((tq, w), lambda b, h, i: (b * per_b + i, h)),
        out_shape=jax.ShapeDtypeStruct((N_LAT_ROWS, MLA_HEADS * MLA_V), BF16),
        scratch_shapes=[pltpu.VMEM((hs, CTX_LEN, 2 * MLA_NOPE), BF16), pltpu.VMEM((hs, SEQ, 2 * MLA_NOPE), BF16),
                        pltpu.VMEM((hs, CTX_LEN, 2 * MLA_V), BF16), pltpu.VMEM((hs, SEQ, 2 * MLA_V), BF16)],
        compiler_params=_params("parallel", "parallel", "arbitrary"),
        name="mla_attention",
    )(q_nope, q_pe, kv, k_pe, kv, kv, k_pe, kv)


def _diff_kernel(lam_ref, q_ref, kc_ref, vc_ref, kl_ref, vl_ref, g_ref, o_ref, voc_ref, vol_ref, *, out_scale):
    heads = range(DIFF_HEADS_PER_STEP)
    cols = [slice(j * DIFF_V, (j + 1) * DIFF_V) for j in heads]

    @pl.when(pl.program_id(2) == 0)
    def _():
        for v_ref, vo_ref in ((vc_ref, voc_ref), (vl_ref, vol_ref)):
            for j in heads:
                vo_ref[j, :, :DIFF_V] = v_ref[:, cols[j]]
                vo_ref[j, :, DIFF_V:] = jnp.ones((v_ref.shape[0], DIFF_V), BF16)

    first = lax.broadcasted_iota(jnp.int32, (q_ref.shape[0], DIFF_V), 1) < DIFF_HEAD
    items = []
    for j in heads:
        q = q_ref[:, cols[j]]
        zero = jnp.zeros_like(q)
        keys = (kc_ref[:, cols[j]], kl_ref[:, cols[j]])
        for qh in (jnp.where(first, q, zero), jnp.where(first, zero, q)):
            items.append((qh, keys, (voc_ref.at[j], vol_ref.at[j])))
    outs = _attend_each(items)
    for j in heads:
        o = outs[2 * j] - lam_ref[0] * outs[2 * j + 1]
        ms = jnp.mean(o * o, axis=-1, keepdims=True)
        o = o * lax.rsqrt(ms + NORM_EPS) * g_ref[...]
        o_ref[:, cols[j]] = (o * out_scale).astype(o_ref.dtype)


DIFF_HEADS_PER_STEP = 2


def _diff_attention(lam, q, k, v, subln, out_scale, *, tq=256):
    per_b = SEQ // tq
    hs = DIFF_HEADS_PER_STEP
    w = hs * DIFF_V
    ctx_kv = pl.BlockSpec((CTX_LEN, w), lambda b, h, i: (CTX_BLOCK0 + b, h))
    lat_kv = pl.BlockSpec((SEQ, w), lambda b, h, i: (b, h))
    in_specs = [
        pl.BlockSpec(memory_space=pltpu.SMEM),
        pl.BlockSpec((tq, w), lambda b, h, i: (b * per_b + i, h)),
        ctx_kv, ctx_kv, lat_kv, lat_kv,
        pl.BlockSpec((1, DIFF_V), lambda b, h, i: (0, 0)),
    ]
    return pl.pallas_call(
        functools.partial(_diff_kernel, out_scale=out_scale),
        grid=(BATCH, DIFF_HEADS // hs, per_b),
        in_specs=in_specs,
        out_specs=pl.BlockSpec((tq, w), lambda b, h, i: (b * per_b + i, h)),
        out_shape=jax.ShapeDtypeStruct((N_LAT_ROWS, DIFF_HEADS * DIFF_V), BF16),
        scratch_shapes=[pltpu.VMEM((hs, CTX_LEN, 2 * DIFF_V), BF16), pltpu.VMEM((hs, SEQ, 2 * DIFF_V), BF16)],
        compiler_params=_params("parallel", "parallel", "arbitrary"),
        name="diff_attention",
    )(lam, q, k, v, k, v, subln.reshape(1, DIFF_V))


RW_CHUNK = 128
RW_PAIRS = RW_HEADS // 2
LANES = 2 * RW_HEAD
SUB = 8
RW_UNROLL = 8
RW_GROUP = 8


def _rwkv_kernel(r_in, kk_in, v_in, w_in, k_in, b_in, y_out, s_ref, r_ref, kk_ref, v_ref, w_ref, k_ref, b_ref,
                 y_ref):
    d = pl.program_id(0)
    c = pl.program_id(2)

    @pl.when(c == 0)
    def _():
        s_ref[...] = jnp.zeros_like(s_ref)

    for p in range(RW_PAIRS):
        cols = slice(p * LANES, (p + 1) * LANES)
        for src, dst in ((r_in, r_ref), (kk_in, kk_ref), (v_in, v_ref)):
            dst[p] = src[:, cols]
        for src, dst in ((w_in, w_ref), (k_in, k_ref), (b_in, b_ref)):
            dst[0, p] = src[0, :, cols]

    v_hi = lax.broadcasted_iota(jnp.int32, (SUB, SUB, LANES), 0)
    v_lo = lax.broadcasted_iota(jnp.int32, (SUB, SUB, LANES), 1)
    lane3 = lax.broadcasted_iota(jnp.int32, (SUB, SUB, LANES), 2)
    dup = jnp.where((lane3 & (RW_HEAD - 1)) == v_hi * SUB + v_lo, 1.0, 0.0)
    blk_r = lax.broadcasted_iota(jnp.int32, (LANES, LANES), 0) // RW_HEAD
    blk_c = lax.broadcasted_iota(jnp.int32, (LANES, LANES), 1) // RW_HEAD
    head_ones = jnp.where(blk_r == blk_c, 1.0, 0.0).astype(BF16)
    slot_of_lane = lax.broadcasted_iota(jnp.int32, (SUB, LANES), 1) & (RW_HEAD - 1)
    step_dir = jnp.where(d == 0, 1, -1)

    y_ref[...] = jnp.zeros_like(y_ref)

    def row(ref, p, t, *lead):
        return ref[(*lead, p, pl.ds(t, SUB, stride=0), slice(None))]

    def as_rows(tile):
        return tile.reshape(RW_HEAD, LANES).astype(BF16)

    def head_sums(states, pairs, t_done, t_next):
        rows = []
        for p in pairs:
            rows.append(as_rows(states[p] * row(r_ref, p, t_done)))
            if t_next is not None:
                rows.append(as_rows(states[p] * row(kk_ref, p, t_next)))
                rows.append(as_rows(dup * row(v_ref, p, t_next)))
        return jnp.dot(jnp.concatenate(rows, axis=0), head_ones, preferred_element_type=F32)

    def slab(sums, q, n_slabs, i):
        lo = (q * n_slabs + i) * RW_HEAD
        return sums[lo:lo + RW_HEAD].reshape(SUB, SUB, LANES)

    def put_y(p, y_b, t, valid):
        window = t // RW_HEAD
        keep = (slot_of_lane == t % RW_HEAD) & valid
        y_ref[0, p, window] = jnp.where(keep, y_b, y_ref[0, p, window])

    groups = [range(g0, g0 + RW_GROUP) for g0 in range(0, RW_PAIRS, RW_GROUP)]

    def block(g, carry):
        first = g * RW_UNROLL
        first = jnp.where(d == 0, first, RW_CHUNK - 1 - first)
        times = [first + u * step_dir for u in range(RW_UNROLL)]
        t_before = jnp.clip(first - step_dir, 0, RW_CHUNK - 1)
        states = [s_ref[p] for p in range(RW_PAIRS)]
        sums = [head_sums(states, grp, t_before, times[0]) for grp in groups]
        for gi, grp in enumerate(groups):
            for q, p in enumerate(grp):
                put_y(p, slab(sums[gi], q, 3, 0), t_before, g > 0)
        for u in range(RW_UNROLL):
            t = times[u]
            for gi, grp in enumerate(groups):
                for q, p in enumerate(grp):
                    sa = slab(sums[gi], q, 3, 1)
                    vb = slab(sums[gi], q, 3, 2)
                    states[p] = (states[p] * row(w_ref, p, t, 0) - sa * row(b_ref, p, t, 0)
                                 + vb * row(k_ref, p, t, 0))
                if u + 1 < RW_UNROLL:
                    sums[gi] = head_sums(states, grp, t, times[u + 1])
                    for q, p in enumerate(grp):
                        put_y(p, slab(sums[gi], q, 3, 0), t, True)
        for p in range(RW_PAIRS):
            s_ref[p] = states[p]
        return carry

    lax.fori_loop(0, RW_CHUNK // RW_UNROLL, block, 0)

    t_last = jnp.where(d == 0, RW_CHUNK - 1, 0)
    states = [s_ref[p] for p in range(RW_PAIRS)]
    for grp in groups:
        last = head_sums(states, grp, t_last, None)
        for q, p in enumerate(grp):
            put_y(p, slab(last, q, 1, 0), t_last, True)

    first_head = lax.broadcasted_iota(jnp.int32, (RW_HEAD, LANES), 1) < RW_HEAD
    for p in range(0, RW_PAIRS, 2):
        for window in range(RW_CHUNK // RW_HEAD):
            both = jnp.concatenate([y_ref[0, p, window].reshape(RW_HEAD, LANES),
                                    y_ref[0, p + 1, window].reshape(RW_HEAD, LANES)], axis=0)
            by_row = both.T
            head0, head1 = by_row[:RW_HEAD], by_row[RW_HEAD:]
            rows = slice(window * RW_HEAD, (window + 1) * RW_HEAD)
            y_out[0, rows, p * LANES:(p + 1) * LANES] = jnp.where(
                first_head, head0, pltpu.roll(head1, RW_HEAD, axis=1))
            y_out[0, rows, (p + 1) * LANES:(p + 2) * LANES] = jnp.where(
                first_head, pltpu.roll(head0, RW_HEAD, axis=1), head1)


def _rwkv_scan(pr, kk, w, k, b):
    n_ctx = CTX_LEN // RW_CHUNK
    n_lat = SEQ // RW_CHUNK
    ctx0 = N_LAT_ROWS // RW_CHUNK

    def chunk(d, bb, c):
        j = jnp.where(d == 0, c, jnp.where(c < n_ctx, n_ctx - 1 - c, n_ctx + n_lat - 1 - (c - n_ctx)))
        return jnp.where(j < n_ctx, ctx0 + bb * n_ctx + j, bb * n_lat + j - n_ctx)

    shared = pl.BlockSpec((RW_CHUNK, RW_W), lambda d, bb, c: (chunk(d, bb, c), 0))
    per_dir = pl.BlockSpec((1, RW_CHUNK, RW_W), lambda d, bb, c: (d, chunk(d, bb, c), 0))
    staged = pltpu.VMEM((RW_PAIRS, RW_CHUNK, LANES), F32)
    staged_dir = pltpu.VMEM((1, RW_PAIRS, RW_CHUNK, LANES), F32)
    return pl.pallas_call(
        _rwkv_kernel,
        grid=(2, BATCH, n_ctx + n_lat),
        in_specs=[shared, shared, pl.BlockSpec((RW_CHUNK, RW_W), lambda d, bb, c: (chunk(d, bb, c), 2)),
                  per_dir, per_dir, per_dir],
        out_specs=per_dir,
        out_shape=jax.ShapeDtypeStruct((2, N_ROWS, RW_W), F32),
        scratch_shapes=[pltpu.VMEM((RW_PAIRS, RW_HEAD // SUB, SUB, LANES), F32),
                        staged, staged, staged, staged_dir, staged_dir, staged_dir,
                        pltpu.VMEM((1, RW_PAIRS, RW_CHUNK // RW_HEAD, SUB, SUB, LANES), F32)],
        compiler_params=_params("parallel", "parallel", "arbitrary"),
        name="rwkv7_scan",
    )(pr, kk, pr, w, k, b)


def _axial_rope_tables(n_tokens, rot_dim):
    n_rows = n_tokens // GRID_W
    row = jnp.repeat(jnp.arange(n_rows, dtype=F32), GRID_W)
    col = jnp.tile(jnp.arange(GRID_W, dtype=F32), n_rows)
    axis_dim = rot_dim // 2
    inv_freq = ROPE_THETA ** (-jnp.arange(0, axis_dim, 2, dtype=F32) / axis_dim)
    ang_r = row[:, None] * inv_freq
    ang_c = col[:, None] * inv_freq
    ang = jnp.concatenate([ang_r, ang_r, ang_c, ang_c], axis=-1)
    return jnp.cos(ang), jnp.sin(ang)


def _rotate_half(z):
    z1, z2 = jnp.split(z, 2, axis=-1)
    return jnp.concatenate([-z2, z1], axis=-1)


def _apply_rope(x, cos, sin):
    half = x.shape[-1] // 2
    rot = jnp.concatenate([_rotate_half(x[..., :half]), _rotate_half(x[..., half:])], axis=-1)
    return x * cos + rot * sin


def _rms_f32(x, g):
    return x * lax.rsqrt(jnp.mean(x * x, axis=-1, keepdims=True) + NORM_EPS) * g


def _rope_lat(x, rope):
    cos, sin = rope
    heads, dim = x.shape[1:]
    lat = x[:N_LAT_ROWS].reshape(BATCH, SEQ, heads, dim)
    lat = _apply_rope(lat, cos[None, :, None, :], sin[None, :, None, :]).reshape(N_LAT_ROWS, heads, dim)
    return lat if x.shape[0] == N_LAT_ROWS else jnp.concatenate([lat, x[N_LAT_ROWS:]], axis=0)


def _centred_shift_rows(p):
    def shift(z):
        prev = jnp.pad(z[:, :-1], ((0, 0), (1, 0), (0, 0)))
        nxt = jnp.pad(z[:, 1:], ((0, 0), (0, 1), (0, 0)))
        return 0.5 * (prev + nxt)
    c = p.shape[-1]
    lat = shift(p[:N_LAT_ROWS].reshape(BATCH, SEQ, c)).reshape(N_LAT_ROWS, c)
    ctx = shift(p[N_LAT_ROWS:].reshape(BATCH, CTX_LEN, c)).reshape(N_CTX_ROWS, c)
    return jnp.concatenate([lat, ctx], axis=0)


RW_SLAB = 512
RW_TAIL = GATE_RANK + 2 * DECAY_RANK + 2 * AAA_RANK
RW_ROW_TILE = 512


def _head_lane_sums(x):
    rows = x.shape[0]
    blk_r = lax.broadcasted_iota(jnp.int32, (LANES, LANES), 0) // RW_HEAD
    blk_c = lax.broadcasted_iota(jnp.int32, (LANES, LANES), 1) // RW_HEAD
    head_ones = jnp.where(blk_r == blk_c, 1.0, 0.0).astype(BF16)
    hi = x.astype(BF16)
    rest = x - hi.astype(F32)
    mid = rest.astype(BF16)
    lo = (rest - mid.astype(F32)).astype(BF16)
    pieces = jnp.concatenate([hi, mid, lo], axis=0)
    out = []
    for s in range(x.shape[1] // LANES):
        part = jnp.dot(pieces[:, s * LANES:(s + 1) * LANES], head_ones, preferred_element_type=F32)
        out.append(part[:rows] + part[rows:2 * rows] + part[2 * rows:])
    return jnp.concatenate(out, axis=1)


def _rwkv_prep_kernel(r_ref, k_ref, v_ref, tail_ref, up_w_ref, w0_ref, a0_ref, kk_gain_ref, ka_ref, rk_ref,
                      kk_out, decay_out, kdir_out, b_out, g_out, bonus_out):
    s_w = RW_SLAB
    tail = tail_ref[...]
    lane = lax.broadcasted_iota(jnp.int32, (1, RW_TAIL), 1)
    act = jnp.where(lane < GATE_RANK, jax.nn.sigmoid(tail),
                    jnp.where(lane < GATE_RANK + 2 * DECAY_RANK, jnp.tanh(tail), tail))
    up = jnp.dot(act.astype(BF16), up_w_ref[...], preferred_element_type=F32)
    g_out[...] = up[:, :s_w]
    r = r_ref[...]
    k = k_ref[...]
    kq = k * kk_gain_ref[...]
    kk = kq / jnp.maximum(jnp.sqrt(_head_lane_sums(kq * kq)), 1e-12)
    kk_out[...] = kk
    kdir_sum = jnp.zeros_like(k)
    for d in range(2):
        z = -(w0_ref[d] + up[:, (1 + d) * s_w:(2 + d) * s_w])
        softplus = jnp.maximum(z, 0.0) + jnp.log(1.0 + jnp.exp(-jnp.abs(z)))
        decay_out[d] = jnp.exp(-jnp.exp(-softplus - 0.5))
        a = jax.nn.sigmoid(a0_ref[d] + up[:, (3 + d) * s_w:(4 + d) * s_w])
        kd = k * (1.0 + (a - 1.0) * ka_ref[...])
        kdir_out[d] = kd
        b_out[d] = kk * a
        kdir_sum = kdir_sum + kd
    bonus_out[...] = _head_lane_sums(r * kdir_sum * rk_ref[...]) * v_ref[...]


def _rwkv_post_kernel(y_ref, g_ref, bonus_ref, lnw_ref, lnb_ref, o_ref):
    y = y_ref[0] + y_ref[1]
    mean = _head_lane_sums(y) * (1.0 / RW_HEAD)
    cen = y - mean
    var = _head_lane_sums(cen * cen) * (1.0 / RW_HEAD)
    yn = cen * lax.rsqrt(var + GN_EPS) * lnw_ref[...] + lnb_ref[...]
    o_ref[...] = ((yn + bonus_ref[...]) * g_ref[...]).astype(o_ref.dtype)


def _rwkv_mixer(pr, mu, w0, w_up, a0, a_up, g_up, k_k, k_a, r_k, ln_w, ln_b):
    pr = pr + (_centred_shift_rows(pr) - pr) * mu
    tail = pr[:, 3 * RW_W:]
    n_slab = RW_W // RW_SLAB
    up_w = jnp.zeros((RW_TAIL, 5, RW_W), F32)
    o = GATE_RANK
    up_w = up_w.at[:o, 0].set(g_up)
    for d in range(2):
        up_w = up_w.at[o + d * DECAY_RANK:o + (d + 1) * DECAY_RANK, 1 + d].set(w_up[d])
    o += 2 * DECAY_RANK
    for d in range(2):
        up_w = up_w.at[o + d * AAA_RANK:o + (d + 1) * AAA_RANK, 3 + d].set(a_up[d])
    up_w = up_w.reshape(RW_TAIL, 5, n_slab, RW_SLAB).transpose(2, 0, 1, 3).reshape(n_slab, RW_TAIL, 5 * RW_SLAB)
    up_w = up_w.astype(BF16)

    tile = RW_ROW_TILE
    col_blocks = RW_W // RW_SLAB
    slab = lambda first: pl.BlockSpec((tile, RW_SLAB), lambda i, j: (i, first + j))
    vec = pl.BlockSpec((1, RW_SLAB), lambda i, j: (0, j))
    vec2 = pl.BlockSpec((2, 1, RW_SLAB), lambda i, j: (0, 0, j))
    out1 = pl.BlockSpec((tile, RW_SLAB), lambda i, j: (i, j))
    out2 = pl.BlockSpec((2, tile, RW_SLAB), lambda i, j: (0, i, j))
    one = jax.ShapeDtypeStruct((N_ROWS, RW_W), F32)
    two = jax.ShapeDtypeStruct((2, N_ROWS, RW_W), F32)
    kk, decay, k_dir, b_dir, g, bonus = pl.pallas_call(
        _rwkv_prep_kernel,
        grid=(N_ROWS // tile, n_slab),
        in_specs=[slab(0), slab(col_blocks), slab(2 * col_blocks),
                  pl.BlockSpec((tile, RW_TAIL), lambda i, j: (i, 0)),
                  pl.BlockSpec((None, RW_TAIL, 5 * RW_SLAB), lambda i, j: (j, 0, 0)),
                  vec2, vec2, vec, vec, vec],
        out_specs=[out1, out2, out2, out2, out1, out1],
        out_shape=[one, two, two, two, one, one],
        compiler_params=_params("parallel", "parallel"),
        name="rwkv7_prep",
    )(pr, pr, pr, tail, up_w, w0.reshape(2, 1, RW_W), a0.reshape(2, 1, RW_W), k_k.reshape(1, RW_W),
      k_a.reshape(1, RW_W), r_k.reshape(1, RW_W))

    y2 = _rwkv_scan(pr, kk, decay, k_dir, b_dir)

    return pl.pallas_call(
        _rwkv_post_kernel,
        grid=(N_ROWS // tile, n_slab),
        in_specs=[out2, out1, out1, vec, vec],
        out_specs=out1,
        out_shape=jax.ShapeDtypeStruct((N_ROWS, RW_W), BF16),
        compiler_params=_params("parallel", "parallel"),
        name="rwkv7_post",
    )(y2, g, bonus, ln_w.reshape(1, RW_W), ln_b.reshape(1, RW_W))


def _even_mixer(a_all, rope, w_in, rw_args, q_g, k_g, need_ctx):
    p = _matmul(a_all, w_in.astype(BF16), out_dtype=F32, tm=1024, tn=896, tk=D_MODEL, name="even_w_in")
    rw = _rwkv_mixer(p[:, :RW_IN], *rw_args).astype(BF16)
    pa = p[:, RW_IN:]
    nq = GQ_HEADS * GQ_HEAD
    nkv = GQ_KV_HEADS * GQ_HEAD
    q = _rms_f32(pa[:, :nq].reshape(N_ROWS, GQ_HEADS, GQ_HEAD), q_g)
    k = _rms_f32(pa[:, nq:nq + nkv].reshape(N_ROWS, GQ_KV_HEADS, GQ_HEAD), k_g)
    v = pa[:, nq + nkv:].astype(BF16)
    q = (_rope_lat(q, rope) * (GQ_SCALE * LOG2E)).reshape(N_ROWS, nq).astype(BF16)
    k = _rope_lat(k, rope).reshape(N_ROWS, nkv).astype(BF16)
    at = _gqa_attention(q, k, v, latent=True)
    if need_ctx:
        at = jnp.concatenate([at, _gqa_attention(q, k, v, latent=False)], axis=0)
        return jnp.concatenate([rw, at], axis=-1)
    return jnp.concatenate([rw[:N_LAT_ROWS], at], axis=-1)


def _odd_mixer(a_all, rope, w_in, q_norm, q_up, kv_norm, kv_up, lq1, lk1, lq2, lk2, subln, layer_idx):
    w_in = jnp.pad(w_in.astype(BF16), ((0, 0), (0, IN_ODD_PADDED - IN_ODD)))
    p = _matmul(a_all, w_in, out_dtype=F32, tm=1024, tn=768, tk=D_MODEL, name="odd_w_in")
    lam_init = 0.8 - 0.6 * math.exp(-0.3 * layer_idx)
    lam = (jnp.exp(jnp.sum(lq1 * lk1).astype(F32)) - jnp.exp(jnp.sum(lq2 * lk2).astype(F32)) + lam_init)

    c_q = _rms_f32(p[:N_LAT_ROWS, :Q_LORA], q_norm).astype(BF16)
    c_kv = _rms_f32(p[:, Q_LORA:Q_LORA + KV_LORA], kv_norm).astype(BF16)
    k_pe = p[:, Q_LORA + KV_LORA:MLA_IN]
    q_up_h = q_up.reshape(Q_LORA, MLA_HEADS, MLA_NOPE + MLA_ROPE)
    q_up_r = jnp.concatenate([q_up_h[:, :, :MLA_NOPE].reshape(Q_LORA, -1),
                              q_up_h[:, :, MLA_NOPE:].reshape(Q_LORA, -1)], axis=1).astype(BF16)
    kv_up_h = kv_up.reshape(KV_LORA, MLA_HEADS, MLA_NOPE + MLA_V)
    kv_up_r = jnp.concatenate([kv_up_h[:, :, :MLA_NOPE].reshape(KV_LORA, -1),
                               kv_up_h[:, :, MLA_NOPE:].reshape(KV_LORA, -1)], axis=1).astype(BF16)
    q_all = _matmul(c_q, q_up_r, out_dtype=F32, tm=1024, tn=1536, tk=Q_LORA, name="mla_q_up")
    kv = _matmul(c_kv, kv_up_r, out_dtype=BF16, tm=1024, tn=2048, tk=KV_LORA, name="mla_kv_up")
    q_all = q_all * (MLA_SCALE * LOG2E)
    q_nope = q_all[:, :MLA_HEADS * MLA_NOPE].astype(BF16)
    q_pe = q_all[:, MLA_HEADS * MLA_NOPE:].reshape(N_LAT_ROWS, MLA_HEADS, MLA_ROPE)
    q_pe = _rope_lat(q_pe, rope).transpose(1, 0, 2).astype(BF16)
    q_pe = jnp.pad(q_pe, ((0, 0), (0, 0), (0, MLA_NOPE - MLA_ROPE)))
    k_pe = _rope_lat(k_pe.reshape(N_ROWS, 1, MLA_ROPE), rope).reshape(N_ROWS, MLA_ROPE).astype(BF16)
    k_pe = jnp.pad(k_pe, ((0, 0), (0, MLA_NOPE - MLA_ROPE)))
    m_out = _mla_attention(q_nope, q_pe, kv, k_pe)

    pd = p[:, MLA_IN:IN_ODD]
    qk_w = DIFF_HEADS * 2 * DIFF_HEAD
    dq = pd[:N_LAT_ROWS, :qk_w].reshape(N_LAT_ROWS, DIFF_HEADS * 2, DIFF_HEAD)
    dk = pd[:, qk_w:2 * qk_w].reshape(N_ROWS, DIFF_HEADS * 2, DIFF_HEAD)
    dv = pd[:, 2 * qk_w:].astype(BF16)
    dq = (_rope_lat(dq, rope) * (DIFF_SCALE * LOG2E)).reshape(N_LAT_ROWS, qk_w).astype(BF16)
    dk = _rope_lat(dk, rope).reshape(N_ROWS, qk_w).astype(BF16)
    d_out = _diff_attention(lam.reshape(1), dq, dk, dv, subln, 1.0 - lam_init)
    return jnp.concatenate([m_out, d_out], axis=-1)


def kernel(x, c, ctx, c_ctx, ada_w, ada_b, norm1_g, norm2_g, mlp_w1, mlp_w2, final_g, ev_w_in, ev_w_out, rw_mu, rw_w0, rw_w_up, rw_a0, rw_a_up, rw_g_up, rw_k_k, rw_k_a, rw_r_k, rw_ln_w, rw_ln_b, gq_q_norm, gq_k_norm, od_w_in, od_w_out, mla_q_norm, mla_q_up, mla_kv_norm, mla_kv_up, diff_lq1, diff_lk1, diff_lq2, diff_lk2, diff_subln):
    rope_gq = _axial_rope_tables(SEQ, GQ_HEAD)
    rope_64 = _axial_rope_tables(SEQ, MLA_ROPE)
    h = jnp.concatenate([x.reshape(N_LAT_ROWS, D_MODEL), ctx.reshape(N_CTX_ROWS, D_MODEL)], axis=0)
    cond = jnp.concatenate([c_ctx[None], c, jnp.zeros((3, D_MODEL), F32)], axis=0)
    cond = jax.nn.silu(cond).astype(BF16)
    for i in range(DEPTH):
        last = i == DEPTH - 1
        mod = _matmul(cond, ada_w[i], out_dtype=F32, tm=8, tn=2048, tk=1024, name="adaln_mod")
        mod = (mod + ada_b[i])[:BATCH + 1].reshape(BATCH + 1, N_MOD, 1, D_MODEL)
        sh1, sc1, g1, sh2, sc2, g2 = (mod[:, m] for m in range(N_MOD))
        a_all = _norm(h, norm1_g[i], sc1, sh1, rows=N_ROWS, out_dtype=BF16)
        j = i // 2
        if i % 2 == 0:
            rw_args = (rw_mu[j], rw_w0[j], rw_w_up[j], rw_a0[j], rw_a_up[j], rw_g_up[j], rw_k_k[j], rw_k_a[j],
                       rw_r_k[j], rw_ln_w[j], rw_ln_b[j])
            mix = _even_mixer(a_all, rope_gq, ev_w_in[j], rw_args, gq_q_norm[j], gq_k_norm[j], not last)
            w_out = ev_w_out[j]
        else:
            if not last:
                raise NotImplementedError("context rows of an odd layer are only needed when a layer follows")
            mix = _odd_mixer(a_all, rope_64, od_w_in[j], mla_q_norm[j], mla_q_up[j], mla_kv_norm[j],
                             mla_kv_up[j], diff_lq1[j], diff_lk1[j], diff_lq2[j], diff_lk2[j], diff_subln[j], i)
            w_out = od_w_out[j]
        rows = N_LAT_ROWS if last else N_ROWS
        h = _matmul(mix, w_out.astype(BF16), out_dtype=F32, tm=1024, tn=1024, tk=2048, rows=rows,
                    epilogue="gated_residual", res=h, gate=g1, name="mixer_w_out")
        a2 = _norm(h, norm2_g[i], sc2, sh2, rows=rows, out_dtype=BF16)
        hid = _matmul(a2, mlp_w1[i], out_dtype=BF16, tm=1024, tn=512, tk=D_MODEL,
                      epilogue="relu2", name="mlp_w1")
        h = _matmul(hid, mlp_w2[i], out_dtype=F32, tm=1024, tn=1024, tk=2048,
                    epilogue="gated_residual", res=h, gate=g2, name="mlp_w2")
    out = _norm(h, final_g, rows=N_LAT_ROWS, out_dtype=F32)
    return out.reshape(BATCH, SEQ, D_MODEL)
```

```python
import functools
import math

import jax
import jax.numpy as jnp
from jax import lax
from jax.experimental import pallas as pl
from jax.experimental.pallas import tpu as pltpu

D_MODEL = 4096
BATCH = 4
SEQ = 4096
DEPTH = 2
CTX_LEN = 256
GRID_W = 64
ROPE_THETA = 10000.0
NORM_EPS = 1e-6
MLP_HIDDEN = 4 * D_MODEL
N_MOD = 6
HALF = D_MODEL // 2

RW_HEAD = 64
RW_W = HALF
RW_HEADS = RW_W // RW_HEAD
DECAY_RANK = 96
AAA_RANK = 96
GATE_RANK = 256
GN_EPS = 64e-5
RW_IN = 3 * RW_W + GATE_RANK + 2 * DECAY_RANK + 2 * AAA_RANK

GQ_HEAD = 128
GQ_HEADS = HALF // GQ_HEAD
GQ_KV_HEADS = GQ_HEADS // 4
GQ_GROUP = GQ_HEADS // GQ_KV_HEADS
GQ_SCALE = GQ_HEAD ** -0.5
IN_EVEN = RW_IN + (GQ_HEADS + 2 * GQ_KV_HEADS) * GQ_HEAD

MLA_NOPE = 128
MLA_ROPE = 64
MLA_V = 128
MLA_HEADS = HALF // MLA_V
Q_LORA = 768
KV_LORA = 512
MLA_SCALE = (MLA_NOPE + MLA_ROPE) ** -0.5
MLA_IN = Q_LORA + KV_LORA + MLA_ROPE

DIFF_HEAD = 64
DIFF_V = 2 * DIFF_HEAD
DIFF_HEADS = HALF // DIFF_V
DIFF_SCALE = DIFF_HEAD ** -0.5
DIFF_IN = DIFF_HEADS * (4 * DIFF_HEAD + DIFF_V)
IN_ODD = MLA_IN + DIFF_IN
IN_ODD_PADDED = 7680

N_LAT_ROWS = BATCH * SEQ
N_CTX_ROWS = BATCH * CTX_LEN
N_ROWS = N_LAT_ROWS + N_CTX_ROWS
T_ALL = CTX_LEN + SEQ
CTX_BLOCK0 = N_LAT_ROWS // CTX_LEN

V7X_VMEM_LIMIT_BYTES = 56 * 1024 * 1024

BF16 = jnp.bfloat16
F32 = jnp.float32


def _params(*sem):
    return pltpu.CompilerParams(dimension_semantics=sem, vmem_limit_bytes=V7X_VMEM_LIMIT_BYTES)


def _row_group(row_tile, tile_rows):
    start = row_tile * tile_rows
    return jnp.where(start < N_LAT_ROWS, 1 + start // SEQ, 0)


def _norm_kernel(x_ref, g_ref, *rest, modulate):
    if modulate:
        sc_ref, sh_ref, o_ref = rest
    else:
        (o_ref,) = rest
    x = x_ref[...]
    ms = jnp.mean(x * x, axis=-1, keepdims=True)
    y = x * lax.rsqrt(ms + NORM_EPS) * g_ref[...]
    if modulate:
        y = y * (1.0 + sc_ref[0]) + sh_ref[0]
    o_ref[...] = y.astype(o_ref.dtype)


def _norm(x, g, sc=None, sh=None, *, rows, out_dtype, tile=256):
    d = x.shape[1]
    modulate = sc is not None
    in_specs = [pl.BlockSpec((tile, d), lambda i: (i, 0)),
                pl.BlockSpec((1, d), lambda i: (0, 0))]
    args = [x, g.reshape(1, d)]
    if modulate:
        mod_spec = pl.BlockSpec((1, 1, d), lambda i: (_row_group(i, tile), 0, 0))
        in_specs += [mod_spec, mod_spec]
        args += [sc, sh]
    return pl.pallas_call(
        functools.partial(_norm_kernel, modulate=modulate),
        grid=(rows // tile,),
        in_specs=in_specs,
        out_specs=pl.BlockSpec((tile, d), lambda i: (i, 0)),
        out_shape=jax.ShapeDtypeStruct((rows, d), out_dtype),
        compiler_params=_params("parallel"),
        name="rmsnorm_mod" if modulate else "rmsnorm",
    )(*args)


def _mm_kernel(a_ref, w_ref, *rest, epilogue, nk):
    if epilogue == "gated_residual":
        res_ref, gate_ref, o_ref, acc_ref = rest
    else:
        o_ref, acc_ref = rest
    k = pl.program_id(2)
    part = jnp.dot(a_ref[...].astype(BF16), w_ref[...].astype(BF16), preferred_element_type=F32)

    @pl.when(k == 0)
    def _():
        acc_ref[...] = part

    @pl.when(k > 0)
    def _():
        acc_ref[...] += part

    @pl.when(k == nk - 1)
    def _():
        acc = acc_ref[...]
        if epilogue == "relu2":
            acc = jnp.square(jnp.maximum(acc, 0.0))
        elif epilogue == "gated_residual":
            acc = res_ref[...] + gate_ref[0] * acc
        o_ref[...] = acc.astype(o_ref.dtype)


def _matmul(a, w, *, out_dtype, tm, tn, tk, rows=None, layer=None, epilogue=None, res=None, gate=None, name):
    kdim, n = w.shape[-2:]
    m = a.shape[0] if rows is None else rows
    assert a.shape[1] == kdim and m % tm == 0 and n % tn == 0 and kdim % tk == 0
    assert (w.ndim == 3) == (layer is not None)
    nk = kdim // tk
    w_spec = (pl.BlockSpec((tk, tn), lambda i, j, k: (k, j)) if layer is None else
              pl.BlockSpec((None, tk, tn), lambda i, j, k: (layer, k, j)))
    in_specs = [pl.BlockSpec((tm, tk), lambda i, j, k: (i, k)), w_spec]
    args = [a, w]
    if epilogue == "gated_residual":
        in_specs += [pl.BlockSpec((tm, tn), lambda i, j, k: (i, j)),
                     pl.BlockSpec((1, 1, tn), lambda i, j, k: (_row_group(i, tm), 0, j))]
        args += [res, gate]
    return pl.pallas_call(
        functools.partial(_mm_kernel, epilogue=epilogue, nk=nk),
        grid=(m // tm, n // tn, nk),
        in_specs=in_specs,
        out_specs=pl.BlockSpec((tm, tn), lambda i, j, k: (i, j)),
        out_shape=jax.ShapeDtypeStruct((m, n), out_dtype),
        scratch_shapes=[pltpu.VMEM((tm, tn), F32)],
        compiler_params=_params("parallel", "parallel", "arbitrary"),
        name=name,
    )(*args)


def _nt_dot(a, b):
    return lax.dot_general(a, b, (((1,), (1,)), ((), ())), preferred_element_type=F32)


LOG2E = math.log2(math.e)


def _fill_values_and_ones(v_ref, vo_ref):
    d = v_ref.shape[1]
    vo_ref[:, :d] = v_ref[...]
    vo_ref[:, d:] = jnp.ones_like(v_ref)


def _softmax_times_values(s_list, vo_refs):
    d = vo_refs[0].shape[1] // 2
    m = functools.reduce(jnp.maximum, [jnp.max(s, axis=-1, keepdims=True) for s in s_list])
    acc = functools.reduce(jnp.add, [jnp.dot(jnp.exp2(s - m).astype(BF16), vo_ref[...], preferred_element_type=F32)
                                     for s, vo_ref in zip(s_list, vo_refs)])
    return acc[:, :d] / acc[:, d:]


def _attend_each(items):
    outs = []
    pending = None
    for q, k_refs, vo_refs in items:
        s_list = [_nt_dot(q, k_ref[...]) for k_ref in k_refs]
        if pending is not None:
            outs.append(_softmax_times_values(*pending))
        pending = (s_list, vo_refs)
    outs.append(_softmax_times_values(*pending))
    return outs


def _gqa_kernel(q_ref, *rest, with_lat):
    if with_lat:
        kc_ref, vc_ref, kl_ref, vl_ref, o_ref, voc_ref, vol_ref = rest
        k_refs, v_refs, vo_refs = (kc_ref, kl_ref), (vc_ref, vl_ref), (voc_ref, vol_ref)
    else:
        kc_ref, vc_ref, o_ref, voc_ref = rest
        k_refs, v_refs, vo_refs = (kc_ref,), (vc_ref,), (voc_ref,)

    @pl.when(pl.program_id(2) == 0)
    def _():
        for v_ref, vo_ref in zip(v_refs, vo_refs):
            _fill_values_and_ones(v_ref, vo_ref)

    outs = _attend_each([(q_ref[:, g * GQ_HEAD:(g + 1) * GQ_HEAD], k_refs, vo_refs) for g in range(GQ_GROUP)])
    for g, o in enumerate(outs):
        o_ref[:, g * GQ_HEAD:(g + 1) * GQ_HEAD] = o.astype(o_ref.dtype)


def _gqa_attention(q, k, v, *, latent, tq=256):
    gw = GQ_GROUP * GQ_HEAD
    ctx_kv = pl.BlockSpec((CTX_LEN, GQ_HEAD), lambda b, n, i: (CTX_BLOCK0 + b, n))
    scratch = [pltpu.VMEM((CTX_LEN, 2 * GQ_HEAD), BF16)]
    if latent:
        per_b = SEQ // tq
        lat_kv = pl.BlockSpec((SEQ, GQ_HEAD), lambda b, n, i: (b, n))
        in_specs = [pl.BlockSpec((tq, gw), lambda b, n, i: (b * per_b + i, n)), ctx_kv, ctx_kv, lat_kv, lat_kv]
        args = (q, k, v, k, v)
        rows = N_LAT_ROWS
        scratch.append(pltpu.VMEM((SEQ, 2 * GQ_HEAD), BF16))
    else:
        per_b = CTX_LEN // tq
        q_blk0 = N_LAT_ROWS // tq
        in_specs = [pl.BlockSpec((tq, gw), lambda b, n, i: (q_blk0 + b * per_b + i, n)), ctx_kv, ctx_kv]
        args = (q, k, v)
        rows = N_CTX_ROWS
    return pl.pallas_call(
        functools.partial(_gqa_kernel, with_lat=latent),
        grid=(BATCH, GQ_KV_HEADS, per_b),
        in_specs=in_specs,
        out_specs=pl.BlockSpec((tq, gw), lambda b, n, i: (b * per_b + i, n)),
        out_shape=jax.ShapeDtypeStruct((rows, GQ_HEADS * GQ_HEAD), BF16),
        scratch_shapes=scratch,
        compiler_params=_params("parallel", "parallel", "arbitrary"),
        name="gqa_attention_lat" if latent else "gqa_attention_ctx",
    )(*args)


def _mla_kernel(qn_ref, qp_ref, knc_ref, kpc_ref, vc_ref, knl_ref, kpl_ref, vl_ref, o_ref,
                kc_ref, kl_ref, voc_ref, vol_ref):
    heads = range(MLA_HEADS_PER_STEP)
    cols = [slice(j * MLA_NOPE, (j + 1) * MLA_NOPE) for j in heads]

    @pl.when(pl.program_id(2) == 0)
    def _():
        for kn_ref, kp_ref, v_ref, k_ref, vo_ref in ((knc_ref, kpc_ref, vc_ref, kc_ref, voc_ref),
                                                     (knl_ref, kpl_ref, vl_ref, kl_ref, vol_ref)):
            for j in heads:
                k_ref[j, :, :MLA_NOPE] = kn_ref[:, cols[j]]
                k_ref[j, :, MLA_NOPE:] = kp_ref[...]
                vo_ref[j, :, :MLA_V] = v_ref[:, cols[j]]
                vo_ref[j, :, MLA_V:] = jnp.ones_like(kp_ref)

    items = [(jnp.concatenate([qn_ref[:, cols[j]], qp_ref[j]], axis=1), (kc_ref.at[j], kl_ref.at[j]),
              (voc_ref.at[j], vol_ref.at[j])) for j in heads]
    for j, o in enumerate(_attend_each(items)):
        o_ref[:, cols[j]] = o.astype(o_ref.dtype)


MLA_HEADS_PER_STEP = 2


def _mla_attention(q_nope, q_pe, kv, k_pe, *, tq=256):
    per_b = SEQ // tq
    hs = MLA_HEADS_PER_STEP
    w = hs * MLA_NOPE
    v_blk0 = MLA_HEADS // hs
    in_specs = [
        pl.BlockSpec((tq, w), lambda b, h, i: (b * per_b + i, h)),
        pl.BlockSpec((hs, tq, MLA_NOPE), lambda b, h, i: (h, b * per_b + i, 0)),
        pl.BlockSpec((CTX_LEN, w), lambda b, h, i: (CTX_BLOCK0 + b, h)),
        pl.BlockSpec((CTX_LEN, MLA_NOPE), lambda b, h, i: (CTX_BLOCK0 + b, 0)),
        pl.BlockSpec((CTX_LEN, w), lambda b, h, i: (CTX_BLOCK0 + b, v_blk0 + h)),
        pl.BlockSpec((SEQ, w), lambda b, h, i: (b, h)),
        pl.BlockSpec((SEQ, MLA_NOPE), lambda b, h, i: (b, 0)),
        pl.BlockSpec((SEQ, w), lambda b, h, i: (b, v_blk0 + h)),
    ]
    return pl.pallas_call(
        _mla_kernel,
        grid=(BATCH, MLA_HEADS // hs, per_b),
        in_specs=in_specs,
        out_specs=pl.BlockSpec((tq, w), lambda b, h, i: (b * per_b + i, h)),
        out_shape=jax.ShapeDtypeStruct((N_LAT_ROWS, MLA_HEADS * MLA_V), BF16),
        scratch_shapes=[pltpu.VMEM((hs, CTX_LEN, 2 * MLA_NOPE), BF16), pltpu.VMEM((hs, SEQ, 2 * MLA_NOPE), BF16),
                        pltpu.VMEM((hs, CTX_LEN, 2 * MLA_V), BF16), pltpu.VMEM((hs, SEQ, 2 * MLA_V), BF16)],
        compiler_params=_params("parallel", "parallel", "arbitrary"),
        name="mla_attention",
    )(q_nope, q_pe, kv, k_pe, kv, kv, k_pe, kv)


def _diff_kernel(lam_ref, q_ref, kc_ref, vc_ref, kl_ref, vl_ref, g_ref, o_ref, voc_ref, vol_ref, *, out_scale):
    heads = range(DIFF_HEADS_PER_STEP)
    cols = [slice(j * DIFF_V, (j + 1) * DIFF_V) for j in heads]

    @pl.when(pl.program_id(2) == 0)
    def _():
        for v_ref, vo_ref in ((vc_ref, voc_ref), (vl_ref, vol_ref)):
            for j in heads:
                vo_ref[j, :, :DIFF_V] = v_ref[:, cols[j]]
                vo_ref[j, :, DIFF_V:] = jnp.ones((v_ref.shape[0], DIFF_V), BF16)

    first = lax.broadcasted_iota(jnp.int32, (q_ref.shape[0], DIFF_V), 1) < DIFF_HEAD
    items = []
    for j in heads:
        q = q_ref[:, cols[j]]
        zero = jnp.zeros_like(q)
        keys = (kc_ref[:, cols[j]], kl_ref[:, cols[j]])
        for qh in (jnp.where(first, q, zero), jnp.where(first, zero, q)):
            items.append((qh, keys, (voc_ref.at[j], vol_ref.at[j])))
    outs = _attend_each(items)
    for j in heads:
        o = outs[2 * j] - lam_ref[0] * outs[2 * j + 1]
        ms = jnp.mean(o * o, axis=-1, keepdims=True)
        o = o * lax.rsqrt(ms + NORM_EPS) * g_ref[...]
        o_ref[:, cols[j]] = (o * out_scale).astype(o_ref.dtype)


DIFF_HEADS_PER_STEP = 2


def _diff_attention(lam, q, k, v, subln, out_scale, *, tq=256):
    per_b = SEQ // tq
    hs = DIFF_HEADS_PER_STEP
    w = hs * DIFF_V
    ctx_kv = pl.BlockSpec((CTX_LEN, w), lambda b, h, i: (CTX_BLOCK0 + b, h))
    lat_kv = pl.BlockSpec((SEQ, w), lambda b, h, i: (b, h))
    in_specs = [
        pl.BlockSpec(memory_space=pltpu.SMEM),
        pl.BlockSpec((tq, w), lambda b, h, i: (b * per_b + i, h)),
        ctx_kv, ctx_kv, lat_kv, lat_kv,
        pl.BlockSpec((1, DIFF_V), lambda b, h, i: (0, 0)),
    ]
    return pl.pallas_call(
        functools.partial(_diff_kernel, out_scale=out_scale),
        grid=(BATCH, DIFF_HEADS // hs, per_b),
        in_specs=in_specs,
        out_specs=pl.BlockSpec((tq, w), lambda b, h, i: (b * per_b + i, h)),
        out_shape=jax.ShapeDtypeStruct((N_LAT_ROWS, DIFF_HEADS * DIFF_V), BF16),
        scratch_shapes=[pltpu.VMEM((hs, CTX_LEN, 2 * DIFF_V), BF16), pltpu.VMEM((hs, SEQ, 2 * DIFF_V), BF16)],
        compiler_params=_params("parallel", "parallel", "arbitrary"),
        name="diff_attention",
    )(lam, q, k, v, k, v, subln.reshape(1, DIFF_V))


RW_CHUNK = 128
RW_PAIRS = RW_HEADS // 2
LANES = 2 * RW_HEAD
SUB = 8
RW_UNROLL = 8
RW_GROUP = 8


def _rwkv_kernel(r_in, kk_in, v_in, w_in, k_in, b_in, y_out, s_ref, r_ref, kk_ref, v_ref, w_ref, k_ref, b_ref,
                 y_ref):
    d = pl.program_id(0)
    c = pl.program_id(2)

    @pl.when(c == 0)
    def _():
        s_ref[...] = jnp.zeros_like(s_ref)

    for p in range(RW_PAIRS):
        cols = slice(p * LANES, (p + 1) * LANES)
        for src, dst in ((r_in, r_ref), (kk_in, kk_ref), (v_in, v_ref)):
            dst[p] = src[:, cols]
        for src, dst in ((w_in, w_ref), (k_in, k_ref), (b_in, b_ref)):
            dst[0, p] = src[0, :, cols]

    v_hi = lax.broadcasted_iota(jnp.int32, (SUB, SUB, LANES), 0)
    v_lo = lax.broadcasted_iota(jnp.int32, (SUB, SUB, LANES), 1)
    lane3 = lax.broadcasted_iota(jnp.int32, (SUB, SUB, LANES), 2)
    dup = jnp.where((lane3 & (RW_HEAD - 1)) == v_hi * SUB + v_lo, 1.0, 0.0)
    blk_r = lax.broadcasted_iota(jnp.int32, (LANES, LANES), 0) // RW_HEAD
    blk_c = lax.broadcasted_iota(jnp.int32, (LANES, LANES), 1) // RW_HEAD
    head_ones = jnp.where(blk_r == blk_c, 1.0, 0.0).astype(BF16)
    slot_of_lane = lax.broadcasted_iota(jnp.int32, (SUB, LANES), 1) & (RW_HEAD - 1)
    step_dir = jnp.where(d == 0, 1, -1)

    y_ref[...] = jnp.zeros_like(y_ref)

    def row(ref, p, t, *lead):
        return ref[(*lead, p, pl.ds(t, SUB, stride=0), slice(None))]

    def as_rows(tile):
        return tile.reshape(RW_HEAD, LANES).astype(BF16)

    def head_sums(states, pairs, t_done, t_next):
        rows = []
        for p in pairs:
            rows.append(as_rows(states[p] * row(r_ref, p, t_done)))
            if t_next is not None:
                rows.append(as_rows(states[p] * row(kk_ref, p, t_next)))
                rows.append(as_rows(dup * row(v_ref, p, t_next)))
        return jnp.dot(jnp.concatenate(rows, axis=0), head_ones, preferred_element_type=F32)

    def slab(sums, q, n_slabs, i):
        lo = (q * n_slabs + i) * RW_HEAD
        return sums[lo:lo + RW_HEAD].reshape(SUB, SUB, LANES)

    def put_y(p, y_b, t, valid):
        window = t // RW_HEAD
        keep = (slot_of_lane == t % RW_HEAD) & valid
        y_ref[0, p, window] = jnp.where(keep, y_b, y_ref[0, p, window])

    groups = [range(g0, g0 + RW_GROUP) for g0 in range(0, RW_PAIRS, RW_GROUP)]

    def block(g, carry):
        first = g * RW_UNROLL
        first = jnp.where(d == 0, first, RW_CHUNK - 1 - first)
        times = [first + u * step_dir for u in range(RW_UNROLL)]
        t_before = jnp.clip(first - step_dir, 0, RW_CHUNK - 1)
        states = [s_ref[p] for p in range(RW_PAIRS)]
        sums = [head_sums(states, grp, t_before, times[0]) for grp in groups]
        for gi, grp in enumerate(groups):
            for q, p in enumerate(grp):
                put_y(p, slab(sums[gi], q, 3, 0), t_before, g > 0)
        for u in range(RW_UNROLL):
            t = times[u]
            for gi, grp in enumerate(groups):
                for q, p in enumerate(grp):
                    sa = slab(sums[gi], q, 3, 1)
                    vb = slab(sums[gi], q, 3, 2)
                    states[p] = (states[p] * row(w_ref, p, t, 0) - sa * row(b_ref, p, t, 0)
                                 + vb * row(k_ref, p, t, 0))
                if u + 1 < RW_UNROLL:
                    sums[gi] = head_sums(states, grp, t, times[u + 1])
                    for q, p in enumerate(grp):
                        put_y(p, slab(sums[gi], q, 3, 0), t, True)
        for p in range(RW_PAIRS):
            s_ref[p] = states[p]
        return carry

    lax.fori_loop(0, RW_CHUNK // RW_UNROLL, block, 0)

    t_last = jnp.where(d == 0, RW_CHUNK - 1, 0)
    states = [s_ref[p] for p in range(RW_PAIRS)]
    for grp in groups:
        last = head_sums(states, grp, t_last, None)
        for q, p in enumerate(grp):
            put_y(p, slab(last, q, 1, 0), t_last, True)

    first_head = lax.broadcasted_iota(jnp.int32, (RW_HEAD, LANES), 1) < RW_HEAD
    for p in range(0, RW_PAIRS, 2):
        for window in range(RW_CHUNK // RW_HEAD):
            both = jnp.concatenate([y_ref[0, p, window].reshape(RW_HEAD, LANES),
                                    y_ref[0, p + 1, window].reshape(RW_HEAD, LANES)], axis=0)
            by_row = both.T
            head0, head1 = by_row[:RW_HEAD], by_row[RW_HEAD:]
            rows = slice(window * RW_HEAD, (window + 1) * RW_HEAD)
            y_out[0, rows, p * LANES:(p + 1) * LANES] = jnp.where(
                first_head, head0, pltpu.roll(head1, RW_HEAD, axis=1))
            y_out[0, rows, (p + 1) * LANES:(p + 2) * LANES] = jnp.where(
                first_head, pltpu.roll(head0, RW_HEAD, axis=1), head1)


def _rwkv_scan(r, kk, v, w, k, b):
    n_ctx = CTX_LEN // RW_CHUNK
    n_lat = SEQ // RW_CHUNK
    ctx0 = N_LAT_ROWS // RW_CHUNK

    def chunk(d, bb, c):
        j = jnp.where(d == 0, c, jnp.where(c < n_ctx, n_ctx - 1 - c, n_ctx + n_lat - 1 - (c - n_ctx)))
        return jnp.where(j < n_ctx, ctx0 + bb * n_ctx + j, bb * n_lat + j - n_ctx)

    shared = pl.BlockSpec((RW_CHUNK, RW_W), lambda d, bb, c: (chunk(d, bb, c), 0))
    per_dir = pl.BlockSpec((1, RW_CHUNK, RW_W), lambda d, bb, c: (d, chunk(d, bb, c), 0))
    staged = pltpu.VMEM((RW_PAIRS, RW_CHUNK, LANES), F32)
    staged_dir = pltpu.VMEM((1, RW_PAIRS, RW_CHUNK, LANES), F32)
    return pl.pallas_call(
        _rwkv_kernel,
        grid=(2, BATCH, n_ctx + n_lat),
        in_specs=[shared, shared, shared, per_dir, per_dir, per_dir],
        out_specs=per_dir,
        out_shape=jax.ShapeDtypeStruct((2, N_ROWS, RW_W), F32),
        scratch_shapes=[pltpu.VMEM((RW_PAIRS, RW_HEAD // SUB, SUB, LANES), F32),
                        staged, staged, staged, staged_dir, staged_dir, staged_dir,
                        pltpu.VMEM((1, RW_PAIRS, RW_CHUNK // RW_HEAD, SUB, SUB, LANES), F32)],
        compiler_params=_params("parallel", "parallel", "arbitrary"),
        name="rwkv7_scan",
    )(r, kk, v, w, k, b)


def _axial_rope_tables(n_tokens, rot_dim):
    n_rows = n_tokens // GRID_W
    row = jnp.repeat(jnp.arange(n_rows, dtype=F32), GRID_W)
    col = jnp.tile(jnp.arange(GRID_W, dtype=F32), n_rows)
    axis_dim = rot_dim // 2
    inv_freq = ROPE_THETA ** (-jnp.arange(0, axis_dim, 2, dtype=F32) / axis_dim)
    ang_r = row[:, None] * inv_freq
    ang_c = col[:, None] * inv_freq
    ang = jnp.concatenate([ang_r, ang_r, ang_c, ang_c], axis=-1)
    return jnp.cos(ang), jnp.sin(ang)


def _rotate_half(z):
    z1, z2 = jnp.split(z, 2, axis=-1)
    return jnp.concatenate([-z2, z1], axis=-1)


def _apply_rope(x, cos, sin):
    half = x.shape[-1] // 2
    rot = jnp.concatenate([_rotate_half(x[..., :half]), _rotate_half(x[..., half:])], axis=-1)
    return x * cos + rot * sin


def _rms_f32(x, g):
    return x * lax.rsqrt(jnp.mean(x * x, axis=-1, keepdims=True) + NORM_EPS) * g


def _rope_lat(x, rope):
    cos, sin = rope
    heads, dim = x.shape[1:]
    lat = x[:N_LAT_ROWS].reshape(BATCH, SEQ, heads, dim)
    lat = _apply_rope(lat, cos[None, :, None, :], sin[None, :, None, :]).reshape(N_LAT_ROWS, heads, dim)
    return lat if x.shape[0] == N_LAT_ROWS else jnp.concatenate([lat, x[N_LAT_ROWS:]], axis=0)


def _centred_shift_rows(p):
    def shift(z):
        prev = jnp.pad(z[:, :-1], ((0, 0), (1, 0), (0, 0)))
        nxt = jnp.pad(z[:, 1:], ((0, 0), (0, 1), (0, 0)))
        return 0.5 * (prev + nxt)
    c = p.shape[-1]
    lat = shift(p[:N_LAT_ROWS].reshape(BATCH, SEQ, c)).reshape(N_LAT_ROWS, c)
    ctx = shift(p[N_LAT_ROWS:].reshape(BATCH, CTX_LEN, c)).reshape(N_CTX_ROWS, c)
    return jnp.concatenate([lat, ctx], axis=0)


RW_SLAB = 512
RW_TAIL = GATE_RANK + 2 * DECAY_RANK + 2 * AAA_RANK
RW_TAIL_PAD = 768
RW_ROW_TILE = 256


def _head_lane_sums(x):
    rows = x.shape[0]
    blk_r = lax.broadcasted_iota(jnp.int32, (LANES, LANES), 0) // RW_HEAD
    blk_c = lax.broadcasted_iota(jnp.int32, (LANES, LANES), 1) // RW_HEAD
    head_ones = jnp.where(blk_r == blk_c, 1.0, 0.0).astype(BF16)
    hi = x.astype(BF16)
    rest = x - hi.astype(F32)
    mid = rest.astype(BF16)
    lo = (rest - mid.astype(F32)).astype(BF16)
    pieces = jnp.concatenate([hi, mid, lo], axis=0)
    out = []
    for s in range(x.shape[1] // LANES):
        part = jnp.dot(pieces[:, s * LANES:(s + 1) * LANES], head_ones, preferred_element_type=F32)
        out.append(part[:rows] + part[rows:2 * rows] + part[2 * rows:])
    return jnp.concatenate(out, axis=1)


def _rwkv_prep_kernel(*refs):
    slabs = [refs[4 * n:4 * n + 4] for n in range(4)]
    (up_w_ref, w0_ref, a0_ref, kk_gain_ref, ka_ref, rk_ref,
     r_out, v_out, kk_out, decay_out, kdir_out, b_out, g_out, bonus_out) = refs[16:30]
    stage_refs = refs[30:34]
    tile = RW_ROW_TILE
    row0 = pl.program_id(0) * tile
    seq_len = jnp.where(row0 < N_LAT_ROWS, SEQ, CTX_LEN)
    starts_seq = row0 % seq_len == 0
    ends_seq = (row0 + tile) % seq_len == 0

    def shifted(main_ref, before_ref, after_ref, mu_ref, stage_ref):
        x = main_ref[...]
        stage_ref[SUB:SUB + tile, :] = x
        stage_ref[SUB - 1:SUB, :] = jnp.where(starts_seq, 0.0, before_ref[SUB - 1:SUB, :])
        stage_ref[SUB + tile:SUB + tile + 1, :] = jnp.where(ends_seq, 0.0, after_ref[0:1, :])
        around = 0.5 * (stage_ref[SUB - 1:SUB - 1 + tile, :] + stage_ref[SUB + 1:SUB + 1 + tile, :])
        return x + (around - x) * mu_ref[...]

    r, k, v, tail = (shifted(*slab, stage) for slab, stage in zip(slabs, stage_refs))
    r_out[...] = r
    v_out[...] = v
    s_w = RW_SLAB
    lane = lax.broadcasted_iota(jnp.int32, (1, RW_TAIL_PAD), 1)
    act = jnp.where(lane < GATE_RANK, jax.nn.sigmoid(tail),
                    jnp.where(lane < GATE_RANK + 2 * DECAY_RANK, jnp.tanh(tail), tail))
    up = jnp.dot(act.astype(BF16), up_w_ref[...], preferred_element_type=F32)
    g_out[...] = up[:, :s_w]
    kq = k * kk_gain_ref[...]
    kk = kq / jnp.maximum(jnp.sqrt(_head_lane_sums(kq * kq)), 1e-12)
    kk_out[...] = kk
    kdir_sum = jnp.zeros_like(k)
    for d in range(2):
        z = -(w0_ref[d] + up[:, (1 + d) * s_w:(2 + d) * s_w])
        softplus = jnp.maximum(z, 0.0) + jnp.log(1.0 + jnp.exp(-jnp.abs(z)))
        decay_out[d] = jnp.exp(-jnp.exp(-softplus - 0.5))
        a = jax.nn.sigmoid(a0_ref[d] + up[:, (3 + d) * s_w:(4 + d) * s_w])
        kd = k * (1.0 + (a - 1.0) * ka_ref[...])
        kdir_out[d] = kd
        b_out[d] = kk * a
        kdir_sum = kdir_sum + kd
    bonus_out[...] = _head_lane_sums(r * kdir_sum * rk_ref[...]) * v


def _rwkv_post_kernel(y_ref, g_ref, bonus_ref, lnw_ref, lnb_ref, o_ref):
    y = y_ref[0] + y_ref[1]
    mean = _head_lane_sums(y) * (1.0 / RW_HEAD)
    cen = y - mean
    var = _head_lane_sums(cen * cen) * (1.0 / RW_HEAD)
    yn = cen * lax.rsqrt(var + GN_EPS) * lnw_ref[...] + lnb_ref[...]
    o_ref[...] = ((yn + bonus_ref[...]) * g_ref[...]).astype(o_ref.dtype)


def _rwkv_mixer(p, mu, w0, w_up, a0, a_up, g_up, k_k, k_a, r_k, ln_w, ln_b):
    n_slab = RW_W // RW_SLAB
    up_w = jnp.zeros((RW_TAIL_PAD, 5, RW_W), F32)
    o = GATE_RANK
    up_w = up_w.at[:o, 0].set(g_up)
    for d in range(2):
        up_w = up_w.at[o + d * DECAY_RANK:o + (d + 1) * DECAY_RANK, 1 + d].set(w_up[d])
    o += 2 * DECAY_RANK
    for d in range(2):
        up_w = up_w.at[o + d * AAA_RANK:o + (d + 1) * AAA_RANK, 3 + d].set(a_up[d])
    up_w = up_w.reshape(RW_TAIL_PAD, 5, n_slab, RW_SLAB).transpose(2, 0, 1, 3)
    up_w = up_w.reshape(n_slab, RW_TAIL_PAD, 5 * RW_SLAB).astype(BF16)
    mu_main = mu[:3 * RW_W].reshape(1, 3 * RW_W)
    mu_tail = jnp.pad(mu[3 * RW_W:], (0, RW_TAIL_PAD - RW_TAIL)).reshape(1, RW_TAIL_PAD)

    tile = RW_ROW_TILE
    col_blocks = RW_W // RW_SLAB
    halo = tile // SUB
    last_halo = N_ROWS // SUB - 1

    def with_halo(width, col):
        return [pl.BlockSpec((tile, width), lambda i, j: (i, col(j))),
                pl.BlockSpec((SUB, width), lambda i, j: (jnp.maximum(i * halo - 1, 0), col(j))),
                pl.BlockSpec((SUB, width), lambda i, j: (jnp.minimum((i + 1) * halo, last_halo), col(j)))]

    in_specs, args = [], []
    for n in range(3):
        in_specs += with_halo(RW_SLAB, lambda j, n=n: n * col_blocks + j)
        in_specs.append(pl.BlockSpec((1, RW_SLAB), lambda i, j, n=n: (0, n * col_blocks + j)))
        args += [p, p, p, mu_main]
    tail_blk = 3 * RW_W // RW_TAIL_PAD
    in_specs += with_halo(RW_TAIL_PAD, lambda j: tail_blk)
    in_specs.append(pl.BlockSpec((1, RW_TAIL_PAD), lambda i, j: (0, 0)))
    args += [p, p, p, mu_tail]

    vec = pl.BlockSpec((1, RW_SLAB), lambda i, j: (0, j))
    vec2 = pl.BlockSpec((2, 1, RW_SLAB), lambda i, j: (0, 0, j))
    out1 = pl.BlockSpec((tile, RW_SLAB), lambda i, j: (i, j))
    out2 = pl.BlockSpec((2, tile, RW_SLAB), lambda i, j: (0, i, j))
    one = jax.ShapeDtypeStruct((N_ROWS, RW_W), F32)
    two = jax.ShapeDtypeStruct((2, N_ROWS, RW_W), F32)
    in_specs += [pl.BlockSpec((None, RW_TAIL_PAD, 5 * RW_SLAB), lambda i, j: (j, 0, 0)), vec2, vec2, vec, vec, vec]
    args += [up_w, w0.reshape(2, 1, RW_W), a0.reshape(2, 1, RW_W), k_k.reshape(1, RW_W), k_a.reshape(1, RW_W),
             r_k.reshape(1, RW_W)]
    stage = lambda width: pltpu.VMEM((tile + 2 * SUB, width), F32)
    r, v, kk, decay, k_dir, b_dir, g, bonus = pl.pallas_call(
        _rwkv_prep_kernel,
        grid=(N_ROWS // tile, n_slab),
        in_specs=in_specs,
        out_specs=[out1, out1, out1, out2, out2, out2, out1, out1],
        out_shape=[one, one, one, two, two, two, one, one],
        scratch_shapes=[stage(RW_SLAB), stage(RW_SLAB), stage(RW_SLAB), stage(RW_TAIL_PAD)],
        compiler_params=_params("parallel", "parallel"),
        name="rwkv7_prep",
    )(*args)

    y2 = _rwkv_scan(r, kk, v, decay, k_dir, b_dir)

    return pl.pallas_call(
        _rwkv_post_kernel,
        grid=(N_ROWS // tile, n_slab),
        in_specs=[out2, out1, out1, vec, vec],
        out_specs=out1,
        out_shape=jax.ShapeDtypeStruct((N_ROWS, RW_W), BF16),
        compiler_params=_params("parallel", "parallel"),
        name="rwkv7_post",
    )(y2, g, bonus, ln_w.reshape(1, RW_W), ln_b.reshape(1, RW_W))


def _even_mixer(a_all, rope, w_in, rw_args, q_g, k_g, need_ctx):
    p = _matmul(a_all, w_in.astype(BF16), out_dtype=F32, tm=1024, tn=896, tk=D_MODEL, name="even_w_in")
    rw = _rwkv_mixer(p, *rw_args)
    pa = p[:, RW_IN:]
    nq = GQ_HEADS * GQ_HEAD
    nkv = GQ_KV_HEADS * GQ_HEAD
    q = _rms_f32(pa[:, :nq].reshape(N_ROWS, GQ_HEADS, GQ_HEAD), q_g)
    k = _rms_f32(pa[:, nq:nq + nkv].reshape(N_ROWS, GQ_KV_HEADS, GQ_HEAD), k_g)
    v = pa[:, nq + nkv:].astype(BF16)
    q = (_rope_lat(q, rope) * (GQ_SCALE * LOG2E)).reshape(N_ROWS, nq).astype(BF16)
    k = _rope_lat(k, rope).reshape(N_ROWS, nkv).astype(BF16)
    at = _gqa_attention(q, k, v, latent=True)
    if need_ctx:
        at = jnp.concatenate([at, _gqa_attention(q, k, v, latent=False)], axis=0)
        return jnp.concatenate([rw, at], axis=-1)
    return jnp.concatenate([rw[:N_LAT_ROWS], at], axis=-1)


def _odd_mixer(a_all, rope, w_in, q_norm, q_up, kv_norm, kv_up, lq1, lk1, lq2, lk2, subln, layer_idx):
    w_in = jnp.pad(w_in.astype(BF16), ((0, 0), (0, IN_ODD_PADDED - IN_ODD)))
    p = _matmul(a_all, w_in, out_dtype=F32, tm=1024, tn=768, tk=D_MODEL, name="odd_w_in")
    lam_init = 0.8 - 0.6 * math.exp(-0.3 * layer_idx)
    lam = (jnp.exp(jnp.sum(lq1 * lk1).astype(F32)) - jnp.exp(jnp.sum(lq2 * lk2).astype(F32)) + lam_init)

    c_q = _rms_f32(p[:N_LAT_ROWS, :Q_LORA], q_norm).astype(BF16)
    c_kv = _rms_f32(p[:, Q_LORA:Q_LORA + KV_LORA], kv_norm).astype(BF16)
    k_pe = p[:, Q_LORA + KV_LORA:MLA_IN]
    q_up_h = q_up.reshape(Q_LORA, MLA_HEADS, MLA_NOPE + MLA_ROPE)
    q_up_r = jnp.concatenate([q_up_h[:, :, :MLA_NOPE].reshape(Q_LORA, -1),
                              q_up_h[:, :, MLA_NOPE:].reshape(Q_LORA, -1)], axis=1).astype(BF16)
    kv_up_h = kv_up.reshape(KV_LORA, MLA_HEADS, MLA_NOPE + MLA_V)
    kv_up_r = jnp.concatenate([kv_up_h[:, :, :MLA_NOPE].reshape(KV_LORA, -1),
                               kv_up_h[:, :, MLA_NOPE:].reshape(KV_LORA, -1)], axis=1).astype(BF16)
    q_all = _matmul(c_q, q_up_r, out_dtype=F32, tm=1024, tn=1536, tk=Q_LORA, name="mla_q_up")
    kv = _matmul(c_kv, kv_up_r, out_dtype=BF16, tm=1024, tn=2048, tk=KV_LORA, name="mla_kv_up")
    q_all = q_all * (MLA_SCALE * LOG2E)
    q_nope = q_all[:, :MLA_HEADS * MLA_NOPE].astype(BF16)
    q_pe = q_all[:, MLA_HEADS * MLA_NOPE:].reshape(N_LAT_ROWS, MLA_HEADS, MLA_ROPE)
    q_pe = _rope_lat(q_pe, rope).transpose(1, 0, 2).astype(BF16)
    q_pe = jnp.pad(q_pe, ((0, 0), (0, 0), (0, MLA_NOPE - MLA_ROPE)))
    k_pe = _rope_lat(k_pe.reshape(N_ROWS, 1, MLA_ROPE), rope).reshape(N_ROWS, MLA_ROPE).astype(BF16)
    k_pe = jnp.pad(k_pe, ((0, 0), (0, MLA_NOPE - MLA_ROPE)))
    m_out = _mla_attention(q_nope, q_pe, kv, k_pe)

    pd = p[:, MLA_IN:IN_ODD]
    qk_w = DIFF_HEADS * 2 * DIFF_HEAD
    dq = pd[:N_LAT_ROWS, :qk_w].reshape(N_LAT_ROWS, DIFF_HEADS * 2, DIFF_HEAD)
    dk = pd[:, qk_w:2 * qk_w].reshape(N_ROWS, DIFF_HEADS * 2, DIFF_HEAD)
    dv = pd[:, 2 * qk_w:].astype(BF16)
    dq = (_rope_lat(dq, rope) * (DIFF_SCALE * LOG2E)).reshape(N_LAT_ROWS, qk_w).astype(BF16)
    dk = _rope_lat(dk, rope).reshape(N_ROWS, qk_w).astype(BF16)
    d_out = _diff_attention(lam.reshape(1), dq, dk, dv, subln, 1.0 - lam_init)
    return jnp.concatenate([m_out, d_out], axis=-1)


def kernel(x, c, ctx, c_ctx, ada_w, ada_b, norm1_g, norm2_g, mlp_w1, mlp_w2, final_g, ev_w_in, ev_w_out, rw_mu, rw_w0, rw_w_up, rw_a0, rw_a_up, rw_g_up, rw_k_k, rw_k_a, rw_r_k, rw_ln_w, rw_ln_b, gq_q_norm, gq_k_norm, od_w_in, od_w_out, mla_q_norm, mla_q_up, mla_kv_norm, mla_kv_up, diff_lq1, diff_lk1, diff_lq2, diff_lk2, diff_subln):
    rope_gq = _axial_rope_tables(SEQ, GQ_HEAD)
    rope_64 = _axial_rope_tables(SEQ, MLA_ROPE)
    h = jnp.concatenate([x.reshape(N_LAT_ROWS, D_MODEL), ctx.reshape(N_CTX_ROWS, D_MODEL)], axis=0)
    cond = jnp.concatenate([c_ctx[None], c, jnp.zeros((3, D_MODEL), F32)], axis=0)
    cond = jax.nn.silu(cond).astype(BF16)
    for i in range(DEPTH):
        last = i == DEPTH - 1
        mod = _matmul(cond, ada_w, layer=i, out_dtype=F32, tm=8, tn=2048, tk=1024, name="adaln_mod")
        mod = (mod + ada_b[i])[:BATCH + 1].reshape(BATCH + 1, N_MOD, 1, D_MODEL)
        sh1, sc1, g1, sh2, sc2, g2 = (mod[:, m] for m in range(N_MOD))
        a_all = _norm(h, norm1_g[i], sc1, sh1, rows=N_ROWS, out_dtype=BF16)
        j = i // 2
        if i % 2 == 0:
            rw_args = (rw_mu[j], rw_w0[j], rw_w_up[j], rw_a0[j], rw_a_up[j], rw_g_up[j], rw_k_k[j], rw_k_a[j],
                       rw_r_k[j], rw_ln_w[j], rw_ln_b[j])
            mix = _even_mixer(a_all, rope_gq, ev_w_in[j], rw_args, gq_q_norm[j], gq_k_norm[j], not last)
            w_out = ev_w_out[j]
        else:
            if not last:
                raise NotImplementedError("context rows of an odd layer are only needed when a layer follows")
            mix = _odd_mixer(a_all, rope_64, od_w_in[j], mla_q_norm[j], mla_q_up[j], mla_kv_norm[j],
                             mla_kv_up[j], diff_lq1[j], diff_lk1[j], diff_lq2[j], diff_lk2[j], diff_subln[j], i)
            w_out = od_w_out[j]
        rows = N_LAT_ROWS if last else N_ROWS
        h = _matmul(mix, w_out.astype(BF16), out_dtype=F32, tm=1024, tn=1024, tk=2048, rows=rows,
                    epilogue="gated_residual", res=h, gate=g1, name="mixer_w_out")
        a2 = _norm(h, norm2_g[i], sc2, sh2, rows=rows, out_dtype=BF16)
        hid = _matmul(a2, mlp_w1, layer=i, out_dtype=BF16, tm=1024, tn=512, tk=D_MODEL,
                      epilogue="relu2", name="mlp_w1")
        h = _matmul(hid, mlp_w2, layer=i, out_dtype=F32, tm=1024, tn=1024, tk=2048,
                    epilogue="gated_residual", res=h, gate=g2, name="mlp_w2")
    out = _norm(h, final_g, rows=N_LAT_ROWS, out_dtype=F32)
    return out.reshape(BATCH, SEQ, D_MODEL)
```

```python
import functools
import math

import jax
import jax.numpy as jnp
from jax import lax
from jax.experimental import pallas as pl
from jax.experimental.pallas import tpu as pltpu

D_MODEL = 4096
BATCH = 4
SEQ = 4096
DEPTH = 2
CTX_LEN = 256
GRID_W = 64
ROPE_THETA = 10000.0
NORM_EPS = 1e-6
MLP_HIDDEN = 4 * D_MODEL
N_MOD = 6
HALF = D_MODEL // 2

RW_HEAD = 64
RW_W = HALF
RW_HEADS = RW_W // RW_HEAD
DECAY_RANK = 96
AAA_RANK = 96
GATE_RANK = 256
GN_EPS = 64e-5
RW_IN = 3 * RW_W + GATE_RANK + 2 * DECAY_RANK + 2 * AAA_RANK

GQ_HEAD = 128
GQ_HEADS = HALF // GQ_HEAD
GQ_KV_HEADS = GQ_HEADS // 4
GQ_GROUP = GQ_HEADS // GQ_KV_HEADS
GQ_SCALE = GQ_HEAD ** -0.5
IN_EVEN = RW_IN + (GQ_HEADS + 2 * GQ_KV_HEADS) * GQ_HEAD

MLA_NOPE = 128
MLA_ROPE = 64
MLA_V = 128
MLA_HEADS = HALF // MLA_V
Q_LORA = 768
KV_LORA = 512
MLA_SCALE = (MLA_NOPE + MLA_ROPE) ** -0.5
MLA_IN = Q_LORA + KV_LORA + MLA_ROPE

DIFF_HEAD = 64
DIFF_V = 2 * DIFF_HEAD
DIFF_HEADS = HALF // DIFF_V
DIFF_SCALE = DIFF_HEAD ** -0.5
DIFF_IN = DIFF_HEADS * (4 * DIFF_HEAD + DIFF_V)
IN_ODD = MLA_IN + DIFF_IN
IN_ODD_PADDED = 7680

N_LAT_ROWS = BATCH * SEQ
N_CTX_ROWS = BATCH * CTX_LEN
N_ROWS = N_LAT_ROWS + N_CTX_ROWS
T_ALL = CTX_LEN + SEQ
CTX_BLOCK0 = N_LAT_ROWS // CTX_LEN

V7X_VMEM_LIMIT_BYTES = 56 * 1024 * 1024

BF16 = jnp.bfloat16
F32 = jnp.float32


def _params(*sem):
    return pltpu.CompilerParams(dimension_semantics=sem, vmem_limit_bytes=V7X_VMEM_LIMIT_BYTES)


def _row_group(row_tile, tile_rows):
    start = row_tile * tile_rows
    return jnp.where(start < N_LAT_ROWS, 1 + start // SEQ, 0)


def _norm_kernel(x_ref, g_ref, *rest, modulate):
    if modulate:
        sc_ref, sh_ref, o_ref = rest
    else:
        (o_ref,) = rest
    x = x_ref[...]
    ms = jnp.mean(x * x, axis=-1, keepdims=True)
    y = x * lax.rsqrt(ms + NORM_EPS) * g_ref[...]
    if modulate:
        y = y * (1.0 + sc_ref[0]) + sh_ref[0]
    o_ref[...] = y.astype(o_ref.dtype)


def _norm(x, g, sc=None, sh=None, *, rows, out_dtype, tile=256):
    d = x.shape[1]
    modulate = sc is not None
    in_specs = [pl.BlockSpec((tile, d), lambda i: (i, 0)),
                pl.BlockSpec((1, d), lambda i: (0, 0))]
    args = [x, g.reshape(1, d)]
    if modulate:
        mod_spec = pl.BlockSpec((1, 1, d), lambda i: (_row_group(i, tile), 0, 0))
        in_specs += [mod_spec, mod_spec]
        args += [sc, sh]
    return pl.pallas_call(
        functools.partial(_norm_kernel, modulate=modulate),
        grid=(rows // tile,),
        in_specs=in_specs,
        out_specs=pl.BlockSpec((tile, d), lambda i: (i, 0)),
        out_shape=jax.ShapeDtypeStruct((rows, d), out_dtype),
        compiler_params=_params("parallel"),
        name="rmsnorm_mod" if modulate else "rmsnorm",
    )(*args)


def _mm_kernel(a_ref, w_ref, *rest, epilogue, nk):
    if epilogue == "gated_residual":
        res_ref, gate_ref, o_ref, acc_ref = rest
    else:
        o_ref, acc_ref = rest
    k = pl.program_id(2)
    part = jnp.dot(a_ref[...].astype(BF16), w_ref[...].astype(BF16), preferred_element_type=F32)

    @pl.when(k == 0)
    def _():
        acc_ref[...] = part

    @pl.when(k > 0)
    def _():
        acc_ref[...] += part

    @pl.when(k == nk - 1)
    def _():
        acc = acc_ref[...]
        if epilogue == "relu2":
            acc = jnp.square(jnp.maximum(acc, 0.0))
        elif epilogue == "gated_residual":
            acc = res_ref[...] + gate_ref[0] * acc
        o_ref[...] = acc.astype(o_ref.dtype)


def _matmul(a, w, *, out_dtype, tm, tn, tk, rows=None, layer=None, epilogue=None, res=None, gate=None, name):
    kdim, n = w.shape[-2:]
    m = a.shape[0] if rows is None else rows
    assert a.shape[1] == kdim and m % tm == 0 and n % tn == 0 and kdim % tk == 0
    assert (w.ndim == 3) == (layer is not None)
    nk = kdim // tk
    w_spec = (pl.BlockSpec((tk, tn), lambda i, j, k: (k, j)) if layer is None else
              pl.BlockSpec((None, tk, tn), lambda i, j, k: (layer, k, j)))
    in_specs = [pl.BlockSpec((tm, tk), lambda i, j, k: (i, k)), w_spec]
    args = [a, w]
    if epilogue == "gated_residual":
        in_specs += [pl.BlockSpec((tm, tn), lambda i, j, k: (i, j)),
                     pl.BlockSpec((1, 1, tn), lambda i, j, k: (_row_group(i, tm), 0, j))]
        args += [res, gate]
    return pl.pallas_call(
        functools.partial(_mm_kernel, epilogue=epilogue, nk=nk),
        grid=(m // tm, n // tn, nk),
        in_specs=in_specs,
        out_specs=pl.BlockSpec((tm, tn), lambda i, j, k: (i, j)),
        out_shape=jax.ShapeDtypeStruct((m, n), out_dtype),
        scratch_shapes=[pltpu.VMEM((tm, tn), F32)],
        compiler_params=_params("parallel", "parallel", "arbitrary"),
        name=name,
    )(*args)


def _nt_dot(a, b):
    return lax.dot_general(a, b, (((1,), (1,)), ((), ())), preferred_element_type=F32)


LOG2E = math.log2(math.e)


def _fill_values_and_ones(v_ref, vo_ref):
    d = v_ref.shape[1]
    vo_ref[:, :d] = v_ref[...]
    vo_ref[:, d:] = jnp.ones_like(v_ref)


def _softmax_times_values(s_list, vo_refs):
    d = vo_refs[0].shape[1] // 2
    m = functools.reduce(jnp.maximum, [jnp.max(s, axis=-1, keepdims=True) for s in s_list])
    acc = functools.reduce(jnp.add, [jnp.dot(jnp.exp2(s - m).astype(BF16), vo_ref[...], preferred_element_type=F32)
                                     for s, vo_ref in zip(s_list, vo_refs)])
    return acc[:, :d] / acc[:, d:]


def _attend_each(items):
    outs = []
    pending = None
    for q, k_refs, vo_refs in items:
        s_list = [_nt_dot(q, k_ref[...]) for k_ref in k_refs]
        if pending is not None:
            outs.append(_softmax_times_values(*pending))
        pending = (s_list, vo_refs)
    outs.append(_softmax_times_values(*pending))
    return outs


def _gqa_kernel(q_ref, *rest, with_lat):
    if with_lat:
        kc_ref, vc_ref, kl_ref, vl_ref, o_ref, voc_ref, vol_ref = rest
        k_refs, v_refs, vo_refs = (kc_ref, kl_ref), (vc_ref, vl_ref), (voc_ref, vol_ref)
    else:
        kc_ref, vc_ref, o_ref, voc_ref = rest
        k_refs, v_refs, vo_refs = (kc_ref,), (vc_ref,), (voc_ref,)

    @pl.when(pl.program_id(2) == 0)
    def _():
        for v_ref, vo_ref in zip(v_refs, vo_refs):
            _fill_values_and_ones(v_ref, vo_ref)

    outs = _attend_each([(q_ref[:, g * GQ_HEAD:(g + 1) * GQ_HEAD], k_refs, vo_refs) for g in range(GQ_GROUP)])
    for g, o in enumerate(outs):
        o_ref[:, g * GQ_HEAD:(g + 1) * GQ_HEAD] = o.astype(o_ref.dtype)


def _gqa_attention(qkv, *, latent, tq=256):
    gw = GQ_GROUP * GQ_HEAD
    k0, v0 = GQ_HEADS, GQ_HEADS + GQ_KV_HEADS
    ctx_k = pl.BlockSpec((CTX_LEN, GQ_HEAD), lambda b, n, i: (CTX_BLOCK0 + b, k0 + n))
    ctx_v = pl.BlockSpec((CTX_LEN, GQ_HEAD), lambda b, n, i: (CTX_BLOCK0 + b, v0 + n))
    scratch = [pltpu.VMEM((CTX_LEN, 2 * GQ_HEAD), BF16)]
    if latent:
        per_b = SEQ // tq
        lat_k = pl.BlockSpec((SEQ, GQ_HEAD), lambda b, n, i: (b, k0 + n))
        lat_v = pl.BlockSpec((SEQ, GQ_HEAD), lambda b, n, i: (b, v0 + n))
        in_specs = [pl.BlockSpec((tq, gw), lambda b, n, i: (b * per_b + i, n)), ctx_k, ctx_v, lat_k, lat_v]
        args = (qkv,) * 5
        rows = N_LAT_ROWS
        scratch.append(pltpu.VMEM((SEQ, 2 * GQ_HEAD), BF16))
    else:
        per_b = CTX_LEN // tq
        q_blk0 = N_LAT_ROWS // tq
        in_specs = [pl.BlockSpec((tq, gw), lambda b, n, i: (q_blk0 + b * per_b + i, n)), ctx_k, ctx_v]
        args = (qkv,) * 3
        rows = N_CTX_ROWS
    return pl.pallas_call(
        functools.partial(_gqa_kernel, with_lat=latent),
        grid=(BATCH, GQ_KV_HEADS, per_b),
        in_specs=in_specs,
        out_specs=pl.BlockSpec((tq, gw), lambda b, n, i: (b * per_b + i, n)),
        out_shape=jax.ShapeDtypeStruct((rows, GQ_HEADS * GQ_HEAD), BF16),
        scratch_shapes=scratch,
        compiler_params=_params("parallel", "parallel", "arbitrary"),
        name="gqa_attention_lat" if latent else "gqa_attention_ctx",
    )(*args)


def _mla_kernel(qn_ref, qp_ref, knc_ref, kpc_ref, vc_ref, knl_ref, kpl_ref, vl_ref, o_ref,
                kc_ref, kl_ref, voc_ref, vol_ref):
    heads = range(MLA_HEADS_PER_STEP)
    cols = [slice(j * MLA_NOPE, (j + 1) * MLA_NOPE) for j in heads]

    @pl.when(pl.program_id(2) == 0)
    def _():
        for kn_ref, kp_ref, v_ref, k_ref, vo_ref in ((knc_ref, kpc_ref, vc_ref, kc_ref, voc_ref),
                                                     (knl_ref, kpl_ref, vl_ref, kl_ref, vol_ref)):
            for j in heads:
                k_ref[j, :, :MLA_NOPE] = kn_ref[:, cols[j]]
                k_ref[j, :, MLA_NOPE:] = kp_ref[...]
                vo_ref[j, :, :MLA_V] = v_ref[:, cols[j]]
                vo_ref[j, :, MLA_V:] = jnp.ones_like(kp_ref)

    items = [(jnp.concatenate([qn_ref[:, cols[j]], qp_ref[j]], axis=1), (kc_ref.at[j], kl_ref.at[j]),
              (voc_ref.at[j], vol_ref.at[j])) for j in heads]
    for j, o in enumerate(_attend_each(items)):
        o_ref[:, cols[j]] = o.astype(o_ref.dtype)


MLA_HEADS_PER_STEP = 2


def _mla_attention(q_nope, q_pe, kv, k_pe, *, tq=256):
    per_b = SEQ // tq
    hs = MLA_HEADS_PER_STEP
    w = hs * MLA_NOPE
    v_blk0 = MLA_HEADS // hs
    in_specs = [
        pl.BlockSpec((tq, w), lambda b, h, i: (b * per_b + i, h)),
        pl.BlockSpec((hs, tq, MLA_NOPE), lambda b, h, i: (h, b * per_b + i, 0)),
        pl.BlockSpec((CTX_LEN, w), lambda b, h, i: (CTX_BLOCK0 + b, h)),
        pl.BlockSpec((CTX_LEN, MLA_NOPE), lambda b, h, i: (CTX_BLOCK0 + b, 0)),
        pl.BlockSpec((CTX_LEN, w), lambda b, h, i: (CTX_BLOCK0 + b, v_blk0 + h)),
        pl.BlockSpec((SEQ, w), lambda b, h, i: (b, h)),
        pl.BlockSpec((SEQ, MLA_NOPE), lambda b, h, i: (b, 0)),
        pl.BlockSpec((SEQ, w), lambda b, h, i: (b, v_blk0 + h)),
    ]
    return pl.pallas_call(
        _mla_kernel,
        grid=(BATCH, MLA_HEADS // hs, per_b),
        in_specs=in_specs,
        out_specs=pl.BlockSpec((tq, w), lambda b, h, i: (b * per_b + i, h)),
        out_shape=jax.ShapeDtypeStruct((N_LAT_ROWS, MLA_HEADS * MLA_V), BF16),
        scratch_shapes=[pltpu.VMEM((hs, CTX_LEN, 2 * MLA_NOPE), BF16), pltpu.VMEM((hs, SEQ, 2 * MLA_NOPE), BF16),
                        pltpu.VMEM((hs, CTX_LEN, 2 * MLA_V), BF16), pltpu.VMEM((hs, SEQ, 2 * MLA_V), BF16)],
        compiler_params=_params("parallel", "parallel", "arbitrary"),
        name="mla_attention",
    )(q_nope, q_pe, kv, k_pe, kv, kv, k_pe, kv)


def _diff_kernel(lam_ref, q_ref, kc_ref, vc_ref, kl_ref, vl_ref, g_ref, o_ref, voc_ref, vol_ref, *, out_scale):
    heads = range(DIFF_HEADS_PER_STEP)
    cols = [slice(j * DIFF_V, (j + 1) * DIFF_V) for j in heads]

    @pl.when(pl.program_id(2) == 0)
    def _():
        for v_ref, vo_ref in ((vc_ref, voc_ref), (vl_ref, vol_ref)):
            for j in heads:
                vo_ref[j, :, :DIFF_V] = v_ref[:, cols[j]]
                vo_ref[j, :, DIFF_V:] = jnp.ones((v_ref.shape[0], DIFF_V), BF16)

    first = lax.broadcasted_iota(jnp.int32, (q_ref.shape[0], DIFF_V), 1) < DIFF_HEAD
    items = []
    for j in heads:
        q = q_ref[:, cols[j]]
        zero = jnp.zeros_like(q)
        keys = (kc_ref[:, cols[j]], kl_ref[:, cols[j]])
        for qh in (jnp.where(first, q, zero), jnp.where(first, zero, q)):
            items.append((qh, keys, (voc_ref.at[j], vol_ref.at[j])))
    outs = _attend_each(items)
    for j in heads:
        o = outs[2 * j] - lam_ref[0] * outs[2 * j + 1]
        ms = jnp.mean(o * o, axis=-1, keepdims=True)
        o = o * lax.rsqrt(ms + NORM_EPS) * g_ref[...]
        o_ref[:, cols[j]] = (o * out_scale).astype(o_ref.dtype)


DIFF_HEADS_PER_STEP = 2


def _diff_attention(lam, qkv, subln, out_scale, *, tq=256):
    per_b = SEQ // tq
    hs = DIFF_HEADS_PER_STEP
    w = hs * DIFF_V
    k0, v0 = DIFF_HEADS // hs, 2 * DIFF_HEADS // hs
    ctx_k = pl.BlockSpec((CTX_LEN, w), lambda b, h, i: (CTX_BLOCK0 + b, k0 + h))
    ctx_v = pl.BlockSpec((CTX_LEN, w), lambda b, h, i: (CTX_BLOCK0 + b, v0 + h))
    lat_k = pl.BlockSpec((SEQ, w), lambda b, h, i: (b, k0 + h))
    lat_v = pl.BlockSpec((SEQ, w), lambda b, h, i: (b, v0 + h))
    in_specs = [
        pl.BlockSpec(memory_space=pltpu.SMEM),
        pl.BlockSpec((tq, w), lambda b, h, i: (b * per_b + i, h)),
        ctx_k, ctx_v, lat_k, lat_v,
        pl.BlockSpec((1, DIFF_V), lambda b, h, i: (0, 0)),
    ]
    return pl.pallas_call(
        functools.partial(_diff_kernel, out_scale=out_scale),
        grid=(BATCH, DIFF_HEADS // hs, per_b),
        in_specs=in_specs,
        out_specs=pl.BlockSpec((tq, w), lambda b, h, i: (b * per_b + i, h)),
        out_shape=jax.ShapeDtypeStruct((N_LAT_ROWS, DIFF_HEADS * DIFF_V), BF16),
        scratch_shapes=[pltpu.VMEM((hs, CTX_LEN, 2 * DIFF_V), BF16), pltpu.VMEM((hs, SEQ, 2 * DIFF_V), BF16)],
        compiler_params=_params("parallel", "parallel", "arbitrary"),
        name="diff_attention",
    )(lam, qkv, qkv, qkv, qkv, qkv, subln.reshape(1, DIFF_V))


RW_CHUNK = 128
RW_PAIRS = RW_HEADS // 2
LANES = 2 * RW_HEAD
SUB = 8
RW_UNROLL = 8
RW_GROUP = 8


def _rwkv_kernel(r_in, kk_in, v_in, w_in, k_in, b_in, y_out, s_ref, r_ref, kk_ref, v_ref, w_ref, k_ref, b_ref,
                 y_ref):
    d = pl.program_id(0)
    c = pl.program_id(2)

    @pl.when(c == 0)
    def _():
        s_ref[...] = jnp.zeros_like(s_ref)

    for p in range(RW_PAIRS):
        cols = slice(p * LANES, (p + 1) * LANES)
        for src, dst in ((r_in, r_ref), (kk_in, kk_ref), (v_in, v_ref)):
            dst[p] = src[:, cols]
        for src, dst in ((w_in, w_ref), (k_in, k_ref), (b_in, b_ref)):
            dst[0, p] = src[0, :, cols]

    v_hi = lax.broadcasted_iota(jnp.int32, (SUB, SUB, LANES), 0)
    v_lo = lax.broadcasted_iota(jnp.int32, (SUB, SUB, LANES), 1)
    lane3 = lax.broadcasted_iota(jnp.int32, (SUB, SUB, LANES), 2)
    dup = jnp.where((lane3 & (RW_HEAD - 1)) == v_hi * SUB + v_lo, 1.0, 0.0)
    blk_r = lax.broadcasted_iota(jnp.int32, (LANES, LANES), 0) // RW_HEAD
    blk_c = lax.broadcasted_iota(jnp.int32, (LANES, LANES), 1) // RW_HEAD
    head_ones = jnp.where(blk_r == blk_c, 1.0, 0.0).astype(BF16)
    slot_of_lane = lax.broadcasted_iota(jnp.int32, (SUB, LANES), 1) & (RW_HEAD - 1)
    step_dir = jnp.where(d == 0, 1, -1)

    y_ref[...] = jnp.zeros_like(y_ref)

    def row(ref, p, t, *lead):
        return ref[(*lead, p, pl.ds(t, SUB, stride=0), slice(None))]

    def as_rows(tile):
        return tile.reshape(RW_HEAD, LANES).astype(BF16)

    def head_sums(states, pairs, t_done, t_next):
        rows = []
        for p in pairs:
            rows.append(as_rows(states[p] * row(r_ref, p, t_done)))
            if t_next is not None:
                rows.append(as_rows(states[p] * row(kk_ref, p, t_next)))
                rows.append(as_rows(dup * row(v_ref, p, t_next)))
        return jnp.dot(jnp.concatenate(rows, axis=0), head_ones, preferred_element_type=F32)

    def slab(sums, q, n_slabs, i):
        lo = (q * n_slabs + i) * RW_HEAD
        return sums[lo:lo + RW_HEAD].reshape(SUB, SUB, LANES)

    def put_y(p, y_b, t, valid):
        window = t // RW_HEAD
        keep = (slot_of_lane == t % RW_HEAD) & valid
        y_ref[0, p, window] = jnp.where(keep, y_b, y_ref[0, p, window])

    groups = [range(g0, g0 + RW_GROUP) for g0 in range(0, RW_PAIRS, RW_GROUP)]

    def block(g, carry):
        first = g * RW_UNROLL
        first = jnp.where(d == 0, first, RW_CHUNK - 1 - first)
        times = [first + u * step_dir for u in range(RW_UNROLL)]
        t_before = jnp.clip(first - step_dir, 0, RW_CHUNK - 1)
        states = [s_ref[p] for p in range(RW_PAIRS)]
        sums = [head_sums(states, grp, t_before, times[0]) for grp in groups]
        for gi, grp in enumerate(groups):
            for q, p in enumerate(grp):
                put_y(p, slab(sums[gi], q, 3, 0), t_before, g > 0)
        for u in range(RW_UNROLL):
            t = times[u]
            for gi, grp in enumerate(groups):
                for q, p in enumerate(grp):
                    sa = slab(sums[gi], q, 3, 1)
                    vb = slab(sums[gi], q, 3, 2)
                    states[p] = (states[p] * row(w_ref, p, t, 0) - sa * row(b_ref, p, t, 0)
                                 + vb * row(k_ref, p, t, 0))
                if u + 1 < RW_UNROLL:
                    sums[gi] = head_sums(states, grp, t, times[u + 1])
                    for q, p in enumerate(grp):
                        put_y(p, slab(sums[gi], q, 3, 0), t, True)
        for p in range(RW_PAIRS):
            s_ref[p] = states[p]
        return carry

    lax.fori_loop(0, RW_CHUNK // RW_UNROLL, block, 0)

    t_last = jnp.where(d == 0, RW_CHUNK - 1, 0)
    states = [s_ref[p] for p in range(RW_PAIRS)]
    for grp in groups:
        last = head_sums(states, grp, t_last, None)
        for q, p in enumerate(grp):
            put_y(p, slab(last, q, 1, 0), t_last, True)

    first_head = lax.broadcasted_iota(jnp.int32, (RW_HEAD, LANES), 1) < RW_HEAD
    for p in range(0, RW_PAIRS, 2):
        for window in range(RW_CHUNK // RW_HEAD):
            both = jnp.concatenate([y_ref[0, p, window].reshape(RW_HEAD, LANES),
                                    y_ref[0, p + 1, window].reshape(RW_HEAD, LANES)], axis=0)
            by_row = both.T
            head0, head1 = by_row[:RW_HEAD], by_row[RW_HEAD:]
            rows = slice(window * RW_HEAD, (window + 1) * RW_HEAD)
            y_out[0, rows, p * LANES:(p + 1) * LANES] = jnp.where(
                first_head, head0, pltpu.roll(head1, RW_HEAD, axis=1))
            y_out[0, rows, (p + 1) * LANES:(p + 2) * LANES] = jnp.where(
                first_head, pltpu.roll(head0, RW_HEAD, axis=1), head1)


def _rwkv_scan(r, kk, v, w, k, b):
    n_ctx = CTX_LEN // RW_CHUNK
    n_lat = SEQ // RW_CHUNK
    ctx0 = N_LAT_ROWS // RW_CHUNK

    def chunk(d, bb, c):
        j = jnp.where(d == 0, c, jnp.where(c < n_ctx, n_ctx - 1 - c, n_ctx + n_lat - 1 - (c - n_ctx)))
        return jnp.where(j < n_ctx, ctx0 + bb * n_ctx + j, bb * n_lat + j - n_ctx)

    shared = pl.BlockSpec((RW_CHUNK, RW_W), lambda d, bb, c: (chunk(d, bb, c), 0))
    per_dir = pl.BlockSpec((1, RW_CHUNK, RW_W), lambda d, bb, c: (d, chunk(d, bb, c), 0))
    staged = pltpu.VMEM((RW_PAIRS, RW_CHUNK, LANES), F32)
    staged_dir = pltpu.VMEM((1, RW_PAIRS, RW_CHUNK, LANES), F32)
    return pl.pallas_call(
        _rwkv_kernel,
        grid=(2, BATCH, n_ctx + n_lat),
        in_specs=[shared, shared, shared, per_dir, per_dir, per_dir],
        out_specs=per_dir,
        out_shape=jax.ShapeDtypeStruct((2, N_ROWS, RW_W), F32),
        scratch_shapes=[pltpu.VMEM((RW_PAIRS, RW_HEAD // SUB, SUB, LANES), F32),
                        staged, staged, staged, staged_dir, staged_dir, staged_dir,
                        pltpu.VMEM((1, RW_PAIRS, RW_CHUNK // RW_HEAD, SUB, SUB, LANES), F32)],
        compiler_params=_params("parallel", "parallel", "arbitrary"),
        name="rwkv7_scan",
    )(r, kk, v, w, k, b)


def _axial_rope_tables(n_tokens, rot_dim):
    n_rows = n_tokens // GRID_W
    row = jnp.repeat(jnp.arange(n_rows, dtype=F32), GRID_W)
    col = jnp.tile(jnp.arange(GRID_W, dtype=F32), n_rows)
    axis_dim = rot_dim // 2
    inv_freq = ROPE_THETA ** (-jnp.arange(0, axis_dim, 2, dtype=F32) / axis_dim)
    ang_r = row[:, None] * inv_freq
    ang_c = col[:, None] * inv_freq
    ang = jnp.concatenate([ang_r, ang_r, ang_c, ang_c], axis=-1)
    return jnp.cos(ang), jnp.sin(ang)


def _rotate_half(z):
    z1, z2 = jnp.split(z, 2, axis=-1)
    return jnp.concatenate([-z2, z1], axis=-1)


def _apply_rope(x, cos, sin):
    half = x.shape[-1] // 2
    rot = jnp.concatenate([_rotate_half(x[..., :half]), _rotate_half(x[..., half:])], axis=-1)
    return x * cos + rot * sin


def _rms_f32(x, g):
    return x * lax.rsqrt(jnp.mean(x * x, axis=-1, keepdims=True) + NORM_EPS) * g


def _rope_lat(x, rope):
    cos, sin = rope
    heads, dim = x.shape[1:]
    lat = x[:N_LAT_ROWS].reshape(BATCH, SEQ, heads, dim)
    lat = _apply_rope(lat, cos[None, :, None, :], sin[None, :, None, :]).reshape(N_LAT_ROWS, heads, dim)
    return lat if x.shape[0] == N_LAT_ROWS else jnp.concatenate([lat, x[N_LAT_ROWS:]], axis=0)


RW_SLAB = 512
RW_TAIL = GATE_RANK + 2 * DECAY_RANK + 2 * AAA_RANK
RW_TAIL_PAD = 768
RW_ROW_TILE = 256


def _head_lane_sums(x):
    rows = x.shape[0]
    blk_r = lax.broadcasted_iota(jnp.int32, (LANES, LANES), 0) // RW_HEAD
    blk_c = lax.broadcasted_iota(jnp.int32, (LANES, LANES), 1) // RW_HEAD
    head_ones = jnp.where(blk_r == blk_c, 1.0, 0.0).astype(BF16)
    hi = x.astype(BF16)
    rest = x - hi.astype(F32)
    mid = rest.astype(BF16)
    lo = (rest - mid.astype(F32)).astype(BF16)
    pieces = jnp.concatenate([hi, mid, lo], axis=0)
    out = []
    for s in range(x.shape[1] // LANES):
        part = jnp.dot(pieces[:, s * LANES:(s + 1) * LANES], head_ones, preferred_element_type=F32)
        out.append(part[:rows] + part[rows:2 * rows] + part[2 * rows:])
    return jnp.concatenate(out, axis=1)


def _rwkv_prep_kernel(*refs):
    slabs = [refs[4 * n:4 * n + 4] for n in range(4)]
    (up_w_ref, w0_ref, a0_ref, kk_gain_ref, ka_ref, rk_ref,
     r_out, v_out, kk_out, decay_out, kdir_out, b_out, g_out, bonus_out) = refs[16:30]
    stage_refs = refs[30:34]
    tile = RW_ROW_TILE
    row0 = pl.program_id(0) * tile
    seq_len = jnp.where(row0 < N_LAT_ROWS, SEQ, CTX_LEN)
    starts_seq = row0 % seq_len == 0
    ends_seq = (row0 + tile) % seq_len == 0

    def shifted(main_ref, before_ref, after_ref, mu_ref, stage_ref):
        x = main_ref[...]
        stage_ref[SUB:SUB + tile, :] = x
        stage_ref[SUB - 1:SUB, :] = jnp.where(starts_seq, 0.0, before_ref[SUB - 1:SUB, :])
        stage_ref[SUB + tile:SUB + tile + 1, :] = jnp.where(ends_seq, 0.0, after_ref[0:1, :])
        around = 0.5 * (stage_ref[SUB - 1:SUB - 1 + tile, :] + stage_ref[SUB + 1:SUB + 1 + tile, :])
        return x + (around - x) * mu_ref[...]

    r, k, v, tail = (shifted(*slab, stage) for slab, stage in zip(slabs, stage_refs))
    r_out[...] = r
    v_out[...] = v
    s_w = RW_SLAB
    lane = lax.broadcasted_iota(jnp.int32, (1, RW_TAIL_PAD), 1)
    act = jnp.where(lane < GATE_RANK, jax.nn.sigmoid(tail),
                    jnp.where(lane < GATE_RANK + 2 * DECAY_RANK, jnp.tanh(tail), tail))
    up = jnp.dot(act.astype(BF16), up_w_ref[...], preferred_element_type=F32)
    g_out[...] = up[:, :s_w]
    kq = k * kk_gain_ref[...]
    kk = kq / jnp.maximum(jnp.sqrt(_head_lane_sums(kq * kq)), 1e-12)
    kk_out[...] = kk
    kdir_sum = jnp.zeros_like(k)
    for d in range(2):
        z = -(w0_ref[d] + up[:, (1 + d) * s_w:(2 + d) * s_w])
        softplus = jnp.maximum(z, 0.0) + jnp.log(1.0 + jnp.exp(-jnp.abs(z)))
        decay_out[d] = jnp.exp(-jnp.exp(-softplus - 0.5))
        a = jax.nn.sigmoid(a0_ref[d] + up[:, (3 + d) * s_w:(4 + d) * s_w])
        kd = k * (1.0 + (a - 1.0) * ka_ref[...])
        kdir_out[d] = kd
        b_out[d] = kk * a
        kdir_sum = kdir_sum + kd
    bonus_out[...] = _head_lane_sums(r * kdir_sum * rk_ref[...]) * v


def _rwkv_post_kernel(y_ref, g_ref, bonus_ref, lnw_ref, lnb_ref, o_ref):
    y = y_ref[0] + y_ref[1]
    mean = _head_lane_sums(y) * (1.0 / RW_HEAD)
    cen = y - mean
    var = _head_lane_sums(cen * cen) * (1.0 / RW_HEAD)
    yn = cen * lax.rsqrt(var + GN_EPS) * lnw_ref[...] + lnb_ref[...]
    o_ref[...] = ((yn + bonus_ref[...]) * g_ref[...]).astype(o_ref.dtype)


def _rwkv_mixer(p, tail_col, mu, w0, w_up, a0, a_up, g_up, k_k, k_a, r_k, ln_w, ln_b):
    assert tail_col % RW_TAIL_PAD == 0 and p.shape[1] >= tail_col + RW_TAIL_PAD
    n_slab = RW_W // RW_SLAB
    up_w = jnp.zeros((RW_TAIL_PAD, 5, RW_W), F32)
    o = GATE_RANK
    up_w = up_w.at[:o, 0].set(g_up)
    for d in range(2):
        up_w = up_w.at[o + d * DECAY_RANK:o + (d + 1) * DECAY_RANK, 1 + d].set(w_up[d])
    o += 2 * DECAY_RANK
    for d in range(2):
        up_w = up_w.at[o + d * AAA_RANK:o + (d + 1) * AAA_RANK, 3 + d].set(a_up[d])
    up_w = up_w.reshape(RW_TAIL_PAD, 5, n_slab, RW_SLAB).transpose(2, 0, 1, 3)
    up_w = up_w.reshape(n_slab, RW_TAIL_PAD, 5 * RW_SLAB).astype(BF16)
    mu_main = mu[:3 * RW_W].reshape(1, 3 * RW_W)
    mu_tail = jnp.pad(mu[3 * RW_W:], (0, RW_TAIL_PAD - RW_TAIL)).reshape(1, RW_TAIL_PAD)

    tile = RW_ROW_TILE
    col_blocks = RW_W // RW_SLAB
    halo = tile // SUB
    last_halo = N_ROWS // SUB - 1

    def with_halo(width, col):
        return [pl.BlockSpec((tile, width), lambda i, j: (i, col(j))),
                pl.BlockSpec((SUB, width), lambda i, j: (jnp.maximum(i * halo - 1, 0), col(j))),
                pl.BlockSpec((SUB, width), lambda i, j: (jnp.minimum((i + 1) * halo, last_halo), col(j)))]

    in_specs, args = [], []
    for n in range(3):
        in_specs += with_halo(RW_SLAB, lambda j, n=n: n * col_blocks + j)
        in_specs.append(pl.BlockSpec((1, RW_SLAB), lambda i, j, n=n: (0, n * col_blocks + j)))
        args += [p, p, p, mu_main]
    tail_blk = tail_col // RW_TAIL_PAD
    in_specs += with_halo(RW_TAIL_PAD, lambda j: tail_blk)
    in_specs.append(pl.BlockSpec((1, RW_TAIL_PAD), lambda i, j: (0, 0)))
    args += [p, p, p, mu_tail]

    vec = pl.BlockSpec((1, RW_SLAB), lambda i, j: (0, j))
    vec2 = pl.BlockSpec((2, 1, RW_SLAB), lambda i, j: (0, 0, j))
    out1 = pl.BlockSpec((tile, RW_SLAB), lambda i, j: (i, j))
    out2 = pl.BlockSpec((2, tile, RW_SLAB), lambda i, j: (0, i, j))
    one = jax.ShapeDtypeStruct((N_ROWS, RW_W), F32)
    two = jax.ShapeDtypeStruct((2, N_ROWS, RW_W), F32)
    in_specs += [pl.BlockSpec((None, RW_TAIL_PAD, 5 * RW_SLAB), lambda i, j: (j, 0, 0)), vec2, vec2, vec, vec, vec]
    args += [up_w, w0.reshape(2, 1, RW_W), a0.reshape(2, 1, RW_W), k_k.reshape(1, RW_W), k_a.reshape(1, RW_W),
             r_k.reshape(1, RW_W)]
    stage = lambda width: pltpu.VMEM((tile + 2 * SUB, width), F32)
    r, v, kk, decay, k_dir, b_dir, g, bonus = pl.pallas_call(
        _rwkv_prep_kernel,
        grid=(N_ROWS // tile, n_slab),
        in_specs=in_specs,
        out_specs=[out1, out1, out1, out2, out2, out2, out1, out1],
        out_shape=[one, one, one, two, two, two, one, one],
        scratch_shapes=[stage(RW_SLAB), stage(RW_SLAB), stage(RW_SLAB), stage(RW_TAIL_PAD)],
        compiler_params=_params("parallel", "parallel"),
        name="rwkv7_prep",
    )(*args)

    y2 = _rwkv_scan(r, kk, v, decay, k_dir, b_dir)

    return pl.pallas_call(
        _rwkv_post_kernel,
        grid=(N_ROWS // tile, n_slab),
        in_specs=[out2, out1, out1, vec, vec],
        out_specs=out1,
        out_shape=jax.ShapeDtypeStruct((N_ROWS, RW_W), BF16),
        compiler_params=_params("parallel", "parallel"),
        name="rwkv7_post",
    )(y2, g, bonus, ln_w.reshape(1, RW_W), ln_b.reshape(1, RW_W))


def _qkv_prep_kernel(x_ref, cos_ref, sin_ref, *rest, segments, quarter):
    gain_ref, o_ref = rest if len(rest) == 2 else (None, rest[0])
    _qkv_prep_body(x_ref, cos_ref, sin_ref, gain_ref, o_ref, segments, quarter)


def _qkv_prep_body(x_ref, cos_ref, sin_ref, gain_ref, o_ref, segments, quarter):
    tile = x_ref.shape[0]
    latent = pl.program_id(0) * tile < N_LAT_ROWS
    cos = jnp.where(latent, cos_ref[...], 1.0)
    sin = jnp.where(latent, sin_ref[...], 0.0)
    lane = lax.broadcasted_iota(jnp.int32, (1, LANES), 1)
    even_quarter = (lane // quarter) % 2 == 0
    for first, count, gain_row, rotary, scale in segments:
        for blk in range(first, first + count):
            cols = slice(blk * LANES, (blk + 1) * LANES)
            x = x_ref[:, cols]
            if gain_row is not None:
                x = x * lax.rsqrt(jnp.mean(x * x, axis=-1, keepdims=True) + NORM_EPS) * gain_ref[gain_row:gain_row + 1]
            if rotary:
                rot = jnp.where(even_quarter, -pltpu.roll(x, LANES - quarter, axis=1), pltpu.roll(x, quarter, axis=1))
                x = x * cos + rot * sin
            if scale != 1.0:
                x = x * scale
            o_ref[:, cols] = x.astype(o_ref.dtype)


def _qkv_prep(p, width, col_block, rope, gains, segments, quarter, name, tile=256):
    cos, sin = rope
    reps = LANES // cos.shape[1]
    cos, sin = jnp.tile(cos, (1, reps)), jnp.tile(sin, (1, reps))
    per_seq = SEQ // tile
    table = pl.BlockSpec((tile, LANES), lambda i: (i % per_seq, 0))
    return pl.pallas_call(
        functools.partial(_qkv_prep_kernel, segments=segments, quarter=quarter),
        grid=(N_ROWS // tile,),
        in_specs=[pl.BlockSpec((tile, width), lambda i: (i, col_block)), table, table]
        + ([] if gains is None else [pl.BlockSpec(gains.shape, lambda i: (0, 0))]),
        out_specs=pl.BlockSpec((tile, width), lambda i: (i, 0)),
        out_shape=jax.ShapeDtypeStruct((N_ROWS, width), BF16),
        compiler_params=_params("parallel"),
        name=name,
    )(p, cos, sin, *(() if gains is None else (gains,)))


def _even_mixer(a_all, rope, w_in, rw_args, q_g, k_g, need_ctx):
    rkv_w = 3 * RW_W
    att_w = (GQ_HEADS + 2 * GQ_KV_HEADS) * GQ_HEAD
    w_in = jnp.concatenate([w_in[:, :rkv_w], w_in[:, RW_IN:], w_in[:, rkv_w:RW_IN],
                            jnp.zeros((D_MODEL, RW_TAIL_PAD - RW_TAIL), w_in.dtype)], axis=1).astype(BF16)
    p = _matmul(a_all, w_in, out_dtype=F32, tm=1024, tn=768, tk=D_MODEL, name="even_w_in")
    rw = _rwkv_mixer(p, rkv_w + att_w, *rw_args)
    assert rkv_w % att_w == 0
    qkv = _qkv_prep(p, att_w, rkv_w // att_w, rope, jnp.stack([q_g, k_g]),
                    ((0, GQ_HEADS, 0, True, GQ_SCALE * LOG2E), (GQ_HEADS, GQ_KV_HEADS, 1, True, 1.0),
                     (GQ_HEADS + GQ_KV_HEADS, GQ_KV_HEADS, None, False, 1.0)), GQ_HEAD // 4, "gqa_prep")
    at = _gqa_attention(qkv, latent=True)
    if need_ctx:
        at = jnp.concatenate([at, _gqa_attention(qkv, latent=False)], axis=0)
        return jnp.concatenate([rw, at], axis=-1)
    return jnp.concatenate([rw[:N_LAT_ROWS], at], axis=-1)


def _odd_mixer(a_all, rope, w_in, q_norm, q_up, kv_norm, kv_up, lq1, lk1, lq2, lk2, subln, layer_idx):
    w_in = jnp.concatenate([w_in[:, MLA_IN:], w_in[:, :MLA_IN],
                            jnp.zeros((D_MODEL, IN_ODD_PADDED - IN_ODD), w_in.dtype)], axis=1).astype(BF16)
    p = _matmul(a_all, w_in, out_dtype=F32, tm=1024, tn=768, tk=D_MODEL, name="odd_w_in")
    lam_init = 0.8 - 0.6 * math.exp(-0.3 * layer_idx)
    lam = (jnp.exp(jnp.sum(lq1 * lk1).astype(F32)) - jnp.exp(jnp.sum(lq2 * lk2).astype(F32)) + lam_init)

    pm = p[:, DIFF_IN:DIFF_IN + MLA_IN]
    c_q = _rms_f32(pm[:N_LAT_ROWS, :Q_LORA], q_norm).astype(BF16)
    c_kv = _rms_f32(pm[:, Q_LORA:Q_LORA + KV_LORA], kv_norm).astype(BF16)
    k_pe = pm[:, Q_LORA + KV_LORA:MLA_IN]
    q_up_h = q_up.reshape(Q_LORA, MLA_HEADS, MLA_NOPE + MLA_ROPE)
    q_up_r = jnp.concatenate([q_up_h[:, :, :MLA_NOPE].reshape(Q_LORA, -1),
                              q_up_h[:, :, MLA_NOPE:].reshape(Q_LORA, -1)], axis=1).astype(BF16)
    kv_up_h = kv_up.reshape(KV_LORA, MLA_HEADS, MLA_NOPE + MLA_V)
    kv_up_r = jnp.concatenate([kv_up_h[:, :, :MLA_NOPE].reshape(KV_LORA, -1),
                               kv_up_h[:, :, MLA_NOPE:].reshape(KV_LORA, -1)], axis=1).astype(BF16)
    q_all = _matmul(c_q, q_up_r, out_dtype=F32, tm=1024, tn=1536, tk=Q_LORA, name="mla_q_up")
    kv = _matmul(c_kv, kv_up_r, out_dtype=BF16, tm=1024, tn=2048, tk=KV_LORA, name="mla_kv_up")
    q_all = q_all * (MLA_SCALE * LOG2E)
    q_nope = q_all[:, :MLA_HEADS * MLA_NOPE].astype(BF16)
    q_pe = q_all[:, MLA_HEADS * MLA_NOPE:].reshape(N_LAT_ROWS, MLA_HEADS, MLA_ROPE)
    q_pe = _rope_lat(q_pe, rope).transpose(1, 0, 2).astype(BF16)
    q_pe = jnp.pad(q_pe, ((0, 0), (0, 0), (0, MLA_NOPE - MLA_ROPE)))
    k_pe = _rope_lat(k_pe.reshape(N_ROWS, 1, MLA_ROPE), rope).reshape(N_ROWS, MLA_ROPE).astype(BF16)
    k_pe = jnp.pad(k_pe, ((0, 0), (0, MLA_NOPE - MLA_ROPE)))
    m_out = _mla_attention(q_nope, q_pe, kv, k_pe)

    dqkv = _qkv_prep(p, DIFF_IN, 0, rope, None,
                     ((0, DIFF_HEADS, None, True, DIFF_SCALE * LOG2E), (DIFF_HEADS, DIFF_HEADS, None, True, 1.0),
                      (2 * DIFF_HEADS, DIFF_HEADS, None, False, 1.0)), DIFF_HEAD // 4, "diff_prep")
    d_out = _diff_attention(lam.reshape(1), dqkv, subln, 1.0 - lam_init)
    return jnp.concatenate([m_out, d_out], axis=-1)


def kernel(x, c, ctx, c_ctx, ada_w, ada_b, norm1_g, norm2_g, mlp_w1, mlp_w2, final_g, ev_w_in, ev_w_out, rw_mu, rw_w0, rw_w_up, rw_a0, rw_a_up, rw_g_up, rw_k_k, rw_k_a, rw_r_k, rw_ln_w, rw_ln_b, gq_q_norm, gq_k_norm, od_w_in, od_w_out, mla_q_norm, mla_q_up, mla_kv_norm, mla_kv_up, diff_lq1, diff_lk1, diff_lq2, diff_lk2, diff_subln):
    rope_gq = _axial_rope_tables(SEQ, GQ_HEAD)
    rope_64 = _axial_rope_tables(SEQ, MLA_ROPE)
    h = jnp.concatenate([x.reshape(N_LAT_ROWS, D_MODEL), ctx.reshape(N_CTX_ROWS, D_MODEL)], axis=0)
    cond = jnp.concatenate([c_ctx[None], c, jnp.zeros((3, D_MODEL), F32)], axis=0)
    cond = jax.nn.silu(cond).astype(BF16)
    for i in range(DEPTH):
        last = i == DEPTH - 1
        mod = _matmul(cond, ada_w, layer=i, out_dtype=F32, tm=8, tn=2048, tk=1024, name="adaln_mod")
        mod = (mod + ada_b[i])[:BATCH + 1].reshape(BATCH + 1, N_MOD, 1, D_MODEL)
        sh1, sc1, g1, sh2, sc2, g2 = (mod[:, m] for m in range(N_MOD))
        a_all = _norm(h, norm1_g[i], sc1, sh1, rows=N_ROWS, out_dtype=BF16)
        j = i // 2
        if i % 2 == 0:
            rw_args = (rw_mu[j], rw_w0[j], rw_w_up[j], rw_a0[j], rw_a_up[j], rw_g_up[j], rw_k_k[j], rw_k_a[j],
                       rw_r_k[j], rw_ln_w[j], rw_ln_b[j])
            mix = _even_mixer(a_all, rope_gq, ev_w_in[j], rw_args, gq_q_norm[j], gq_k_norm[j], not last)
            w_out = ev_w_out[j]
        else:
            if not last:
                raise NotImplementedError("context rows of an odd layer are only needed when a layer follows")
            mix = _odd_mixer(a_all, rope_64, od_w_in[j], mla_q_norm[j], mla_q_up[j], mla_kv_norm[j],
                             mla_kv_up[j], diff_lq1[j], diff_lk1[j], diff_lq2[j], diff_lk2[j], diff_subln[j], i)
            w_out = od_w_out[j]
        rows = N_LAT_ROWS if last else N_ROWS
        h = _matmul(mix, w_out.astype(BF16), out_dtype=F32, tm=1024, tn=1024, tk=2048, rows=rows,
                    epilogue="gated_residual", res=h, gate=g1, name="mixer_w_out")
        a2 = _norm(h, norm2_g[i], sc2, sh2, rows=rows, out_dtype=BF16)
        hid = _matmul(a2, mlp_w1, layer=i, out_dtype=BF16, tm=1024, tn=512, tk=D_MODEL,
                      epilogue="relu2", name="mlp_w1")
        h = _matmul(hid, mlp_w2, layer=i, out_dtype=F32, tm=1024, tn=1024, tk=2048,
                    epilogue="gated_residual", res=h, gate=g2, name="mlp_w2")
    out = _norm(h, final_g, rows=N_LAT_ROWS, out_dtype=F32)
    return out.reshape(BATCH, SEQ, D_MODEL)
```

```python
import functools
import math

import jax
import jax.numpy as jnp
from jax import lax
from jax.experimental import pallas as pl
from jax.experimental.pallas import tpu as pltpu

D_MODEL = 4096
BATCH = 4
SEQ = 4096
DEPTH = 2
CTX_LEN = 256
GRID_W = 64
ROPE_THETA = 10000.0
NORM_EPS = 1e-6
MLP_HIDDEN = 4 * D_MODEL
N_MOD = 6
HALF = D_MODEL // 2

RW_HEAD = 64
RW_W = HALF
RW_HEADS = RW_W // RW_HEAD
DECAY_RANK = 96
AAA_RANK = 96
GATE_RANK = 256
GN_EPS = 64e-5
RW_IN = 3 * RW_W + GATE_RANK + 2 * DECAY_RANK + 2 * AAA_RANK

GQ_HEAD = 128
GQ_HEADS = HALF // GQ_HEAD
GQ_KV_HEADS = GQ_HEADS // 4
GQ_GROUP = GQ_HEADS // GQ_KV_HEADS
GQ_SCALE = GQ_HEAD ** -0.5
IN_EVEN = RW_IN + (GQ_HEADS + 2 * GQ_KV_HEADS) * GQ_HEAD

MLA_NOPE = 128
MLA_ROPE = 64
MLA_V = 128
MLA_HEADS = HALF // MLA_V
Q_LORA = 768
KV_LORA = 512
MLA_SCALE = (MLA_NOPE + MLA_ROPE) ** -0.5
MLA_IN = Q_LORA + KV_LORA + MLA_ROPE

DIFF_HEAD = 64
DIFF_V = 2 * DIFF_HEAD
DIFF_HEADS = HALF // DIFF_V
DIFF_SCALE = DIFF_HEAD ** -0.5
DIFF_IN = DIFF_HEADS * (4 * DIFF_HEAD + DIFF_V)
IN_ODD = MLA_IN + DIFF_IN
IN_ODD_PADDED = 7680

N_LAT_ROWS = BATCH * SEQ
N_CTX_ROWS = BATCH * CTX_LEN
N_ROWS = N_LAT_ROWS + N_CTX_ROWS
T_ALL = CTX_LEN + SEQ
CTX_BLOCK0 = N_LAT_ROWS // CTX_LEN

V7X_VMEM_LIMIT_BYTES = 56 * 1024 * 1024

BF16 = jnp.bfloat16
F32 = jnp.float32


def _params(*sem):
    return pltpu.CompilerParams(dimension_semantics=sem, vmem_limit_bytes=V7X_VMEM_LIMIT_BYTES)


def _row_group(row_tile, tile_rows):
    start = row_tile * tile_rows
    return jnp.where(start < N_LAT_ROWS, 1 + start // SEQ, 0)


def _norm_kernel(x_ref, g_ref, *rest, modulate):
    if modulate:
        sc_ref, sh_ref, o_ref = rest
    else:
        (o_ref,) = rest
    x = x_ref[...]
    ms = jnp.mean(x * x, axis=-1, keepdims=True)
    y = x * lax.rsqrt(ms + NORM_EPS) * g_ref[...]
    if modulate:
        y = y * (1.0 + sc_ref[0]) + sh_ref[0]
    o_ref[...] = y.astype(o_ref.dtype)


def _norm(x, g, sc=None, sh=None, *, rows, out_dtype, tile=256):
    d = x.shape[1]
    modulate = sc is not None
    in_specs = [pl.BlockSpec((tile, d), lambda i: (i, 0)),
                pl.BlockSpec((1, d), lambda i: (0, 0))]
    args = [x, g.reshape(1, d)]
    if modulate:
        mod_spec = pl.BlockSpec((1, 1, d), lambda i: (_row_group(i, tile), 0, 0))
        in_specs += [mod_spec, mod_spec]
        args += [sc, sh]
    return pl.pallas_call(
        functools.partial(_norm_kernel, modulate=modulate),
        grid=(rows // tile,),
        in_specs=in_specs,
        out_specs=pl.BlockSpec((tile, d), lambda i: (i, 0)),
        out_shape=jax.ShapeDtypeStruct((rows, d), out_dtype),
        compiler_params=_params("parallel"),
        name="rmsnorm_mod" if modulate else "rmsnorm",
    )(*args)


def _mm_kernel(*refs, epilogue, nk, n_a):
    a_refs, w_ref, rest = refs[:n_a], refs[n_a], refs[n_a + 1:]
    if epilogue == "gated_residual":
        res_ref, gate_ref, o_ref, acc_ref = rest
    else:
        o_ref, acc_ref = rest
    k = pl.program_id(2)

    def product(a_ref):
        return jnp.dot(a_ref[...].astype(BF16), w_ref[...].astype(BF16), preferred_element_type=F32)

    def store(part, first):
        if first:
            acc_ref[...] = part
        else:
            acc_ref[...] += part

    if n_a == 1:
        part = product(a_refs[0])
        pl.when(k == 0)(lambda: store(part, True))
        pl.when(k > 0)(lambda: store(part, False))
    else:
        for idx, a_ref in enumerate(a_refs):
            pl.when(k == idx)(lambda a_ref=a_ref, idx=idx: store(product(a_ref), idx == 0))

    @pl.when(k == nk - 1)
    def _():
        acc = acc_ref[...]
        if epilogue == "relu2":
            acc = jnp.square(jnp.maximum(acc, 0.0))
        elif epilogue == "gated_residual":
            acc = res_ref[...] + gate_ref[0] * acc
        o_ref[...] = acc.astype(o_ref.dtype)


def _matmul(a, w, *, out_dtype, tm, tn, tk, rows=None, layer=None, epilogue=None, res=None, gate=None, name):
    kdim, n = w.shape[-2:]
    a_parts = a if isinstance(a, tuple) else (a,)
    m = a_parts[0].shape[0] if rows is None else rows
    assert sum(part.shape[1] for part in a_parts) == kdim and m % tm == 0 and n % tn == 0 and kdim % tk == 0
    assert (w.ndim == 3) == (layer is not None)
    nk = kdim // tk
    w_spec = (pl.BlockSpec((tk, tn), lambda i, j, k: (k, j)) if layer is None else
              pl.BlockSpec((None, tk, tn), lambda i, j, k: (layer, k, j)))
    if len(a_parts) == 1:
        in_specs = [pl.BlockSpec((tm, tk), lambda i, j, k: (i, k)), w_spec]
    else:
        assert len(a_parts) == nk and all(part.shape[1] == tk for part in a_parts)
        in_specs = [pl.BlockSpec((tm, tk), lambda i, j, k: (i, 0))] * nk + [w_spec]
    args = [*a_parts, w]
    if epilogue == "gated_residual":
        in_specs += [pl.BlockSpec((tm, tn), lambda i, j, k: (i, j)),
                     pl.BlockSpec((1, 1, tn), lambda i, j, k: (_row_group(i, tm), 0, j))]
        args += [res, gate]
    return pl.pallas_call(
        functools.partial(_mm_kernel, epilogue=epilogue, nk=nk, n_a=len(a_parts)),
        grid=(m // tm, n // tn, nk),
        in_specs=in_specs,
        out_specs=pl.BlockSpec((tm, tn), lambda i, j, k: (i, j)),
        out_shape=jax.ShapeDtypeStruct((m, n), out_dtype),
        scratch_shapes=[pltpu.VMEM((tm, tn), F32)],
        compiler_params=_params("parallel", "parallel", "arbitrary"),
        name=name,
    )(*args)


def _nt_dot(a, b):
    return lax.dot_general(a, b, (((1,), (1,)), ((), ())), preferred_element_type=F32)


LOG2E = math.log2(math.e)


def _fill_values_and_ones(v_ref, vo_ref):
    d = v_ref.shape[1]
    vo_ref[:, :d] = v_ref[...]
    vo_ref[:, d:] = jnp.ones_like(v_ref)


def _softmax_times_values(s_list, vo_refs):
    d = vo_refs[0].shape[1] // 2
    m = functools.reduce(jnp.maximum, [jnp.max(s, axis=-1, keepdims=True) for s in s_list])
    acc = functools.reduce(jnp.add, [jnp.dot(jnp.exp2(s - m).astype(BF16), vo_ref[...], preferred_element_type=F32)
                                     for s, vo_ref in zip(s_list, vo_refs)])
    return acc[:, :d] / acc[:, d:]


def _attend_each(items):
    outs = []
    pending = None
    for q, k_refs, vo_refs in items:
        s_list = [_nt_dot(q, k_ref[...]) for k_ref in k_refs]
        if pending is not None:
            outs.append(_softmax_times_values(*pending))
        pending = (s_list, vo_refs)
    outs.append(_softmax_times_values(*pending))
    return outs


def _gqa_kernel(q_ref, *rest, with_lat):
    if with_lat:
        kc_ref, vc_ref, kl_ref, vl_ref, o_ref, voc_ref, vol_ref = rest
        k_refs, v_refs, vo_refs = (kc_ref, kl_ref), (vc_ref, vl_ref), (voc_ref, vol_ref)
    else:
        kc_ref, vc_ref, o_ref, voc_ref = rest
        k_refs, v_refs, vo_refs = (kc_ref,), (vc_ref,), (voc_ref,)

    @pl.when(pl.program_id(2) == 0)
    def _():
        for v_ref, vo_ref in zip(v_refs, vo_refs):
            _fill_values_and_ones(v_ref, vo_ref)

    outs = _attend_each([(q_ref[:, g * GQ_HEAD:(g + 1) * GQ_HEAD], k_refs, vo_refs) for g in range(GQ_GROUP)])
    for g, o in enumerate(outs):
        o_ref[:, g * GQ_HEAD:(g + 1) * GQ_HEAD] = o.astype(o_ref.dtype)


def _gqa_attention(qkv, *, latent, tq=256):
    gw = GQ_GROUP * GQ_HEAD
    k0, v0 = GQ_HEADS, GQ_HEADS + GQ_KV_HEADS
    ctx_k = pl.BlockSpec((CTX_LEN, GQ_HEAD), lambda b, n, i: (CTX_BLOCK0 + b, k0 + n))
    ctx_v = pl.BlockSpec((CTX_LEN, GQ_HEAD), lambda b, n, i: (CTX_BLOCK0 + b, v0 + n))
    scratch = [pltpu.VMEM((CTX_LEN, 2 * GQ_HEAD), BF16)]
    if latent:
        per_b = SEQ // tq
        lat_k = pl.BlockSpec((SEQ, GQ_HEAD), lambda b, n, i: (b, k0 + n))
        lat_v = pl.BlockSpec((SEQ, GQ_HEAD), lambda b, n, i: (b, v0 + n))
        in_specs = [pl.BlockSpec((tq, gw), lambda b, n, i: (b * per_b + i, n)), ctx_k, ctx_v, lat_k, lat_v]
        args = (qkv,) * 5
        rows = N_LAT_ROWS
        scratch.append(pltpu.VMEM((SEQ, 2 * GQ_HEAD), BF16))
    else:
        per_b = CTX_LEN // tq
        q_blk0 = N_LAT_ROWS // tq
        in_specs = [pl.BlockSpec((tq, gw), lambda b, n, i: (q_blk0 + b * per_b + i, n)), ctx_k, ctx_v]
        args = (qkv,) * 3
        rows = N_CTX_ROWS
    return pl.pallas_call(
        functools.partial(_gqa_kernel, with_lat=latent),
        grid=(BATCH, GQ_KV_HEADS, per_b),
        in_specs=in_specs,
        out_specs=pl.BlockSpec((tq, gw), lambda b, n, i: (b * per_b + i, n)),
        out_shape=jax.ShapeDtypeStruct((rows, GQ_HEADS * GQ_HEAD), BF16),
        scratch_shapes=scratch,
        compiler_params=_params("parallel", "parallel", "arbitrary"),
        name="gqa_attention_lat" if latent else "gqa_attention_ctx",
    )(*args)


def _mla_kernel(qn_ref, qp_ref, knc_ref, kpc_ref, vc_ref, knl_ref, kpl_ref, vl_ref, o_ref,
                kc_ref, kl_ref, voc_ref, vol_ref):
    heads = range(MLA_HEADS_PER_STEP)
    cols = [slice(j * MLA_NOPE, (j + 1) * MLA_NOPE) for j in heads]

    @pl.when(pl.program_id(2) == 0)
    def _():
        for kn_ref, kp_ref, v_ref, k_ref, vo_ref in ((knc_ref, kpc_ref, vc_ref, kc_ref, voc_ref),
                                                     (knl_ref, kpl_ref, vl_ref, kl_ref, vol_ref)):
            for j in heads:
                k_ref[j, :, :MLA_NOPE] = kn_ref[:, cols[j]]
                k_ref[j, :, MLA_NOPE:] = kp_ref[...]
                vo_ref[j, :, :MLA_V] = v_ref[:, cols[j]]
                vo_ref[j, :, MLA_V:] = jnp.ones_like(kp_ref)

    items = [(jnp.concatenate([qn_ref[:, cols[j]], qp_ref[j]], axis=1), (kc_ref.at[j], kl_ref.at[j]),
              (voc_ref.at[j], vol_ref.at[j])) for j in heads]
    for j, o in enumerate(_attend_each(items)):
        o_ref[:, cols[j]] = o.astype(o_ref.dtype)


MLA_HEADS_PER_STEP = 2


def _mla_attention(q_nope, q_pe, kv, k_pe, *, tq=256):
    per_b = SEQ // tq
    hs = MLA_HEADS_PER_STEP
    w = hs * MLA_NOPE
    v_blk0 = MLA_HEADS // hs
    in_specs = [
        pl.BlockSpec((tq, w), lambda b, h, i: (b * per_b + i, h)),
        pl.BlockSpec((hs, tq, MLA_NOPE), lambda b, h, i: (h, b * per_b + i, 0)),
        pl.BlockSpec((CTX_LEN, w), lambda b, h, i: (CTX_BLOCK0 + b, h)),
        pl.BlockSpec((CTX_LEN, MLA_NOPE), lambda b, h, i: (CTX_BLOCK0 + b, 0)),
        pl.BlockSpec((CTX_LEN, w), lambda b, h, i: (CTX_BLOCK0 + b, v_blk0 + h)),
        pl.BlockSpec((SEQ, w), lambda b, h, i: (b, h)),
        pl.BlockSpec((SEQ, MLA_NOPE), lambda b, h, i: (b, 0)),
        pl.BlockSpec((SEQ, w), lambda b, h, i: (b, v_blk0 + h)),
    ]
    return pl.pallas_call(
        _mla_kernel,
        grid=(BATCH, MLA_HEADS // hs, per_b),
        in_specs=in_specs,
        out_specs=pl.BlockSpec((tq, w), lambda b, h, i: (b * per_b + i, h)),
        out_shape=jax.ShapeDtypeStruct((N_LAT_ROWS, MLA_HEADS * MLA_V), BF16),
        scratch_shapes=[pltpu.VMEM((hs, CTX_LEN, 2 * MLA_NOPE), BF16), pltpu.VMEM((hs, SEQ, 2 * MLA_NOPE), BF16),
                        pltpu.VMEM((hs, CTX_LEN, 2 * MLA_V), BF16), pltpu.VMEM((hs, SEQ, 2 * MLA_V), BF16)],
        compiler_params=_params("parallel", "parallel", "arbitrary"),
        name="mla_attention",
    )(q_nope, q_pe, kv, k_pe, kv, kv, k_pe, kv)


def _diff_kernel(lam_ref, q_ref, kc_ref, vc_ref, kl_ref, vl_ref, g_ref, o_ref, voc_ref, vol_ref, *, out_scale):
    heads = range(DIFF_HEADS_PER_STEP)
    cols = [slice(j * DIFF_V, (j + 1) * DIFF_V) for j in heads]

    @pl.when(pl.program_id(2) == 0)
    def _():
        for v_ref, vo_ref in ((vc_ref, voc_ref), (vl_ref, vol_ref)):
            for j in heads:
                vo_ref[j, :, :DIFF_V] = v_ref[:, cols[j]]
                vo_ref[j, :, DIFF_V:] = jnp.ones((v_ref.shape[0], DIFF_V), BF16)

    first = lax.broadcasted_iota(jnp.int32, (q_ref.shape[0], DIFF_V), 1) < DIFF_HEAD
    items = []
    for j in heads:
        q = q_ref[:, cols[j]]
        zero = jnp.zeros_like(q)
        keys = (kc_ref[:, cols[j]], kl_ref[:, cols[j]])
        for qh in (jnp.where(first, q, zero), jnp.where(first, zero, q)):
            items.append((qh, keys, (voc_ref.at[j], vol_ref.at[j])))
    outs = _attend_each(items)
    for j in heads:
        o = outs[2 * j] - lam_ref[0] * outs[2 * j + 1]
        ms = jnp.mean(o * o, axis=-1, keepdims=True)
        o = o * lax.rsqrt(ms + NORM_EPS) * g_ref[...]
        o_ref[:, cols[j]] = (o * out_scale).astype(o_ref.dtype)


DIFF_HEADS_PER_STEP = 2


def _diff_attention(lam, qkv, subln, out_scale, *, tq=256):
    per_b = SEQ // tq
    hs = DIFF_HEADS_PER_STEP
    w = hs * DIFF_V
    k0, v0 = DIFF_HEADS // hs, 2 * DIFF_HEADS // hs
    ctx_k = pl.BlockSpec((CTX_LEN, w), lambda b, h, i: (CTX_BLOCK0 + b, k0 + h))
    ctx_v = pl.BlockSpec((CTX_LEN, w), lambda b, h, i: (CTX_BLOCK0 + b, v0 + h))
    lat_k = pl.BlockSpec((SEQ, w), lambda b, h, i: (b, k0 + h))
    lat_v = pl.BlockSpec((SEQ, w), lambda b, h, i: (b, v0 + h))
    in_specs = [
        pl.BlockSpec(memory_space=pltpu.SMEM),
        pl.BlockSpec((tq, w), lambda b, h, i: (b * per_b + i, h)),
        ctx_k, ctx_v, lat_k, lat_v,
        pl.BlockSpec((1, DIFF_V), lambda b, h, i: (0, 0)),
    ]
    return pl.pallas_call(
        functools.partial(_diff_kernel, out_scale=out_scale),
        grid=(BATCH, DIFF_HEADS // hs, per_b),
        in_specs=in_specs,
        out_specs=pl.BlockSpec((tq, w), lambda b, h, i: (b * per_b + i, h)),
        out_shape=jax.ShapeDtypeStruct((N_LAT_ROWS, DIFF_HEADS * DIFF_V), BF16),
        scratch_shapes=[pltpu.VMEM((hs, CTX_LEN, 2 * DIFF_V), BF16), pltpu.VMEM((hs, SEQ, 2 * DIFF_V), BF16)],
        compiler_params=_params("parallel", "parallel", "arbitrary"),
        name="diff_attention",
    )(lam, qkv, qkv, qkv, qkv, qkv, subln.reshape(1, DIFF_V))


RW_CHUNK = 128
RW_PAIRS = RW_HEADS // 2
LANES = 2 * RW_HEAD
SUB = 8
RW_UNROLL = 16
RW_GROUP = 8


def _rwkv_kernel(r_in, kk_in, v_in, w_in, k_in, b_in, y_out, s_ref, r_ref, kk_ref, v_ref, w_ref, k_ref, b_ref,
                 y_ref):
    d = pl.program_id(0)
    c = pl.program_id(2)

    @pl.when(c == 0)
    def _():
        s_ref[...] = jnp.zeros_like(s_ref)

    for p in range(RW_PAIRS):
        cols = slice(p * LANES, (p + 1) * LANES)
        for src, dst in ((r_in, r_ref), (kk_in, kk_ref), (v_in, v_ref)):
            dst[p] = src[:, cols]
        for src, dst in ((w_in, w_ref), (k_in, k_ref), (b_in, b_ref)):
            dst[0, p] = src[0, :, cols]

    v_hi = lax.broadcasted_iota(jnp.int32, (SUB, SUB, LANES), 0)
    v_lo = lax.broadcasted_iota(jnp.int32, (SUB, SUB, LANES), 1)
    lane3 = lax.broadcasted_iota(jnp.int32, (SUB, SUB, LANES), 2)
    dup = jnp.where((lane3 & (RW_HEAD - 1)) == v_hi * SUB + v_lo, 1.0, 0.0)
    blk_r = lax.broadcasted_iota(jnp.int32, (LANES, LANES), 0) // RW_HEAD
    blk_c = lax.broadcasted_iota(jnp.int32, (LANES, LANES), 1) // RW_HEAD
    head_ones = jnp.where(blk_r == blk_c, 1.0, 0.0).astype(BF16)
    slot_of_lane = lax.broadcasted_iota(jnp.int32, (SUB, LANES), 1) & (RW_HEAD - 1)
    step_dir = jnp.where(d == 0, 1, -1)

    y_ref[...] = jnp.zeros_like(y_ref)

    def row(ref, p, t, *lead):
        return ref[(*lead, p, pl.ds(t, SUB, stride=0), slice(None))]

    def as_rows(tile):
        return tile.reshape(RW_HEAD, LANES).astype(BF16)

    def head_sums(states, pairs, t_done, t_next):
        rows = []
        for p in pairs:
            rows.append(as_rows(states[p] * row(r_ref, p, t_done)))
            if t_next is not None:
                rows.append(as_rows(states[p] * row(kk_ref, p, t_next)))
                rows.append(as_rows(dup * row(v_ref, p, t_next)))
        return jnp.dot(jnp.concatenate(rows, axis=0), head_ones, preferred_element_type=F32)

    def slab(sums, q, n_slabs, i):
        lo = (q * n_slabs + i) * RW_HEAD
        return sums[lo:lo + RW_HEAD].reshape(SUB, SUB, LANES)

    def put_y(p, y_b, t, valid):
        window = t // RW_HEAD
        keep = (slot_of_lane == t % RW_HEAD) & valid
        y_ref[0, p, window] = jnp.where(keep, y_b, y_ref[0, p, window])

    groups = [range(g0, g0 + RW_GROUP) for g0 in range(0, RW_PAIRS, RW_GROUP)]

    def block(g, carry):
        first = g * RW_UNROLL
        first = jnp.where(d == 0, first, RW_CHUNK - 1 - first)
        times = [first + u * step_dir for u in range(RW_UNROLL)]
        t_before = jnp.clip(first - step_dir, 0, RW_CHUNK - 1)
        states = [s_ref[p] for p in range(RW_PAIRS)]
        sums = [head_sums(states, grp, t_before, times[0]) for grp in groups]
        for gi, grp in enumerate(groups):
            for q, p in enumerate(grp):
                put_y(p, slab(sums[gi], q, 3, 0), t_before, g > 0)
        for u in range(RW_UNROLL):
            t = times[u]
            for gi, grp in enumerate(groups):
                for q, p in enumerate(grp):
                    sa = slab(sums[gi], q, 3, 1)
                    vb = slab(sums[gi], q, 3, 2)
                    states[p] = (states[p] * row(w_ref, p, t, 0) - sa * row(b_ref, p, t, 0)
                                 + vb * row(k_ref, p, t, 0))
                if u + 1 < RW_UNROLL:
                    sums[gi] = head_sums(states, grp, t, times[u + 1])
                    for q, p in enumerate(grp):
                        put_y(p, slab(sums[gi], q, 3, 0), t, True)
        for p in range(RW_PAIRS):
            s_ref[p] = states[p]
        return carry

    lax.fori_loop(0, RW_CHUNK // RW_UNROLL, block, 0)

    t_last = jnp.where(d == 0, RW_CHUNK - 1, 0)
    states = [s_ref[p] for p in range(RW_PAIRS)]
    for grp in groups:
        last = head_sums(states, grp, t_last, None)
        for q, p in enumerate(grp):
            put_y(p, slab(last, q, 1, 0), t_last, True)

    first_head = lax.broadcasted_iota(jnp.int32, (RW_HEAD, LANES), 1) < RW_HEAD
    for p in range(0, RW_PAIRS, 2):
        for window in range(RW_CHUNK // RW_HEAD):
            both = jnp.concatenate([y_ref[0, p, window].reshape(RW_HEAD, LANES),
                                    y_ref[0, p + 1, window].reshape(RW_HEAD, LANES)], axis=0)
            by_row = both.T
            head0, head1 = by_row[:RW_HEAD], by_row[RW_HEAD:]
            rows = slice(window * RW_HEAD, (window + 1) * RW_HEAD)
            y_out[0, rows, p * LANES:(p + 1) * LANES] = jnp.where(
                first_head, head0, pltpu.roll(head1, RW_HEAD, axis=1))
            y_out[0, rows, (p + 1) * LANES:(p + 2) * LANES] = jnp.where(
                first_head, pltpu.roll(head0, RW_HEAD, axis=1), head1)


def _rwkv_scan(r, kk, v, w, k, b):
    n_ctx = CTX_LEN // RW_CHUNK
    n_lat = SEQ // RW_CHUNK
    ctx0 = N_LAT_ROWS // RW_CHUNK

    def chunk(d, bb, c):
        j = jnp.where(d == 0, c, jnp.where(c < n_ctx, n_ctx - 1 - c, n_ctx + n_lat - 1 - (c - n_ctx)))
        return jnp.where(j < n_ctx, ctx0 + bb * n_ctx + j, bb * n_lat + j - n_ctx)

    shared = pl.BlockSpec((RW_CHUNK, RW_W), lambda d, bb, c: (chunk(d, bb, c), 0))
    per_dir = pl.BlockSpec((1, RW_CHUNK, RW_W), lambda d, bb, c: (d, chunk(d, bb, c), 0))
    staged = pltpu.VMEM((RW_PAIRS, RW_CHUNK, LANES), F32)
    staged_dir = pltpu.VMEM((1, RW_PAIRS, RW_CHUNK, LANES), F32)
    return pl.pallas_call(
        _rwkv_kernel,
        grid=(2, BATCH, n_ctx + n_lat),
        in_specs=[shared, shared, shared, per_dir, per_dir, per_dir],
        out_specs=per_dir,
        out_shape=jax.ShapeDtypeStruct((2, N_ROWS, RW_W), F32),
        scratch_shapes=[pltpu.VMEM((RW_PAIRS, RW_HEAD // SUB, SUB, LANES), F32),
                        staged, staged, staged, staged_dir, staged_dir, staged_dir,
                        pltpu.VMEM((1, RW_PAIRS, RW_CHUNK // RW_HEAD, SUB, SUB, LANES), F32)],
        compiler_params=_params("parallel", "parallel", "arbitrary"),
        name="rwkv7_scan",
    )(r, kk, v, w, k, b)


def _axial_rope_tables(n_tokens, rot_dim):
    n_rows = n_tokens // GRID_W
    row = jnp.repeat(jnp.arange(n_rows, dtype=F32), GRID_W)
    col = jnp.tile(jnp.arange(GRID_W, dtype=F32), n_rows)
    axis_dim = rot_dim // 2
    inv_freq = ROPE_THETA ** (-jnp.arange(0, axis_dim, 2, dtype=F32) / axis_dim)
    ang_r = row[:, None] * inv_freq
    ang_c = col[:, None] * inv_freq
    ang = jnp.concatenate([ang_r, ang_r, ang_c, ang_c], axis=-1)
    return jnp.cos(ang), jnp.sin(ang)


def _rotate_half(z):
    z1, z2 = jnp.split(z, 2, axis=-1)
    return jnp.concatenate([-z2, z1], axis=-1)


def _apply_rope(x, cos, sin):
    half = x.shape[-1] // 2
    rot = jnp.concatenate([_rotate_half(x[..., :half]), _rotate_half(x[..., half:])], axis=-1)
    return x * cos + rot * sin


def _rms_f32(x, g):
    return x * lax.rsqrt(jnp.mean(x * x, axis=-1, keepdims=True) + NORM_EPS) * g


def _rope_lat(x, rope):
    cos, sin = rope
    heads, dim = x.shape[1:]
    lat = x[:N_LAT_ROWS].reshape(BATCH, SEQ, heads, dim)
    lat = _apply_rope(lat, cos[None, :, None, :], sin[None, :, None, :]).reshape(N_LAT_ROWS, heads, dim)
    return lat if x.shape[0] == N_LAT_ROWS else jnp.concatenate([lat, x[N_LAT_ROWS:]], axis=0)


RW_SLAB = 512
RW_TAIL = GATE_RANK + 2 * DECAY_RANK + 2 * AAA_RANK
RW_TAIL_PAD = 768
RW_ROW_TILE = 256


def _head_lane_sums(x):
    rows = x.shape[0]
    blk_r = lax.broadcasted_iota(jnp.int32, (LANES, LANES), 0) // RW_HEAD
    blk_c = lax.broadcasted_iota(jnp.int32, (LANES, LANES), 1) // RW_HEAD
    head_ones = jnp.where(blk_r == blk_c, 1.0, 0.0).astype(BF16)
    hi = x.astype(BF16)
    rest = x - hi.astype(F32)
    mid = rest.astype(BF16)
    lo = (rest - mid.astype(F32)).astype(BF16)
    pieces = jnp.concatenate([hi, mid, lo], axis=0)
    out = []
    for s in range(x.shape[1] // LANES):
        part = jnp.dot(pieces[:, s * LANES:(s + 1) * LANES], head_ones, preferred_element_type=F32)
        out.append(part[:rows] + part[rows:2 * rows] + part[2 * rows:])
    return jnp.concatenate(out, axis=1)


def _rwkv_prep_kernel(*refs):
    slabs = [refs[4 * n:4 * n + 4] for n in range(4)]
    (up_w_ref, w0_ref, a0_ref, kk_gain_ref, ka_ref, rk_ref,
     r_out, v_out, kk_out, decay_out, kdir_out, b_out, g_out, bonus_out) = refs[16:30]
    stage_refs = refs[30:34]
    tile = RW_ROW_TILE
    row0 = pl.program_id(0) * tile
    seq_len = jnp.where(row0 < N_LAT_ROWS, SEQ, CTX_LEN)
    starts_seq = row0 % seq_len == 0
    ends_seq = (row0 + tile) % seq_len == 0

    def shifted(main_ref, before_ref, after_ref, mu_ref, stage_ref):
        x = main_ref[...]
        stage_ref[SUB:SUB + tile, :] = x
        stage_ref[SUB - 1:SUB, :] = jnp.where(starts_seq, 0.0, before_ref[SUB - 1:SUB, :])
        stage_ref[SUB + tile:SUB + tile + 1, :] = jnp.where(ends_seq, 0.0, after_ref[0:1, :])
        around = 0.5 * (stage_ref[SUB - 1:SUB - 1 + tile, :] + stage_ref[SUB + 1:SUB + 1 + tile, :])
        return x + (around - x) * mu_ref[...]

    r, k, v, tail = (shifted(*slab, stage) for slab, stage in zip(slabs, stage_refs))
    r_out[...] = r
    v_out[...] = v
    s_w = RW_SLAB
    lane = lax.broadcasted_iota(jnp.int32, (1, RW_TAIL_PAD), 1)
    act = jnp.where(lane < GATE_RANK, jax.nn.sigmoid(tail),
                    jnp.where(lane < GATE_RANK + 2 * DECAY_RANK, jnp.tanh(tail), tail))
    up = jnp.dot(act.astype(BF16), up_w_ref[...], preferred_element_type=F32)
    g_out[...] = up[:, :s_w]
    kq = k * kk_gain_ref[...]
    kk = kq / jnp.maximum(jnp.sqrt(_head_lane_sums(kq * kq)), 1e-12)
    kk_out[...] = kk
    kdir_sum = jnp.zeros_like(k)
    for d in range(2):
        z = -(w0_ref[d] + up[:, (1 + d) * s_w:(2 + d) * s_w])
        softplus = jnp.maximum(z, 0.0) + jnp.log(1.0 + jnp.exp(-jnp.abs(z)))
        decay_out[d] = jnp.exp(-jnp.exp(-softplus - 0.5))
        a = jax.nn.sigmoid(a0_ref[d] + up[:, (3 + d) * s_w:(4 + d) * s_w])
        kd = k * (1.0 + (a - 1.0) * ka_ref[...])
        kdir_out[d] = kd
        b_out[d] = kk * a
        kdir_sum = kdir_sum + kd
    bonus_out[...] = _head_lane_sums(r * kdir_sum * rk_ref[...]) * v


def _rwkv_post_kernel(y_ref, g_ref, bonus_ref, lnw_ref, lnb_ref, o_ref):
    y = y_ref[0] + y_ref[1]
    mean = _head_lane_sums(y) * (1.0 / RW_HEAD)
    cen = y - mean
    var = _head_lane_sums(cen * cen) * (1.0 / RW_HEAD)
    yn = cen * lax.rsqrt(var + GN_EPS) * lnw_ref[...] + lnb_ref[...]
    o_ref[...] = ((yn + bonus_ref[...]) * g_ref[...]).astype(o_ref.dtype)


def _rwkv_mixer(p, tail_col, mu, w0, w_up, a0, a_up, g_up, k_k, k_a, r_k, ln_w, ln_b):
    assert tail_col % RW_TAIL_PAD == 0 and p.shape[1] >= tail_col + RW_TAIL_PAD
    n_slab = RW_W // RW_SLAB
    up_w = jnp.zeros((RW_TAIL_PAD, 5, RW_W), F32)
    o = GATE_RANK
    up_w = up_w.at[:o, 0].set(g_up)
    for d in range(2):
        up_w = up_w.at[o + d * DECAY_RANK:o + (d + 1) * DECAY_RANK, 1 + d].set(w_up[d])
    o += 2 * DECAY_RANK
    for d in range(2):
        up_w = up_w.at[o + d * AAA_RANK:o + (d + 1) * AAA_RANK, 3 + d].set(a_up[d])
    up_w = up_w.reshape(RW_TAIL_PAD, 5, n_slab, RW_SLAB).transpose(2, 0, 1, 3)
    up_w = up_w.reshape(n_slab, RW_TAIL_PAD, 5 * RW_SLAB).astype(BF16)
    mu_main = mu[:3 * RW_W].reshape(1, 3 * RW_W)
    mu_tail = jnp.pad(mu[3 * RW_W:], (0, RW_TAIL_PAD - RW_TAIL)).reshape(1, RW_TAIL_PAD)

    tile = RW_ROW_TILE
    col_blocks = RW_W // RW_SLAB
    halo = tile // SUB
    last_halo = N_ROWS // SUB - 1

    def with_halo(width, col):
        return [pl.BlockSpec((tile, width), lambda i, j: (i, col(j))),
                pl.BlockSpec((SUB, width), lambda i, j: (jnp.maximum(i * halo - 1, 0), col(j))),
                pl.BlockSpec((SUB, width), lambda i, j: (jnp.minimum((i + 1) * halo, last_halo), col(j)))]

    in_specs, args = [], []
    for n in range(3):
        in_specs += with_halo(RW_SLAB, lambda j, n=n: n * col_blocks + j)
        in_specs.append(pl.BlockSpec((1, RW_SLAB), lambda i, j, n=n: (0, n * col_blocks + j)))
        args += [p, p, p, mu_main]
    tail_blk = tail_col // RW_TAIL_PAD
    in_specs += with_halo(RW_TAIL_PAD, lambda j: tail_blk)
    in_specs.append(pl.BlockSpec((1, RW_TAIL_PAD), lambda i, j: (0, 0)))
    args += [p, p, p, mu_tail]

    vec = pl.BlockSpec((1, RW_SLAB), lambda i, j: (0, j))
    vec2 = pl.BlockSpec((2, 1, RW_SLAB), lambda i, j: (0, 0, j))
    out1 = pl.BlockSpec((tile, RW_SLAB), lambda i, j: (i, j))
    out2 = pl.BlockSpec((2, tile, RW_SLAB), lambda i, j: (0, i, j))
    one = jax.ShapeDtypeStruct((N_ROWS, RW_W), F32)
    two = jax.ShapeDtypeStruct((2, N_ROWS, RW_W), F32)
    in_specs += [pl.BlockSpec((None, RW_TAIL_PAD, 5 * RW_SLAB), lambda i, j: (j, 0, 0)), vec2, vec2, vec, vec, vec]
    args += [up_w, w0.reshape(2, 1, RW_W), a0.reshape(2, 1, RW_W), k_k.reshape(1, RW_W), k_a.reshape(1, RW_W),
             r_k.reshape(1, RW_W)]
    stage = lambda width: pltpu.VMEM((tile + 2 * SUB, width), F32)
    r, v, kk, decay, k_dir, b_dir, g, bonus = pl.pallas_call(
        _rwkv_prep_kernel,
        grid=(N_ROWS // tile, n_slab),
        in_specs=in_specs,
        out_specs=[out1, out1, out1, out2, out2, out2, out1, out1],
        out_shape=[one, one, one, two, two, two, one, one],
        scratch_shapes=[stage(RW_SLAB), stage(RW_SLAB), stage(RW_SLAB), stage(RW_TAIL_PAD)],
        compiler_params=_params("parallel", "parallel"),
        name="rwkv7_prep",
    )(*args)

    y2 = _rwkv_scan(r, kk, v, decay, k_dir, b_dir)

    return pl.pallas_call(
        _rwkv_post_kernel,
        grid=(N_ROWS // tile, n_slab),
        in_specs=[out2, out1, out1, vec, vec],
        out_specs=out1,
        out_shape=jax.ShapeDtypeStruct((N_ROWS, RW_W), BF16),
        compiler_params=_params("parallel", "parallel"),
        name="rwkv7_post",
    )(y2, g, bonus, ln_w.reshape(1, RW_W), ln_b.reshape(1, RW_W))


def _qkv_prep_kernel(x_ref, cos_ref, sin_ref, *rest, segments, quarter):
    gain_ref, o_ref = rest if len(rest) == 2 else (None, rest[0])
    _qkv_prep_body(x_ref, cos_ref, sin_ref, gain_ref, o_ref, segments, quarter)


def _qkv_prep_body(x_ref, cos_ref, sin_ref, gain_ref, o_ref, segments, quarter):
    tile = x_ref.shape[0]
    latent = pl.program_id(0) * tile < N_LAT_ROWS
    cos = jnp.where(latent, cos_ref[...], 1.0)
    sin = jnp.where(latent, sin_ref[...], 0.0)
    lane = lax.broadcasted_iota(jnp.int32, (1, LANES), 1)
    even_quarter = (lane // quarter) % 2 == 0
    for first, count, gain_row, rotary, scale in segments:
        for blk in range(first, first + count):
            cols = slice(blk * LANES, (blk + 1) * LANES)
            x = x_ref[:, cols]
            if gain_row is not None:
                x = x * lax.rsqrt(jnp.mean(x * x, axis=-1, keepdims=True) + NORM_EPS) * gain_ref[gain_row:gain_row + 1]
            if rotary:
                rot = jnp.where(even_quarter, -pltpu.roll(x, LANES - quarter, axis=1), pltpu.roll(x, quarter, axis=1))
                x = x * cos + rot * sin
            if scale != 1.0:
                x = x * scale
            o_ref[:, cols] = x.astype(o_ref.dtype)


def _qkv_prep(p, width, col_block, rope, gains, segments, quarter, name, tile=256):
    cos, sin = rope
    reps = LANES // cos.shape[1]
    cos, sin = jnp.tile(cos, (1, reps)), jnp.tile(sin, (1, reps))
    per_seq = SEQ // tile
    table = pl.BlockSpec((tile, LANES), lambda i: (i % per_seq, 0))
    return pl.pallas_call(
        functools.partial(_qkv_prep_kernel, segments=segments, quarter=quarter),
        grid=(N_ROWS // tile,),
        in_specs=[pl.BlockSpec((tile, width), lambda i: (i, col_block)), table, table]
        + ([] if gains is None else [pl.BlockSpec(gains.shape, lambda i: (0, 0))]),
        out_specs=pl.BlockSpec((tile, width), lambda i: (i, 0)),
        out_shape=jax.ShapeDtypeStruct((N_ROWS, width), BF16),
        compiler_params=_params("parallel"),
        name=name,
    )(p, cos, sin, *(() if gains is None else (gains,)))


def _even_mixer(a_all, rope, w_in, rw_args, q_g, k_g, need_ctx):
    rkv_w = 3 * RW_W
    att_w = (GQ_HEADS + 2 * GQ_KV_HEADS) * GQ_HEAD
    w_in = jnp.concatenate([w_in[:, :rkv_w], w_in[:, RW_IN:], w_in[:, rkv_w:RW_IN],
                            jnp.zeros((D_MODEL, RW_TAIL_PAD - RW_TAIL), w_in.dtype)], axis=1).astype(BF16)
    p = _matmul(a_all, w_in, out_dtype=F32, tm=1024, tn=768, tk=D_MODEL, name="even_w_in")
    rw = _rwkv_mixer(p, rkv_w + att_w, *rw_args)
    assert rkv_w % att_w == 0
    qkv = _qkv_prep(p, att_w, rkv_w // att_w, rope, jnp.stack([q_g, k_g]),
                    ((0, GQ_HEADS, 0, True, GQ_SCALE * LOG2E), (GQ_HEADS, GQ_KV_HEADS, 1, True, 1.0),
                     (GQ_HEADS + GQ_KV_HEADS, GQ_KV_HEADS, None, False, 1.0)), GQ_HEAD // 4, "gqa_prep")
    at = _gqa_attention(qkv, latent=True, tq=512)
    if need_ctx:
        at = jnp.concatenate([at, _gqa_attention(qkv, latent=False)], axis=0)
    return rw, at


def _odd_mixer(a_all, rope, w_in, q_norm, q_up, kv_norm, kv_up, lq1, lk1, lq2, lk2, subln, layer_idx):
    w_in = jnp.concatenate([w_in[:, MLA_IN:], w_in[:, :MLA_IN],
                            jnp.zeros((D_MODEL, IN_ODD_PADDED - IN_ODD), w_in.dtype)], axis=1).astype(BF16)
    p = _matmul(a_all, w_in, out_dtype=F32, tm=1024, tn=768, tk=D_MODEL, name="odd_w_in")
    lam_init = 0.8 - 0.6 * math.exp(-0.3 * layer_idx)
    lam = (jnp.exp(jnp.sum(lq1 * lk1).astype(F32)) - jnp.exp(jnp.sum(lq2 * lk2).astype(F32)) + lam_init)

    pm = p[:, DIFF_IN:DIFF_IN + MLA_IN]
    c_q = _rms_f32(pm[:N_LAT_ROWS, :Q_LORA], q_norm).astype(BF16)
    c_kv = _rms_f32(pm[:, Q_LORA:Q_LORA + KV_LORA], kv_norm).astype(BF16)
    k_pe = pm[:, Q_LORA + KV_LORA:MLA_IN]
    q_up_h = q_up.reshape(Q_LORA, MLA_HEADS, MLA_NOPE + MLA_ROPE)
    q_up_r = jnp.concatenate([q_up_h[:, :, :MLA_NOPE].reshape(Q_LORA, -1),
                              q_up_h[:, :, MLA_NOPE:].reshape(Q_LORA, -1)], axis=1).astype(BF16)
    kv_up_h = kv_up.reshape(KV_LORA, MLA_HEADS, MLA_NOPE + MLA_V)
    kv_up_r = jnp.concatenate([kv_up_h[:, :, :MLA_NOPE].reshape(KV_LORA, -1),
                               kv_up_h[:, :, MLA_NOPE:].reshape(KV_LORA, -1)], axis=1).astype(BF16)
    q_all = _matmul(c_q, q_up_r, out_dtype=F32, tm=1024, tn=1536, tk=Q_LORA, name="mla_q_up")
    kv = _matmul(c_kv, kv_up_r, out_dtype=BF16, tm=1024, tn=2048, tk=KV_LORA, name="mla_kv_up")
    q_all = q_all * (MLA_SCALE * LOG2E)
    q_nope = q_all[:, :MLA_HEADS * MLA_NOPE].astype(BF16)
    q_pe = q_all[:, MLA_HEADS * MLA_NOPE:].reshape(N_LAT_ROWS, MLA_HEADS, MLA_ROPE)
    q_pe = _rope_lat(q_pe, rope).transpose(1, 0, 2).astype(BF16)
    q_pe = jnp.pad(q_pe, ((0, 0), (0, 0), (0, MLA_NOPE - MLA_ROPE)))
    k_pe = _rope_lat(k_pe.reshape(N_ROWS, 1, MLA_ROPE), rope).reshape(N_ROWS, MLA_ROPE).astype(BF16)
    k_pe = jnp.pad(k_pe, ((0, 0), (0, MLA_NOPE - MLA_ROPE)))
    m_out = _mla_attention(q_nope, q_pe, kv, k_pe, tq=512)

    dqkv = _qkv_prep(p, DIFF_IN, 0, rope, None,
                     ((0, DIFF_HEADS, None, True, DIFF_SCALE * LOG2E), (DIFF_HEADS, DIFF_HEADS, None, True, 1.0),
                      (2 * DIFF_HEADS, DIFF_HEADS, None, False, 1.0)), DIFF_HEAD // 4, "diff_prep")
    d_out = _diff_attention(lam.reshape(1), dqkv, subln, 1.0 - lam_init, tq=512)
    return m_out, d_out


def kernel(x, c, ctx, c_ctx, ada_w, ada_b, norm1_g, norm2_g, mlp_w1, mlp_w2, final_g, ev_w_in, ev_w_out, rw_mu, rw_w0, rw_w_up, rw_a0, rw_a_up, rw_g_up, rw_k_k, rw_k_a, rw_r_k, rw_ln_w, rw_ln_b, gq_q_norm, gq_k_norm, od_w_in, od_w_out, mla_q_norm, mla_q_up, mla_kv_norm, mla_kv_up, diff_lq1, diff_lk1, diff_lq2, diff_lk2, diff_subln):
    rope_gq = _axial_rope_tables(SEQ, GQ_HEAD)
    rope_64 = _axial_rope_tables(SEQ, MLA_ROPE)
    h = jnp.concatenate([x.reshape(N_LAT_ROWS, D_MODEL), ctx.reshape(N_CTX_ROWS, D_MODEL)], axis=0)
    cond = jnp.concatenate([c_ctx[None], c, jnp.zeros((3, D_MODEL), F32)], axis=0)
    cond = jax.nn.silu(cond).astype(BF16)
    for i in range(DEPTH):
        last = i == DEPTH - 1
        mod = _matmul(cond, ada_w, layer=i, out_dtype=F32, tm=8, tn=2048, tk=1024, name="adaln_mod")
        mod = (mod + ada_b[i])[:BATCH + 1].reshape(BATCH + 1, N_MOD, 1, D_MODEL)
        sh1, sc1, g1, sh2, sc2, g2 = (mod[:, m] for m in range(N_MOD))
        a_all = _norm(h, norm1_g[i], sc1, sh1, rows=N_ROWS, out_dtype=BF16)
        j = i // 2
        if i % 2 == 0:
            rw_args = (rw_mu[j], rw_w0[j], rw_w_up[j], rw_a0[j], rw_a_up[j], rw_g_up[j], rw_k_k[j], rw_k_a[j],
                       rw_r_k[j], rw_ln_w[j], rw_ln_b[j])
            mix = _even_mixer(a_all, rope_gq, ev_w_in[j], rw_args, gq_q_norm[j], gq_k_norm[j], not last)
            w_out = ev_w_out[j]
        else:
            if not last:
                raise NotImplementedError("context rows of an odd layer are only needed when a layer follows")
            mix = _odd_mixer(a_all, rope_64, od_w_in[j], mla_q_norm[j], mla_q_up[j], mla_kv_norm[j],
                             mla_kv_up[j], diff_lq1[j], diff_lk1[j], diff_lq2[j], diff_lk2[j], diff_subln[j], i)
            w_out = od_w_out[j]
        rows = N_LAT_ROWS if last else N_ROWS
        h = _matmul(mix, w_out.astype(BF16), out_dtype=F32, tm=1024, tn=1024, tk=2048, rows=rows,
                    epilogue="gated_residual", res=h, gate=g1, name="mixer_w_out")
        a2 = _norm(h, norm2_g[i], sc2, sh2, rows=rows, out_dtype=BF16)
        hid = _matmul(a2, mlp_w1, layer=i, out_dtype=BF16, tm=1024, tn=512, tk=D_MODEL,
                      epilogue="relu2", name="mlp_w1")
        h = _matmul(hid, mlp_w2, layer=i, out_dtype=F32, tm=1024, tn=1024, tk=2048,
                    epilogue="gated_residual", res=h, gate=g2, name="mlp_w2")
    out = _norm(h, final_g, rows=N_LAT_ROWS, out_dtype=F32)
    return out.reshape(BATCH, SEQ, D_MODEL)
```

```python
import functools
import math

import jax
import jax.numpy as jnp
from jax import lax
from jax.experimental import pallas as pl
from jax.experimental.pallas import tpu as pltpu

D_MODEL = 4096
BATCH = 4
SEQ = 4096
DEPTH = 2
CTX_LEN = 256
GRID_W = 64
ROPE_THETA = 10000.0
NORM_EPS = 1e-6
MLP_HIDDEN = 4 * D_MODEL
N_MOD = 6
HALF = D_MODEL // 2

RW_HEAD = 64
RW_W = HALF
RW_HEADS = RW_W // RW_HEAD
DECAY_RANK = 96
AAA_RANK = 96
GATE_RANK = 256
GN_EPS = 64e-5
RW_IN = 3 * RW_W + GATE_RANK + 2 * DECAY_RANK + 2 * AAA_RANK

GQ_HEAD = 128
GQ_HEADS = HALF // GQ_HEAD
GQ_KV_HEADS = GQ_HEADS // 4
GQ_GROUP = GQ_HEADS // GQ_KV_HEADS
GQ_SCALE = GQ_HEAD ** -0.5
IN_EVEN = RW_IN + (GQ_HEADS + 2 * GQ_KV_HEADS) * GQ_HEAD

MLA_NOPE = 128
MLA_ROPE = 64
MLA_V = 128
MLA_HEADS = HALF // MLA_V
Q_LORA = 768
KV_LORA = 512
MLA_SCALE = (MLA_NOPE + MLA_ROPE) ** -0.5
MLA_IN = Q_LORA + KV_LORA + MLA_ROPE

DIFF_HEAD = 64
DIFF_V = 2 * DIFF_HEAD
DIFF_HEADS = HALF // DIFF_V
DIFF_SCALE = DIFF_HEAD ** -0.5
DIFF_IN = DIFF_HEADS * (4 * DIFF_HEAD + DIFF_V)
IN_ODD = MLA_IN + DIFF_IN
IN_ODD_PADDED = 7680

N_LAT_ROWS = BATCH * SEQ
N_CTX_ROWS = BATCH * CTX_LEN
N_ROWS = N_LAT_ROWS + N_CTX_ROWS
T_ALL = CTX_LEN + SEQ
CTX_BLOCK0 = N_LAT_ROWS // CTX_LEN

V7X_VMEM_LIMIT_BYTES = 56 * 1024 * 1024

BF16 = jnp.bfloat16
F32 = jnp.float32


def _params(*sem):
    return pltpu.CompilerParams(dimension_semantics=sem, vmem_limit_bytes=V7X_VMEM_LIMIT_BYTES)


def _row_group(row_tile, tile_rows):
    start = row_tile * tile_rows
    return jnp.where(start < N_LAT_ROWS, 1 + start // SEQ, 0)


def _norm_kernel(x_ref, g_ref, *rest, modulate):
    if modulate:
        sc_ref, sh_ref, o_ref = rest
    else:
        (o_ref,) = rest
    x = x_ref[...]
    ms = jnp.mean(x * x, axis=-1, keepdims=True)
    y = x * lax.rsqrt(ms + NORM_EPS) * g_ref[...]
    if modulate:
        y = y * (1.0 + sc_ref[0]) + sh_ref[0]
    o_ref[...] = y.astype(o_ref.dtype)


def _norm(x, g, sc=None, sh=None, *, rows, out_dtype, tile=256):
    d = x.shape[1]
    modulate = sc is not None
    in_specs = [pl.BlockSpec((tile, d), lambda i: (i, 0)),
                pl.BlockSpec((1, d), lambda i: (0, 0))]
    args = [x, g.reshape(1, d)]
    if modulate:
        mod_spec = pl.BlockSpec((1, 1, d), lambda i: (_row_group(i, tile), 0, 0))
        in_specs += [mod_spec, mod_spec]
        args += [sc, sh]
    return pl.pallas_call(
        functools.partial(_norm_kernel, modulate=modulate),
        grid=(rows // tile,),
        in_specs=in_specs,
        out_specs=pl.BlockSpec((tile, d), lambda i: (i, 0)),
        out_shape=jax.ShapeDtypeStruct((rows, d), out_dtype),
        compiler_params=_params("parallel"),
        name="rmsnorm_mod" if modulate else "rmsnorm",
    )(*args)


def _mm_kernel(*refs, epilogue, nk, n_a):
    a_refs, w_ref, rest = refs[:n_a], refs[n_a], refs[n_a + 1:]
    if epilogue == "gated_residual":
        res_ref, gate_ref, o_ref, acc_ref = rest
    else:
        o_ref, acc_ref = rest
    k = pl.program_id(2)

    def product(a_ref):
        return jnp.dot(a_ref[...].astype(BF16), w_ref[...].astype(BF16), preferred_element_type=F32)

    def store(part, first):
        if first:
            acc_ref[...] = part
        else:
            acc_ref[...] += part

    if n_a == 1:
        part = product(a_refs[0])
        pl.when(k == 0)(lambda: store(part, True))
        pl.when(k > 0)(lambda: store(part, False))
    else:
        for idx, a_ref in enumerate(a_refs):
            pl.when(k == idx)(lambda a_ref=a_ref, idx=idx: store(product(a_ref), idx == 0))

    @pl.when(k == nk - 1)
    def _():
        acc = acc_ref[...]
        if epilogue == "relu2":
            acc = jnp.square(jnp.maximum(acc, 0.0))
        elif epilogue == "gated_residual":
            acc = res_ref[...] + gate_ref[0] * acc
        o_ref[...] = acc.astype(o_ref.dtype)


def _matmul(a, w, *, out_dtype, tm, tn, tk, rows=None, layer=None, epilogue=None, res=None, gate=None, name):
    kdim, n = w.shape[-2:]
    a_parts = a if isinstance(a, tuple) else (a,)
    m = a_parts[0].shape[0] if rows is None else rows
    assert sum(part.shape[1] for part in a_parts) == kdim and m % tm == 0 and n % tn == 0 and kdim % tk == 0
    assert (w.ndim == 3) == (layer is not None)
    nk = kdim // tk
    w_spec = (pl.BlockSpec((tk, tn), lambda i, j, k: (k, j)) if layer is None else
              pl.BlockSpec((None, tk, tn), lambda i, j, k: (layer, k, j)))
    if len(a_parts) == 1:
        in_specs = [pl.BlockSpec((tm, tk), lambda i, j, k: (i, k)), w_spec]
    else:
        assert len(a_parts) == nk and all(part.shape[1] == tk for part in a_parts)
        in_specs = [pl.BlockSpec((tm, tk), lambda i, j, k: (i, 0))] * nk + [w_spec]
    args = [*a_parts, w]
    if epilogue == "gated_residual":
        in_specs += [pl.BlockSpec((tm, tn), lambda i, j, k: (i, j)),
                     pl.BlockSpec((1, 1, tn), lambda i, j, k: (_row_group(i, tm), 0, j))]
        args += [res, gate]
    return pl.pallas_call(
        functools.partial(_mm_kernel, epilogue=epilogue, nk=nk, n_a=len(a_parts)),
        grid=(m // tm, n // tn, nk),
        in_specs=in_specs,
        out_specs=pl.BlockSpec((tm, tn), lambda i, j, k: (i, j)),
        out_shape=jax.ShapeDtypeStruct((m, n), out_dtype),
        scratch_shapes=[pltpu.VMEM((tm, tn), F32)],
        compiler_params=_params("parallel", "parallel", "arbitrary"),
        name=name,
    )(*args)


def _nt_dot(a, b):
    return lax.dot_general(a, b, (((1,), (1,)), ((), ())), preferred_element_type=F32)


LOG2E = math.log2(math.e)


def _fill_values_and_ones(v_ref, vo_ref):
    d = v_ref.shape[1]
    vo_ref[:, :d] = v_ref[...]
    vo_ref[:, d:] = jnp.ones_like(v_ref)


def _softmax_times_values(s_list, vo_refs):
    d = vo_refs[0].shape[1] // 2
    m = functools.reduce(jnp.maximum, [jnp.max(s, axis=-1, keepdims=True) for s in s_list])
    acc = functools.reduce(jnp.add, [jnp.dot(jnp.exp2(s - m).astype(BF16), vo_ref[...], preferred_element_type=F32)
                                     for s, vo_ref in zip(s_list, vo_refs)])
    return acc[:, :d] / acc[:, d:]


def _attend_each(items):
    outs = []
    pending = None
    for q, k_refs, vo_refs in items:
        s_list = [_nt_dot(q, k_ref[...]) for k_ref in k_refs]
        if pending is not None:
            outs.append(_softmax_times_values(*pending))
        pending = (s_list, vo_refs)
    outs.append(_softmax_times_values(*pending))
    return outs


def _gqa_kernel(q_ref, *rest, with_lat):
    if with_lat:
        kc_ref, vc_ref, kl_ref, vl_ref, o_ref, voc_ref, vol_ref = rest
        k_refs, v_refs, vo_refs = (kc_ref, kl_ref), (vc_ref, vl_ref), (voc_ref, vol_ref)
    else:
        kc_ref, vc_ref, o_ref, voc_ref = rest
        k_refs, v_refs, vo_refs = (kc_ref,), (vc_ref,), (voc_ref,)

    @pl.when(pl.program_id(2) == 0)
    def _():
        for v_ref, vo_ref in zip(v_refs, vo_refs):
            _fill_values_and_ones(v_ref, vo_ref)

    outs = _attend_each([(q_ref[:, g * GQ_HEAD:(g + 1) * GQ_HEAD], k_refs, vo_refs) for g in range(GQ_GROUP)])
    for g, o in enumerate(outs):
        o_ref[:, g * GQ_HEAD:(g + 1) * GQ_HEAD] = o.astype(o_ref.dtype)


def _gqa_attention(qkv, *, latent, tq=256):
    gw = GQ_GROUP * GQ_HEAD
    k0, v0 = GQ_HEADS, GQ_HEADS + GQ_KV_HEADS
    ctx_k = pl.BlockSpec((CTX_LEN, GQ_HEAD), lambda b, n, i: (CTX_BLOCK0 + b, k0 + n))
    ctx_v = pl.BlockSpec((CTX_LEN, GQ_HEAD), lambda b, n, i: (CTX_BLOCK0 + b, v0 + n))
    scratch = [pltpu.VMEM((CTX_LEN, 2 * GQ_HEAD), BF16)]
    if latent:
        per_b = SEQ // tq
        lat_k = pl.BlockSpec((SEQ, GQ_HEAD), lambda b, n, i: (b, k0 + n))
        lat_v = pl.BlockSpec((SEQ, GQ_HEAD), lambda b, n, i: (b, v0 + n))
        in_specs = [pl.BlockSpec((tq, gw), lambda b, n, i: (b * per_b + i, n)), ctx_k, ctx_v, lat_k, lat_v]
        args = (qkv,) * 5
        rows = N_LAT_ROWS
        scratch.append(pltpu.VMEM((SEQ, 2 * GQ_HEAD), BF16))
    else:
        per_b = CTX_LEN // tq
        q_blk0 = N_LAT_ROWS // tq
        in_specs = [pl.BlockSpec((tq, gw), lambda b, n, i: (q_blk0 + b * per_b + i, n)), ctx_k, ctx_v]
        args = (qkv,) * 3
        rows = N_CTX_ROWS
    return pl.pallas_call(
        functools.partial(_gqa_kernel, with_lat=latent),
        grid=(BATCH, GQ_KV_HEADS, per_b),
        in_specs=in_specs,
        out_specs=pl.BlockSpec((tq, gw), lambda b, n, i: (b * per_b + i, n)),
        out_shape=jax.ShapeDtypeStruct((rows, GQ_HEADS * GQ_HEAD), BF16),
        scratch_shapes=scratch,
        compiler_params=_params("parallel", "parallel", "arbitrary"),
        name="gqa_attention_lat" if latent else "gqa_attention_ctx",
    )(*args)


def _mla_kernel(qn_ref, qp_ref, knc_ref, kpc_ref, vc_ref, knl_ref, kpl_ref, vl_ref, o_ref,
                kc_ref, kl_ref, voc_ref, vol_ref):
    heads = range(MLA_HEADS_PER_STEP)
    cols = [slice(j * MLA_NOPE, (j + 1) * MLA_NOPE) for j in heads]

    def own_rope_lanes(x, j):
        lane = lax.broadcasted_iota(jnp.int32, x.shape, 1)
        return jnp.where(lane // MLA_ROPE == j, x, jnp.zeros_like(x))

    @pl.when(pl.program_id(2) == 0)
    def _():
        for kn_ref, kp_ref, v_ref, k_ref, vo_ref in ((knc_ref, kpc_ref, vc_ref, kc_ref, voc_ref),
                                                     (knl_ref, kpl_ref, vl_ref, kl_ref, vol_ref)):
            for j in heads:
                k_ref[j, :, :MLA_NOPE] = kn_ref[:, cols[j]]
                k_ref[j, :, MLA_NOPE:] = own_rope_lanes(kp_ref[...], j)
                vo_ref[j, :, :MLA_V] = v_ref[:, cols[j]]
                vo_ref[j, :, MLA_V:] = jnp.ones_like(kp_ref)

    qp = qp_ref[...]
    items = [(jnp.concatenate([qn_ref[:, cols[j]], own_rope_lanes(qp, j)], axis=1), (kc_ref.at[j], kl_ref.at[j]),
              (voc_ref.at[j], vol_ref.at[j])) for j in heads]
    for j, o in enumerate(_attend_each(items)):
        o_ref[:, cols[j]] = o.astype(o_ref.dtype)


MLA_HEADS_PER_STEP = 2


def _mla_attention(q_nope, q_pe, kv, k_pe, *, tq=256):
    per_b = SEQ // tq
    hs = MLA_HEADS_PER_STEP
    assert hs * MLA_ROPE == MLA_NOPE
    w = hs * MLA_NOPE
    v_blk0 = MLA_HEADS // hs
    in_specs = [
        pl.BlockSpec((tq, w), lambda b, h, i: (b * per_b + i, h)),
        pl.BlockSpec((tq, hs * MLA_ROPE), lambda b, h, i: (b * per_b + i, h)),
        pl.BlockSpec((CTX_LEN, w), lambda b, h, i: (CTX_BLOCK0 + b, h)),
        pl.BlockSpec((CTX_LEN, MLA_NOPE), lambda b, h, i: (CTX_BLOCK0 + b, 0)),
        pl.BlockSpec((CTX_LEN, w), lambda b, h, i: (CTX_BLOCK0 + b, v_blk0 + h)),
        pl.BlockSpec((SEQ, w), lambda b, h, i: (b, h)),
        pl.BlockSpec((SEQ, MLA_NOPE), lambda b, h, i: (b, 0)),
        pl.BlockSpec((SEQ, w), lambda b, h, i: (b, v_blk0 + h)),
    ]
    return pl.pallas_call(
        _mla_kernel,
        grid=(BATCH, MLA_HEADS // hs, per_b),
        in_specs=in_specs,
        out_specs=pl.BlockSpec((tq, w), lambda b, h, i: (b * per_b + i, h)),
        out_shape=jax.ShapeDtypeStruct((N_LAT_ROWS, MLA_HEADS * MLA_V), BF16),
        scratch_shapes=[pltpu.VMEM((hs, CTX_LEN, 2 * MLA_NOPE), BF16), pltpu.VMEM((hs, SEQ, 2 * MLA_NOPE), BF16),
                        pltpu.VMEM((hs, CTX_LEN, 2 * MLA_V), BF16), pltpu.VMEM((hs, SEQ, 2 * MLA_V), BF16)],
        compiler_params=_params("parallel", "parallel", "arbitrary"),
        name="mla_attention",
    )(q_nope, q_pe, kv, k_pe, kv, kv, k_pe, kv)


def _diff_kernel(lam_ref, q_ref, kc_ref, vc_ref, kl_ref, vl_ref, g_ref, o_ref, voc_ref, vol_ref, *, out_scale):
    heads = range(DIFF_HEADS_PER_STEP)
    cols = [slice(j * DIFF_V, (j + 1) * DIFF_V) for j in heads]

    @pl.when(pl.program_id(2) == 0)
    def _():
        for v_ref, vo_ref in ((vc_ref, voc_ref), (vl_ref, vol_ref)):
            for j in heads:
                vo_ref[j, :, :DIFF_V] = v_ref[:, cols[j]]
                vo_ref[j, :, DIFF_V:] = jnp.ones((v_ref.shape[0], DIFF_V), BF16)

    first = lax.broadcasted_iota(jnp.int32, (q_ref.shape[0], DIFF_V), 1) < DIFF_HEAD
    items = []
    for j in heads:
        q = q_ref[:, cols[j]]
        zero = jnp.zeros_like(q)
        keys = (kc_ref[:, cols[j]], kl_ref[:, cols[j]])
        for qh in (jnp.where(first, q, zero), jnp.where(first, zero, q)):
            items.append((qh, keys, (voc_ref.at[j], vol_ref.at[j])))
    outs = _attend_each(items)
    for j in heads:
        o = outs[2 * j] - lam_ref[0] * outs[2 * j + 1]
        ms = jnp.mean(o * o, axis=-1, keepdims=True)
        o = o * lax.rsqrt(ms + NORM_EPS) * g_ref[...]
        o_ref[:, cols[j]] = (o * out_scale).astype(o_ref.dtype)


DIFF_HEADS_PER_STEP = 2


def _diff_attention(lam, qkv, subln, out_scale, *, tq=256):
    per_b = SEQ // tq
    hs = DIFF_HEADS_PER_STEP
    w = hs * DIFF_V
    k0, v0 = DIFF_HEADS // hs, 2 * DIFF_HEADS // hs
    ctx_k = pl.BlockSpec((CTX_LEN, w), lambda b, h, i: (CTX_BLOCK0 + b, k0 + h))
    ctx_v = pl.BlockSpec((CTX_LEN, w), lambda b, h, i: (CTX_BLOCK0 + b, v0 + h))
    lat_k = pl.BlockSpec((SEQ, w), lambda b, h, i: (b, k0 + h))
    lat_v = pl.BlockSpec((SEQ, w), lambda b, h, i: (b, v0 + h))
    in_specs = [
        pl.BlockSpec(memory_space=pltpu.SMEM),
        pl.BlockSpec((tq, w), lambda b, h, i: (b * per_b + i, h)),
        ctx_k, ctx_v, lat_k, lat_v,
        pl.BlockSpec((1, DIFF_V), lambda b, h, i: (0, 0)),
    ]
    return pl.pallas_call(
        functools.partial(_diff_kernel, out_scale=out_scale),
        grid=(BATCH, DIFF_HEADS // hs, per_b),
        in_specs=in_specs,
        out_specs=pl.BlockSpec((tq, w), lambda b, h, i: (b * per_b + i, h)),
        out_shape=jax.ShapeDtypeStruct((N_LAT_ROWS, DIFF_HEADS * DIFF_V), BF16),
        scratch_shapes=[pltpu.VMEM((hs, CTX_LEN, 2 * DIFF_V), BF16), pltpu.VMEM((hs, SEQ, 2 * DIFF_V), BF16)],
        compiler_params=_params("parallel", "parallel", "arbitrary"),
        name="diff_attention",
    )(lam, qkv, qkv, qkv, qkv, qkv, subln.reshape(1, DIFF_V))


RW_CHUNK = 128
RW_PAIRS = RW_HEADS // 2
LANES = 2 * RW_HEAD
SUB = 8
RW_UNROLL = 16
RW_GROUP = 8


def _rwkv_kernel(r_in, kk_in, v_in, w_in, k_in, b_in, y_out, s_ref, r_ref, kk_ref, v_ref, w_ref, k_ref, b_ref,
                 y_ref):
    d = pl.program_id(0)
    c = pl.program_id(2)

    @pl.when(c == 0)
    def _():
        s_ref[...] = jnp.zeros_like(s_ref)

    for p in range(RW_PAIRS):
        cols = slice(p * LANES, (p + 1) * LANES)
        for src, dst in ((r_in, r_ref), (kk_in, kk_ref), (v_in, v_ref)):
            dst[p] = src[:, cols]
        for src, dst in ((w_in, w_ref), (k_in, k_ref), (b_in, b_ref)):
            dst[0, p] = src[0, :, cols]

    v_hi = lax.broadcasted_iota(jnp.int32, (SUB, SUB, LANES), 0)
    v_lo = lax.broadcasted_iota(jnp.int32, (SUB, SUB, LANES), 1)
    lane3 = lax.broadcasted_iota(jnp.int32, (SUB, SUB, LANES), 2)
    dup = jnp.where((lane3 & (RW_HEAD - 1)) == v_hi * SUB + v_lo, 1.0, 0.0)
    blk_r = lax.broadcasted_iota(jnp.int32, (LANES, LANES), 0) // RW_HEAD
    blk_c = lax.broadcasted_iota(jnp.int32, (LANES, LANES), 1) // RW_HEAD
    head_ones = jnp.where(blk_r == blk_c, 1.0, 0.0).astype(BF16)
    slot_of_lane = lax.broadcasted_iota(jnp.int32, (SUB, LANES), 1) & (RW_HEAD - 1)
    step_dir = jnp.where(d == 0, 1, -1)

    y_ref[...] = jnp.zeros_like(y_ref)

    def row(ref, p, t, *lead):
        return ref[(*lead, p, pl.ds(t, SUB, stride=0), slice(None))]

    def as_rows(tile):
        return tile.reshape(RW_HEAD, LANES).astype(BF16)

    def head_sums(states, pairs, t_done, t_next):
        rows = []
        for p in pairs:
            rows.append(as_rows(states[p] * row(r_ref, p, t_done)))
            if t_next is not None:
                rows.append(as_rows(states[p] * row(kk_ref, p, t_next)))
                rows.append(as_rows(dup * row(v_ref, p, t_next)))
        return jnp.dot(jnp.concatenate(rows, axis=0), head_ones, preferred_element_type=F32)

    def slab(sums, q, n_slabs, i):
        lo = (q * n_slabs + i) * RW_HEAD
        return sums[lo:lo + RW_HEAD].reshape(SUB, SUB, LANES)

    def put_y(p, y_b, t, valid):
        window = t // RW_HEAD
        keep = (slot_of_lane == t % RW_HEAD) & valid
        y_ref[0, p, window] = jnp.where(keep, y_b, y_ref[0, p, window])

    groups = [range(g0, g0 + RW_GROUP) for g0 in range(0, RW_PAIRS, RW_GROUP)]

    def block(g, carry):
        first = g * RW_UNROLL
        first = jnp.where(d == 0, first, RW_CHUNK - 1 - first)
        times = [first + u * step_dir for u in range(RW_UNROLL)]
        t_before = jnp.clip(first - step_dir, 0, RW_CHUNK - 1)
        states = [s_ref[p] for p in range(RW_PAIRS)]
        sums = [head_sums(states, grp, t_before, times[0]) for grp in groups]
        for gi, grp in enumerate(groups):
            for q, p in enumerate(grp):
                put_y(p, slab(sums[gi], q, 3, 0), t_before, g > 0)
        for u in range(RW_UNROLL):
            t = times[u]
            for gi, grp in enumerate(groups):
                for q, p in enumerate(grp):
                    sa = slab(sums[gi], q, 3, 1)
                    vb = slab(sums[gi], q, 3, 2)
                    states[p] = (states[p] * row(w_ref, p, t, 0) - sa * row(b_ref, p, t, 0)
                                 + vb * row(k_ref, p, t, 0))
                if u + 1 < RW_UNROLL:
                    sums[gi] = head_sums(states, grp, t, times[u + 1])
                    for q, p in enumerate(grp):
                        put_y(p, slab(sums[gi], q, 3, 0), t, True)
        for p in range(RW_PAIRS):
            s_ref[p] = states[p]
        return carry

    lax.fori_loop(0, RW_CHUNK // RW_UNROLL, block, 0)

    t_last = jnp.where(d == 0, RW_CHUNK - 1, 0)
    states = [s_ref[p] for p in range(RW_PAIRS)]
    for grp in groups:
        last = head_sums(states, grp, t_last, None)
        for q, p in enumerate(grp):
            put_y(p, slab(last, q, 1, 0), t_last, True)

    first_head = lax.broadcasted_iota(jnp.int32, (RW_HEAD, LANES), 1) < RW_HEAD
    for p in range(0, RW_PAIRS, 2):
        for window in range(RW_CHUNK // RW_HEAD):
            both = jnp.concatenate([y_ref[0, p, window].reshape(RW_HEAD, LANES),
                                    y_ref[0, p + 1, window].reshape(RW_HEAD, LANES)], axis=0)
            by_row = both.T
            head0, head1 = by_row[:RW_HEAD], by_row[RW_HEAD:]
            rows = slice(window * RW_HEAD, (window + 1) * RW_HEAD)
            y_out[0, rows, p * LANES:(p + 1) * LANES] = jnp.where(
                first_head, head0, pltpu.roll(head1, RW_HEAD, axis=1))
            y_out[0, rows, (p + 1) * LANES:(p + 2) * LANES] = jnp.where(
                first_head, pltpu.roll(head0, RW_HEAD, axis=1), head1)


def _rwkv_scan(r, kk, v, w, k, b):
    n_ctx = CTX_LEN // RW_CHUNK
    n_lat = SEQ // RW_CHUNK
    ctx0 = N_LAT_ROWS // RW_CHUNK

    def chunk(d, bb, c):
        j = jnp.where(d == 0, c, jnp.where(c < n_ctx, n_ctx - 1 - c, n_ctx + n_lat - 1 - (c - n_ctx)))
        return jnp.where(j < n_ctx, ctx0 + bb * n_ctx + j, bb * n_lat + j - n_ctx)

    shared = pl.BlockSpec((RW_CHUNK, RW_W), lambda d, bb, c: (chunk(d, bb, c), 0))
    per_dir = pl.BlockSpec((1, RW_CHUNK, RW_W), lambda d, bb, c: (d, chunk(d, bb, c), 0))
    staged = pltpu.VMEM((RW_PAIRS, RW_CHUNK, LANES), F32)
    staged_dir = pltpu.VMEM((1, RW_PAIRS, RW_CHUNK, LANES), F32)
    return pl.pallas_call(
        _rwkv_kernel,
        grid=(2, BATCH, n_ctx + n_lat),
        in_specs=[shared, shared, shared, per_dir, per_dir, per_dir],
        out_specs=per_dir,
        out_shape=jax.ShapeDtypeStruct((2, N_ROWS, RW_W), F32),
        scratch_shapes=[pltpu.VMEM((RW_PAIRS, RW_HEAD // SUB, SUB, LANES), F32),
                        staged, staged, staged, staged_dir, staged_dir, staged_dir,
                        pltpu.VMEM((1, RW_PAIRS, RW_CHUNK // RW_HEAD, SUB, SUB, LANES), F32)],
        compiler_params=_params("parallel", "parallel", "arbitrary"),
        name="rwkv7_scan",
    )(r, kk, v, w, k, b)


def _axial_rope_tables(n_tokens, rot_dim):
    n_rows = n_tokens // GRID_W
    row = jnp.repeat(jnp.arange(n_rows, dtype=F32), GRID_W)
    col = jnp.tile(jnp.arange(GRID_W, dtype=F32), n_rows)
    axis_dim = rot_dim // 2
    inv_freq = ROPE_THETA ** (-jnp.arange(0, axis_dim, 2, dtype=F32) / axis_dim)
    ang_r = row[:, None] * inv_freq
    ang_c = col[:, None] * inv_freq
    ang = jnp.concatenate([ang_r, ang_r, ang_c, ang_c], axis=-1)
    return jnp.cos(ang), jnp.sin(ang)


def _rotate_half(z):
    z1, z2 = jnp.split(z, 2, axis=-1)
    return jnp.concatenate([-z2, z1], axis=-1)


def _apply_rope(x, cos, sin):
    half = x.shape[-1] // 2
    rot = jnp.concatenate([_rotate_half(x[..., :half]), _rotate_half(x[..., half:])], axis=-1)
    return x * cos + rot * sin


def _rms_f32(x, g):
    return x * lax.rsqrt(jnp.mean(x * x, axis=-1, keepdims=True) + NORM_EPS) * g


def _rope_lat(x, rope):
    cos, sin = rope
    heads, dim = x.shape[1:]
    lat = x[:N_LAT_ROWS].reshape(BATCH, SEQ, heads, dim)
    lat = _apply_rope(lat, cos[None, :, None, :], sin[None, :, None, :]).reshape(N_LAT_ROWS, heads, dim)
    return lat if x.shape[0] == N_LAT_ROWS else jnp.concatenate([lat, x[N_LAT_ROWS:]], axis=0)


RW_SLAB = 512
RW_TAIL = GATE_RANK + 2 * DECAY_RANK + 2 * AAA_RANK
RW_TAIL_PAD = 768
RW_ROW_TILE = 256


def _head_lane_sums(x):
    rows = x.shape[0]
    blk_r = lax.broadcasted_iota(jnp.int32, (LANES, LANES), 0) // RW_HEAD
    blk_c = lax.broadcasted_iota(jnp.int32, (LANES, LANES), 1) // RW_HEAD
    head_ones = jnp.where(blk_r == blk_c, 1.0, 0.0).astype(BF16)
    hi = x.astype(BF16)
    rest = x - hi.astype(F32)
    mid = rest.astype(BF16)
    lo = (rest - mid.astype(F32)).astype(BF16)
    pieces = jnp.concatenate([hi, mid, lo], axis=0)
    out = []
    for s in range(x.shape[1] // LANES):
        part = jnp.dot(pieces[:, s * LANES:(s + 1) * LANES], head_ones, preferred_element_type=F32)
        out.append(part[:rows] + part[rows:2 * rows] + part[2 * rows:])
    return jnp.concatenate(out, axis=1)


def _rwkv_prep_kernel(*refs):
    slabs = [refs[4 * n:4 * n + 4] for n in range(4)]
    (up_w_ref, w0_ref, a0_ref, kk_gain_ref, ka_ref, rk_ref,
     r_out, v_out, kk_out, decay_out, kdir_out, b_out, g_out, bonus_out) = refs[16:30]
    stage_refs = refs[30:34]
    tile = RW_ROW_TILE
    row0 = pl.program_id(0) * tile
    seq_len = jnp.where(row0 < N_LAT_ROWS, SEQ, CTX_LEN)
    starts_seq = row0 % seq_len == 0
    ends_seq = (row0 + tile) % seq_len == 0

    def shifted(main_ref, before_ref, after_ref, mu_ref, stage_ref):
        x = main_ref[...]
        stage_ref[SUB:SUB + tile, :] = x
        stage_ref[SUB - 1:SUB, :] = jnp.where(starts_seq, 0.0, before_ref[SUB - 1:SUB, :])
        stage_ref[SUB + tile:SUB + tile + 1, :] = jnp.where(ends_seq, 0.0, after_ref[0:1, :])
        around = 0.5 * (stage_ref[SUB - 1:SUB - 1 + tile, :] + stage_ref[SUB + 1:SUB + 1 + tile, :])
        return x + (around - x) * mu_ref[...]

    r, k, v, tail = (shifted(*slab, stage) for slab, stage in zip(slabs, stage_refs))
    r_out[...] = r
    v_out[...] = v
    s_w = RW_SLAB
    mid = tail[:, GATE_RANK:2 * GATE_RANK]
    mid_lane = lax.broadcasted_iota(jnp.int32, (1, GATE_RANK), 1)
    act = jnp.concatenate([jax.nn.sigmoid(tail[:, :GATE_RANK]),
                           jnp.where(mid_lane < 2 * DECAY_RANK, jnp.tanh(mid), mid),
                           tail[:, 2 * GATE_RANK:]], axis=1)
    up = jnp.dot(act.astype(BF16), up_w_ref[...], preferred_element_type=F32)
    g_out[...] = up[:, :s_w]
    kq = k * kk_gain_ref[...]
    kk = kq / jnp.maximum(jnp.sqrt(_head_lane_sums(kq * kq)), 1e-12)
    kk_out[...] = kk
    kdir_sum = jnp.zeros_like(k)
    for d in range(2):
        x = w0_ref[d] + up[:, (1 + d) * s_w:(2 + d) * s_w]
        decay_out[d] = jnp.exp(jax.nn.sigmoid(x) * -math.exp(-0.5))
        a = jax.nn.sigmoid(a0_ref[d] + up[:, (3 + d) * s_w:(4 + d) * s_w])
        kd = k * (1.0 + (a - 1.0) * ka_ref[...])
        kdir_out[d] = kd
        b_out[d] = kk * a
        kdir_sum = kdir_sum + kd
    bonus_out[...] = _head_lane_sums(r * kdir_sum * rk_ref[...]) * v


def _rwkv_post_kernel(y_ref, g_ref, bonus_ref, lnw_ref, lnb_ref, o_ref):
    y = y_ref[0] + y_ref[1]
    mean = _head_lane_sums(y) * (1.0 / RW_HEAD)
    cen = y - mean
    var = _head_lane_sums(cen * cen) * (1.0 / RW_HEAD)
    yn = cen * lax.rsqrt(var + GN_EPS) * lnw_ref[...] + lnb_ref[...]
    o_ref[...] = ((yn + bonus_ref[...]) * g_ref[...]).astype(o_ref.dtype)


def _rwkv_mixer(p, tail_col, mu, w0, w_up, a0, a_up, g_up, k_k, k_a, r_k, ln_w, ln_b):
    assert tail_col % RW_TAIL_PAD == 0 and p.shape[1] >= tail_col + RW_TAIL_PAD
    n_slab = RW_W // RW_SLAB
    up_w = jnp.zeros((RW_TAIL_PAD, 5, RW_W), F32)
    o = GATE_RANK
    up_w = up_w.at[:o, 0].set(g_up)
    for d in range(2):
        up_w = up_w.at[o + d * DECAY_RANK:o + (d + 1) * DECAY_RANK, 1 + d].set(w_up[d])
    o += 2 * DECAY_RANK
    for d in range(2):
        up_w = up_w.at[o + d * AAA_RANK:o + (d + 1) * AAA_RANK, 3 + d].set(a_up[d])
    up_w = up_w.reshape(RW_TAIL_PAD, 5, n_slab, RW_SLAB).transpose(2, 0, 1, 3)
    up_w = up_w.reshape(n_slab, RW_TAIL_PAD, 5 * RW_SLAB).astype(BF16)
    mu_main = mu[:3 * RW_W].reshape(1, 3 * RW_W)
    mu_tail = jnp.pad(mu[3 * RW_W:], (0, RW_TAIL_PAD - RW_TAIL)).reshape(1, RW_TAIL_PAD)

    tile = RW_ROW_TILE
    col_blocks = RW_W // RW_SLAB
    halo = tile // SUB
    last_halo = N_ROWS // SUB - 1

    def with_halo(width, col):
        return [pl.BlockSpec((tile, width), lambda i, j: (i, col(j))),
                pl.BlockSpec((SUB, width), lambda i, j: (jnp.maximum(i * halo - 1, 0), col(j))),
                pl.BlockSpec((SUB, width), lambda i, j: (jnp.minimum((i + 1) * halo, last_halo), col(j)))]

    in_specs, args = [], []
    for n in range(3):
        in_specs += with_halo(RW_SLAB, lambda j, n=n: n * col_blocks + j)
        in_specs.append(pl.BlockSpec((1, RW_SLAB), lambda i, j, n=n: (0, n * col_blocks + j)))
        args += [p, p, p, mu_main]
    tail_blk = tail_col // RW_TAIL_PAD
    in_specs += with_halo(RW_TAIL_PAD, lambda j: tail_blk)
    in_specs.append(pl.BlockSpec((1, RW_TAIL_PAD), lambda i, j: (0, 0)))
    args += [p, p, p, mu_tail]

    vec = pl.BlockSpec((1, RW_SLAB), lambda i, j: (0, j))
    vec2 = pl.BlockSpec((2, 1, RW_SLAB), lambda i, j: (0, 0, j))
    out1 = pl.BlockSpec((tile, RW_SLAB), lambda i, j: (i, j))
    out2 = pl.BlockSpec((2, tile, RW_SLAB), lambda i, j: (0, i, j))
    one = jax.ShapeDtypeStruct((N_ROWS, RW_W), F32)
    two = jax.ShapeDtypeStruct((2, N_ROWS, RW_W), F32)
    in_specs += [pl.BlockSpec((None, RW_TAIL_PAD, 5 * RW_SLAB), lambda i, j: (j, 0, 0)), vec2, vec2, vec, vec, vec]
    args += [up_w, w0.reshape(2, 1, RW_W), a0.reshape(2, 1, RW_W), k_k.reshape(1, RW_W), k_a.reshape(1, RW_W),
             r_k.reshape(1, RW_W)]
    stage = lambda width: pltpu.VMEM((tile + 2 * SUB, width), F32)
    r, v, kk, decay, k_dir, b_dir, g, bonus = pl.pallas_call(
        _rwkv_prep_kernel,
        grid=(N_ROWS // tile, n_slab),
        in_specs=in_specs,
        out_specs=[out1, out1, out1, out2, out2, out2, out1, out1],
        out_shape=[one, one, one, two, two, two, one, one],
        scratch_shapes=[stage(RW_SLAB), stage(RW_SLAB), stage(RW_SLAB), stage(RW_TAIL_PAD)],
        compiler_params=_params("parallel", "parallel"),
        name="rwkv7_prep",
    )(*args)

    y2 = _rwkv_scan(r, kk, v, decay, k_dir, b_dir)

    return pl.pallas_call(
        _rwkv_post_kernel,
        grid=(N_ROWS // tile, n_slab),
        in_specs=[out2, out1, out1, vec, vec],
        out_specs=out1,
        out_shape=jax.ShapeDtypeStruct((N_ROWS, RW_W), BF16),
        compiler_params=_params("parallel", "parallel"),
        name="rwkv7_post",
    )(y2, g, bonus, ln_w.reshape(1, RW_W), ln_b.reshape(1, RW_W))


def _qkv_prep_kernel(x_ref, cos_ref, sin_ref, *rest, segments, quarter):
    gain_ref, o_ref = rest if len(rest) == 2 else (None, rest[0])
    _qkv_prep_body(x_ref, cos_ref, sin_ref, gain_ref, o_ref, segments, quarter)


def _qkv_prep_body(x_ref, cos_ref, sin_ref, gain_ref, o_ref, segments, quarter):
    tile = x_ref.shape[0]
    latent = pl.program_id(0) * tile < N_LAT_ROWS
    cos = jnp.where(latent, cos_ref[...], 1.0)
    sin = jnp.where(latent, sin_ref[...], 0.0)
    lane = lax.broadcasted_iota(jnp.int32, (1, LANES), 1)
    even_quarter = (lane // quarter) % 2 == 0
    for first, count, gain_row, rotary, scale in segments:
        for blk in range(first, first + count):
            cols = slice(blk * LANES, (blk + 1) * LANES)
            x = x_ref[:, cols]
            if gain_row is not None:
                x = x * lax.rsqrt(jnp.mean(x * x, axis=-1, keepdims=True) + NORM_EPS) * gain_ref[gain_row:gain_row + 1]
            if rotary:
                rot = jnp.where(even_quarter, -pltpu.roll(x, LANES - quarter, axis=1), pltpu.roll(x, quarter, axis=1))
                x = x * cos + rot * sin
            if scale != 1.0:
                x = x * scale
            o_ref[:, cols] = x.astype(o_ref.dtype)


def _qkv_prep(p, width, col_block, rope, gains, segments, quarter, name, tile=256):
    cos, sin = rope
    reps = LANES // cos.shape[1]
    cos, sin = jnp.tile(cos, (1, reps)), jnp.tile(sin, (1, reps))
    per_seq = SEQ // tile
    table = pl.BlockSpec((tile, LANES), lambda i: (i % per_seq, 0))
    return pl.pallas_call(
        functools.partial(_qkv_prep_kernel, segments=segments, quarter=quarter),
        grid=(N_ROWS // tile,),
        in_specs=[pl.BlockSpec((tile, width), lambda i: (i, col_block)), table, table]
        + ([] if gains is None else [pl.BlockSpec(gains.shape, lambda i: (0, 0))]),
        out_specs=pl.BlockSpec((tile, width), lambda i: (i, 0)),
        out_shape=jax.ShapeDtypeStruct((N_ROWS, width), BF16),
        compiler_params=_params("parallel"),
        name=name,
    )(p, cos, sin, *(() if gains is None else (gains,)))


def _even_mixer(a_all, rope, w_in, rw_args, q_g, k_g, need_ctx):
    rkv_w = 3 * RW_W
    att_w = (GQ_HEADS + 2 * GQ_KV_HEADS) * GQ_HEAD
    w_in = jnp.concatenate([w_in[:, :rkv_w], w_in[:, RW_IN:], w_in[:, rkv_w:RW_IN],
                            jnp.zeros((D_MODEL, RW_TAIL_PAD - RW_TAIL), w_in.dtype)], axis=1).astype(BF16)
    p = _matmul(a_all, w_in, out_dtype=F32, tm=1024, tn=768, tk=D_MODEL, name="even_w_in")
    rw = _rwkv_mixer(p, rkv_w + att_w, *rw_args)
    assert rkv_w % att_w == 0
    qkv = _qkv_prep(p, att_w, rkv_w // att_w, rope, jnp.stack([q_g, k_g]),
                    ((0, GQ_HEADS, 0, True, GQ_SCALE * LOG2E), (GQ_HEADS, GQ_KV_HEADS, 1, True, 1.0),
                     (GQ_HEADS + GQ_KV_HEADS, GQ_KV_HEADS, None, False, 1.0)), GQ_HEAD // 4, "gqa_prep")
    at = _gqa_attention(qkv, latent=True, tq=512)
    if need_ctx:
        at = jnp.concatenate([at, _gqa_attention(qkv, latent=False)], axis=0)
    return rw, at


def _odd_mixer(a_all, rope, w_in, q_norm, q_up, kv_norm, kv_up, lq1, lk1, lq2, lk2, subln, layer_idx):
    w_in = jnp.concatenate([w_in[:, MLA_IN:], w_in[:, :MLA_IN],
                            jnp.zeros((D_MODEL, IN_ODD_PADDED - IN_ODD), w_in.dtype)], axis=1).astype(BF16)
    p = _matmul(a_all, w_in, out_dtype=F32, tm=1024, tn=768, tk=D_MODEL, name="odd_w_in")
    lam_init = 0.8 - 0.6 * math.exp(-0.3 * layer_idx)
    lam = (jnp.exp(jnp.sum(lq1 * lk1).astype(F32)) - jnp.exp(jnp.sum(lq2 * lk2).astype(F32)) + lam_init)

    pm = p[:, DIFF_IN:DIFF_IN + MLA_IN]
    c_q = _rms_f32(pm[:N_LAT_ROWS, :Q_LORA], q_norm).astype(BF16)
    c_kv = _rms_f32(pm[:, Q_LORA:Q_LORA + KV_LORA], kv_norm).astype(BF16)
    k_pe = pm[:, Q_LORA + KV_LORA:MLA_IN]
    q_up_h = q_up.reshape(Q_LORA, MLA_HEADS, MLA_NOPE + MLA_ROPE) * (MLA_SCALE * LOG2E)
    q_up_nope = q_up_h[:, :, :MLA_NOPE].reshape(Q_LORA, -1).astype(BF16)
    q_up_pe = q_up_h[:, :, MLA_NOPE:].reshape(Q_LORA, -1).astype(BF16)
    kv_up_h = kv_up.reshape(KV_LORA, MLA_HEADS, MLA_NOPE + MLA_V)
    kv_up_r = jnp.concatenate([kv_up_h[:, :, :MLA_NOPE].reshape(KV_LORA, -1),
                               kv_up_h[:, :, MLA_NOPE:].reshape(KV_LORA, -1)], axis=1).astype(BF16)
    q_nope = _matmul(c_q, q_up_nope, out_dtype=BF16, tm=1024, tn=2048, tk=Q_LORA, name="mla_q_up_nope")
    q_pe = _matmul(c_q, q_up_pe, out_dtype=F32, tm=1024, tn=1024, tk=Q_LORA, name="mla_q_up_rope")
    kv = _matmul(c_kv, kv_up_r, out_dtype=BF16, tm=1024, tn=2048, tk=KV_LORA, name="mla_kv_up")
    q_pe = _rope_lat(q_pe.reshape(N_LAT_ROWS, MLA_HEADS, MLA_ROPE), rope)
    q_pe = q_pe.reshape(N_LAT_ROWS, MLA_HEADS * MLA_ROPE).astype(BF16)
    k_pe = _rope_lat(k_pe.reshape(N_ROWS, 1, MLA_ROPE), rope).reshape(N_ROWS, MLA_ROPE).astype(BF16)
    k_pe = jnp.tile(k_pe, (1, MLA_HEADS_PER_STEP))
    m_out = _mla_attention(q_nope, q_pe, kv, k_pe, tq=512)

    dqkv = _qkv_prep(p, DIFF_IN, 0, rope, None,
                     ((0, DIFF_HEADS, None, True, DIFF_SCALE * LOG2E), (DIFF_HEADS, DIFF_HEADS, None, True, 1.0),
                      (2 * DIFF_HEADS, DIFF_HEADS, None, False, 1.0)), DIFF_HEAD // 4, "diff_prep")
    d_out = _diff_attention(lam.reshape(1), dqkv, subln, 1.0 - lam_init, tq=512)
    return m_out, d_out


def kernel(x, c, ctx, c_ctx, ada_w, ada_b, norm1_g, norm2_g, mlp_w1, mlp_w2, final_g, ev_w_in, ev_w_out, rw_mu, rw_w0, rw_w_up, rw_a0, rw_a_up, rw_g_up, rw_k_k, rw_k_a, rw_r_k, rw_ln_w, rw_ln_b, gq_q_norm, gq_k_norm, od_w_in, od_w_out, mla_q_norm, mla_q_up, mla_kv_norm, mla_kv_up, diff_lq1, diff_lk1, diff_lq2, diff_lk2, diff_subln):
    rope_gq = _axial_rope_tables(SEQ, GQ_HEAD)
    rope_64 = _axial_rope_tables(SEQ, MLA_ROPE)
    h = jnp.concatenate([x.reshape(N_LAT_ROWS, D_MODEL), ctx.reshape(N_CTX_ROWS, D_MODEL)], axis=0)
    cond = jnp.concatenate([c_ctx[None], c, jnp.zeros((3, D_MODEL), F32)], axis=0)
    cond = jax.nn.silu(cond).astype(BF16)
    for i in range(DEPTH):
        last = i == DEPTH - 1
        mod = _matmul(cond, ada_w, layer=i, out_dtype=F32, tm=8, tn=2048, tk=1024, name="adaln_mod")
        mod = (mod + ada_b[i])[:BATCH + 1].reshape(BATCH + 1, N_MOD, 1, D_MODEL)
        sh1, sc1, g1, sh2, sc2, g2 = (mod[:, m] for m in range(N_MOD))
        a_all = _norm(h, norm1_g[i], sc1, sh1, rows=N_ROWS, out_dtype=BF16)
        j = i // 2
        if i % 2 == 0:
            rw_args = (rw_mu[j], rw_w0[j], rw_w_up[j], rw_a0[j], rw_a_up[j], rw_g_up[j], rw_k_k[j], rw_k_a[j],
                       rw_r_k[j], rw_ln_w[j], rw_ln_b[j])
            mix = _even_mixer(a_all, rope_gq, ev_w_in[j], rw_args, gq_q_norm[j], gq_k_norm[j], not last)
            w_out = ev_w_out[j]
        else:
            if not last:
                raise NotImplementedError("context rows of an odd layer are only needed when a layer follows")
            mix = _odd_mixer(a_all, rope_64, od_w_in[j], mla_q_norm[j], mla_q_up[j], mla_kv_norm[j],
                             mla_kv_up[j], diff_lq1[j], diff_lk1[j], diff_lq2[j], diff_lk2[j], diff_subln[j], i)
            w_out = od_w_out[j]
        rows = N_LAT_ROWS if last else N_ROWS
        h = _matmul(mix, w_out.astype(BF16), out_dtype=F32, tm=1024, tn=1024, tk=2048, rows=rows,
                    epilogue="gated_residual", res=h, gate=g1, name="mixer_w_out")
        a2 = _norm(h, norm2_g[i], sc2, sh2, rows=rows, out_dtype=BF16)
        hid = _matmul(a2, mlp_w1, layer=i, out_dtype=BF16, tm=1024, tn=512, tk=D_MODEL,
                      epilogue="relu2", name="mlp_w1")
        h = _matmul(hid, mlp_w2, layer=i, out_dtype=F32, tm=1024, tn=1024, tk=2048,
                    epilogue="gated_residual", res=h, gate=g2, name="mlp_w2")
    out = _norm(h, final_g, rows=N_LAT_ROWS, out_dtype=F32)
    return out.reshape(BATCH, SEQ, D_MODEL)
```

```python
import functools
import math

import jax
import jax.numpy as jnp
from jax import lax
from jax.experimental import pallas as pl
from jax.experimental.pallas import tpu as pltpu

D_MODEL = 4096
BATCH = 4
SEQ = 4096
DEPTH = 2
CTX_LEN = 256
GRID_W = 64
ROPE_THETA = 10000.0
NORM_EPS = 1e-6
N_MOD = 6
HALF = D_MODEL // 2

RW_HEAD = 64
RW_W = HALF
RW_HEADS = RW_W // RW_HEAD
DECAY_RANK = 96
AAA_RANK = 96
GATE_RANK = 256
GN_EPS = 64e-5
RW_IN = 3 * RW_W + GATE_RANK + 2 * DECAY_RANK + 2 * AAA_RANK

GQ_HEAD = 128
GQ_HEADS = HALF // GQ_HEAD
GQ_KV_HEADS = GQ_HEADS // 4
GQ_GROUP = GQ_HEADS // GQ_KV_HEADS
GQ_SCALE = GQ_HEAD ** -0.5

MLA_NOPE = 128
MLA_ROPE = 64
MLA_V = 128
MLA_HEADS = HALF // MLA_V
Q_LORA = 768
KV_LORA = 512
MLA_SCALE = (MLA_NOPE + MLA_ROPE) ** -0.5
MLA_IN = Q_LORA + KV_LORA + MLA_ROPE

DIFF_HEAD = 64
DIFF_V = 2 * DIFF_HEAD
DIFF_HEADS = HALF // DIFF_V
DIFF_SCALE = DIFF_HEAD ** -0.5
DIFF_IN = DIFF_HEADS * (4 * DIFF_HEAD + DIFF_V)
IN_ODD = MLA_IN + DIFF_IN
IN_ODD_PADDED = 7680

N_LAT_ROWS = BATCH * SEQ
N_CTX_ROWS = BATCH * CTX_LEN
N_ROWS = N_LAT_ROWS + N_CTX_ROWS
CTX_BLOCK0 = N_LAT_ROWS // CTX_LEN

V7X_VMEM_LIMIT_BYTES = 56 * 1024 * 1024

BF16 = jnp.bfloat16
F32 = jnp.float32


def _params(*sem):
    return pltpu.CompilerParams(dimension_semantics=sem, vmem_limit_bytes=V7X_VMEM_LIMIT_BYTES)


def _row_group(row_tile, tile_rows):
    start = row_tile * tile_rows
    return jnp.where(start < N_LAT_ROWS, 1 + start // SEQ, 0)


def _norm_kernel(x_ref, g_ref, *rest, modulate):
    if modulate:
        sc_ref, sh_ref, o_ref = rest
    else:
        (o_ref,) = rest
    x = x_ref[...]
    ms = jnp.mean(x * x, axis=-1, keepdims=True)
    y = x * lax.rsqrt(ms + NORM_EPS) * g_ref[...]
    if modulate:
        y = y * (1.0 + sc_ref[0]) + sh_ref[0]
    o_ref[...] = y.astype(o_ref.dtype)


def _norm(x, g, sc=None, sh=None, *, rows, out_dtype, tile=256):
    d = x.shape[1]
    modulate = sc is not None
    in_specs = [pl.BlockSpec((tile, d), lambda i: (i, 0)),
                pl.BlockSpec((1, d), lambda i: (0, 0))]
    args = [x, g.reshape(1, d)]
    if modulate:
        mod_spec = pl.BlockSpec((1, 1, d), lambda i: (_row_group(i, tile), 0, 0))
        in_specs += [mod_spec, mod_spec]
        args += [sc, sh]
    return pl.pallas_call(
        functools.partial(_norm_kernel, modulate=modulate),
        grid=(rows // tile,),
        in_specs=in_specs,
        out_specs=pl.BlockSpec((tile, d), lambda i: (i, 0)),
        out_shape=jax.ShapeDtypeStruct((rows, d), out_dtype),
        compiler_params=_params("parallel"),
        name="rmsnorm_mod" if modulate else "rmsnorm",
    )(*args)


def _mm_kernel(*refs, epilogue, nk, n_a):
    a_refs, w_ref, rest = refs[:n_a], refs[n_a], refs[n_a + 1:]
    if epilogue == "gated_residual":
        res_ref, gate_ref, o_ref, acc_ref = rest
    else:
        o_ref, acc_ref = rest
    k = pl.program_id(2)

    def product(a_ref):
        return jnp.dot(a_ref[...].astype(BF16), w_ref[...].astype(BF16), preferred_element_type=F32)

    def store(part, first):
        if first:
            acc_ref[...] = part
        else:
            acc_ref[...] += part

    if n_a == 1:
        part = product(a_refs[0])
        pl.when(k == 0)(lambda: store(part, True))
        pl.when(k > 0)(lambda: store(part, False))
    else:
        for idx, a_ref in enumerate(a_refs):
            pl.when(k == idx)(lambda a_ref=a_ref, idx=idx: store(product(a_ref), idx == 0))

    @pl.when(k == nk - 1)
    def _():
        acc = acc_ref[...]
        if epilogue == "relu2":
            acc = jnp.square(jnp.maximum(acc, 0.0))
        elif epilogue == "gated_residual":
            acc = res_ref[...] + gate_ref[0] * acc
        o_ref[...] = acc.astype(o_ref.dtype)


def _matmul(a, w, *, out_dtype, tm, tn, tk, rows=None, layer=None, epilogue=None, res=None, gate=None, name):
    kdim, n = w.shape[-2:]
    a_parts = a if isinstance(a, tuple) else (a,)
    m = a_parts[0].shape[0] if rows is None else rows
    assert sum(part.shape[1] for part in a_parts) == kdim and m % tm == 0 and n % tn == 0 and kdim % tk == 0
    assert (w.ndim == 3) == (layer is not None)
    nk = kdim // tk
    w_spec = (pl.BlockSpec((tk, tn), lambda i, j, k: (k, j)) if layer is None else
              pl.BlockSpec((None, tk, tn), lambda i, j, k: (layer, k, j)))
    if len(a_parts) == 1:
        in_specs = [pl.BlockSpec((tm, tk), lambda i, j, k: (i, k)), w_spec]
    else:
        assert len(a_parts) == nk and all(part.shape[1] == tk for part in a_parts)
        in_specs = [pl.BlockSpec((tm, tk), lambda i, j, k: (i, 0))] * nk + [w_spec]
    args = [*a_parts, w]
    if epilogue == "gated_residual":
        in_specs += [pl.BlockSpec((tm, tn), lambda i, j, k: (i, j)),
                     pl.BlockSpec((1, 1, tn), lambda i, j, k: (_row_group(i, tm), 0, j))]
        args += [res, gate]
    return pl.pallas_call(
        functools.partial(_mm_kernel, epilogue=epilogue, nk=nk, n_a=len(a_parts)),
        grid=(m // tm, n // tn, nk),
        in_specs=in_specs,
        out_specs=pl.BlockSpec((tm, tn), lambda i, j, k: (i, j)),
        out_shape=jax.ShapeDtypeStruct((m, n), out_dtype),
        scratch_shapes=[pltpu.VMEM((tm, tn), F32)],
        compiler_params=_params("parallel", "parallel", "arbitrary"),
        name=name,
    )(*args)


def _nt_dot(a, b):
    return lax.dot_general(a, b, (((1,), (1,)), ((), ())), preferred_element_type=F32)


LOG2E = math.log2(math.e)


def _fill_values_and_ones(v_ref, vo_ref):
    d = v_ref.shape[1]
    vo_ref[:, :d] = v_ref[...]
    vo_ref[:, d:] = jnp.ones_like(v_ref)


def _softmax_times_values(s_list, vo_refs):
    d = vo_refs[0].shape[1] // 2
    m = functools.reduce(jnp.maximum, [jnp.max(s, axis=-1, keepdims=True) for s in s_list])
    acc = functools.reduce(jnp.add, [jnp.dot(jnp.exp2(s - m).astype(BF16), vo_ref[...], preferred_element_type=F32)
                                     for s, vo_ref in zip(s_list, vo_refs)])
    return acc[:, :d] / acc[:, d:]


def _attend_each(items):
    outs = []
    pending = None
    for q, k_refs, vo_refs in items:
        s_list = [_nt_dot(q, k_ref[...]) for k_ref in k_refs]
        if pending is not None:
            outs.append(_softmax_times_values(*pending))
        pending = (s_list, vo_refs)
    outs.append(_softmax_times_values(*pending))
    return outs


def _gqa_kernel(q_ref, *rest, with_lat):
    if with_lat:
        kc_ref, vc_ref, kl_ref, vl_ref, o_ref, voc_ref, vol_ref = rest
        k_refs, v_refs, vo_refs = (kc_ref, kl_ref), (vc_ref, vl_ref), (voc_ref, vol_ref)
    else:
        kc_ref, vc_ref, o_ref, voc_ref = rest
        k_refs, v_refs, vo_refs = (kc_ref,), (vc_ref,), (voc_ref,)

    @pl.when(pl.program_id(2) == 0)
    def _():
        for v_ref, vo_ref in zip(v_refs, vo_refs):
            _fill_values_and_ones(v_ref, vo_ref)

    outs = _attend_each([(q_ref[:, g * GQ_HEAD:(g + 1) * GQ_HEAD], k_refs, vo_refs) for g in range(GQ_GROUP)])
    for g, o in enumerate(outs):
        o_ref[:, g * GQ_HEAD:(g + 1) * GQ_HEAD] = o.astype(o_ref.dtype)


def _gqa_attention(qkv, *, latent, tq=256):
    gw = GQ_GROUP * GQ_HEAD
    k0, v0 = GQ_HEADS, GQ_HEADS + GQ_KV_HEADS
    ctx_k = pl.BlockSpec((CTX_LEN, GQ_HEAD), lambda b, n, i: (CTX_BLOCK0 + b, k0 + n))
    ctx_v = pl.BlockSpec((CTX_LEN, GQ_HEAD), lambda b, n, i: (CTX_BLOCK0 + b, v0 + n))
    scratch = [pltpu.VMEM((CTX_LEN, 2 * GQ_HEAD), BF16)]
    if latent:
        per_b = SEQ // tq
        lat_k = pl.BlockSpec((SEQ, GQ_HEAD), lambda b, n, i: (b, k0 + n))
        lat_v = pl.BlockSpec((SEQ, GQ_HEAD), lambda b, n, i: (b, v0 + n))
        in_specs = [pl.BlockSpec((tq, gw), lambda b, n, i: (b * per_b + i, n)), ctx_k, ctx_v, lat_k, lat_v]
        args = (qkv,) * 5
        rows = N_LAT_ROWS
        scratch.append(pltpu.VMEM((SEQ, 2 * GQ_HEAD), BF16))
    else:
        per_b = CTX_LEN // tq
        q_blk0 = N_LAT_ROWS // tq
        in_specs = [pl.BlockSpec((tq, gw), lambda b, n, i: (q_blk0 + b * per_b + i, n)), ctx_k, ctx_v]
        args = (qkv,) * 3
        rows = N_CTX_ROWS
    return pl.pallas_call(
        functools.partial(_gqa_kernel, with_lat=latent),
        grid=(BATCH, GQ_KV_HEADS, per_b),
        in_specs=in_specs,
        out_specs=pl.BlockSpec((tq, gw), lambda b, n, i: (b * per_b + i, n)),
        out_shape=jax.ShapeDtypeStruct((rows, GQ_HEADS * GQ_HEAD), BF16),
        scratch_shapes=scratch,
        compiler_params=_params("parallel", "parallel", "arbitrary"),
        name="gqa_attention_lat" if latent else "gqa_attention_ctx",
    )(*args)


def _mla_kernel(qn_ref, qp_ref, knc_ref, kpc_ref, vc_ref, knl_ref, kpl_ref, vl_ref, o_ref,
                kc_ref, kl_ref, voc_ref, vol_ref):
    heads = range(MLA_HEADS_PER_STEP)
    cols = [slice(j * MLA_NOPE, (j + 1) * MLA_NOPE) for j in heads]

    def own_rope_lanes(x, j):
        lane = lax.broadcasted_iota(jnp.int32, x.shape, 1)
        return jnp.where(lane // MLA_ROPE == j, x, jnp.zeros_like(x))

    @pl.when(pl.program_id(2) == 0)
    def _():
        for kn_ref, kp_ref, v_ref, k_ref, vo_ref in ((knc_ref, kpc_ref, vc_ref, kc_ref, voc_ref),
                                                     (knl_ref, kpl_ref, vl_ref, kl_ref, vol_ref)):
            for j in heads:
                k_ref[j, :, :MLA_NOPE] = kn_ref[:, cols[j]]
                k_ref[j, :, MLA_NOPE:] = own_rope_lanes(kp_ref[...], j)
                vo_ref[j, :, :MLA_V] = v_ref[:, cols[j]]
                vo_ref[j, :, MLA_V:] = jnp.ones_like(kp_ref)

    qp = qp_ref[...]
    items = [(jnp.concatenate([qn_ref[:, cols[j]], own_rope_lanes(qp, j)], axis=1), (kc_ref.at[j], kl_ref.at[j]),
              (voc_ref.at[j], vol_ref.at[j])) for j in heads]
    for j, o in enumerate(_attend_each(items)):
        o_ref[:, cols[j]] = o.astype(o_ref.dtype)


MLA_HEADS_PER_STEP = 2


def _mla_attention(q_nope, q_pe, kv, k_pe, *, tq=256):
    per_b = SEQ // tq
    hs = MLA_HEADS_PER_STEP
    assert hs * MLA_ROPE == MLA_NOPE
    w = hs * MLA_NOPE
    v_blk0 = MLA_HEADS // hs
    in_specs = [
        pl.BlockSpec((tq, w), lambda b, h, i: (b * per_b + i, h)),
        pl.BlockSpec((tq, hs * MLA_ROPE), lambda b, h, i: (b * per_b + i, h)),
        pl.BlockSpec((CTX_LEN, w), lambda b, h, i: (CTX_BLOCK0 + b, h)),
        pl.BlockSpec((CTX_LEN, MLA_NOPE), lambda b, h, i: (CTX_BLOCK0 + b, 0)),
        pl.BlockSpec((CTX_LEN, w), lambda b, h, i: (CTX_BLOCK0 + b, v_blk0 + h)),
        pl.BlockSpec((SEQ, w), lambda b, h, i: (b, h)),
        pl.BlockSpec((SEQ, MLA_NOPE), lambda b, h, i: (b, 0)),
        pl.BlockSpec((SEQ, w), lambda b, h, i: (b, v_blk0 + h)),
    ]
    return pl.pallas_call(
        _mla_kernel,
        grid=(BATCH, MLA_HEADS // hs, per_b),
        in_specs=in_specs,
        out_specs=pl.BlockSpec((tq, w), lambda b, h, i: (b * per_b + i, h)),
        out_shape=jax.ShapeDtypeStruct((N_LAT_ROWS, MLA_HEADS * MLA_V), BF16),
        scratch_shapes=[pltpu.VMEM((hs, CTX_LEN, 2 * MLA_NOPE), BF16), pltpu.VMEM((hs, SEQ, 2 * MLA_NOPE), BF16),
                        pltpu.VMEM((hs, CTX_LEN, 2 * MLA_V), BF16), pltpu.VMEM((hs, SEQ, 2 * MLA_V), BF16)],
        compiler_params=_params("parallel", "parallel", "arbitrary"),
        name="mla_attention",
    )(q_nope, q_pe, kv, k_pe, kv, kv, k_pe, kv)


def _diff_kernel(lam_ref, q_ref, kc_ref, vc_ref, kl_ref, vl_ref, g_ref, o_ref, voc_ref, vol_ref, *, out_scale):
    heads = range(DIFF_HEADS_PER_STEP)
    cols = [slice(j * DIFF_V, (j + 1) * DIFF_V) for j in heads]

    @pl.when(pl.program_id(2) == 0)
    def _():
        for v_ref, vo_ref in ((vc_ref, voc_ref), (vl_ref, vol_ref)):
            for j in heads:
                vo_ref[j, :, :DIFF_V] = v_ref[:, cols[j]]
                vo_ref[j, :, DIFF_V:] = jnp.ones((v_ref.shape[0], DIFF_V), BF16)

    first = lax.broadcasted_iota(jnp.int32, (q_ref.shape[0], DIFF_V), 1) < DIFF_HEAD
    items = []
    for j in heads:
        q = q_ref[:, cols[j]]
        zero = jnp.zeros_like(q)
        keys = (kc_ref[:, cols[j]], kl_ref[:, cols[j]])
        for qh in (jnp.where(first, q, zero), jnp.where(first, zero, q)):
            items.append((qh, keys, (voc_ref.at[j], vol_ref.at[j])))
    outs = _attend_each(items)
    for j in heads:
        o = outs[2 * j] - lam_ref[0] * outs[2 * j + 1]
        ms = jnp.mean(o * o, axis=-1, keepdims=True)
        o = o * lax.rsqrt(ms + NORM_EPS) * g_ref[...]
        o_ref[:, cols[j]] = (o * out_scale).astype(o_ref.dtype)


DIFF_HEADS_PER_STEP = 2


def _diff_attention(lam, qkv, subln, out_scale, *, tq=256):
    per_b = SEQ // tq
    hs = DIFF_HEADS_PER_STEP
    w = hs * DIFF_V
    k0, v0 = DIFF_HEADS // hs, 2 * DIFF_HEADS // hs
    ctx_k = pl.BlockSpec((CTX_LEN, w), lambda b, h, i: (CTX_BLOCK0 + b, k0 + h))
    ctx_v = pl.BlockSpec((CTX_LEN, w), lambda b, h, i: (CTX_BLOCK0 + b, v0 + h))
    lat_k = pl.BlockSpec((SEQ, w), lambda b, h, i: (b, k0 + h))
    lat_v = pl.BlockSpec((SEQ, w), lambda b, h, i: (b, v0 + h))
    in_specs = [
        pl.BlockSpec(memory_space=pltpu.SMEM),
        pl.BlockSpec((tq, w), lambda b, h, i: (b * per_b + i, h)),
        ctx_k, ctx_v, lat_k, lat_v,
        pl.BlockSpec((1, DIFF_V), lambda b, h, i: (0, 0)),
    ]
    return pl.pallas_call(
        functools.partial(_diff_kernel, out_scale=out_scale),
        grid=(BATCH, DIFF_HEADS // hs, per_b),
        in_specs=in_specs,
        out_specs=pl.BlockSpec((tq, w), lambda b, h, i: (b * per_b + i, h)),
        out_shape=jax.ShapeDtypeStruct((N_LAT_ROWS, DIFF_HEADS * DIFF_V), BF16),
        scratch_shapes=[pltpu.VMEM((hs, CTX_LEN, 2 * DIFF_V), BF16), pltpu.VMEM((hs, SEQ, 2 * DIFF_V), BF16)],
        compiler_params=_params("parallel", "parallel", "arbitrary"),
        name="diff_attention",
    )(lam, qkv, qkv, qkv, qkv, qkv, subln.reshape(1, DIFF_V))


RW_CHUNK = 128
RW_PAIRS = RW_HEADS // 2
LANES = 2 * RW_HEAD
SUB = 8
RW_UNROLL = 16
RW_GROUP = 8


def _rwkv_kernel(r_ref, kk_ref, v_ref, w_ref, k_ref, b_ref, y_out, s_ref, y_ref):
    d = pl.program_id(0)
    c = pl.program_id(2)

    @pl.when(c == 0)
    def _():
        s_ref[...] = jnp.zeros_like(s_ref)

    v_hi = lax.broadcasted_iota(jnp.int32, (SUB, SUB, LANES), 0)
    v_lo = lax.broadcasted_iota(jnp.int32, (SUB, SUB, LANES), 1)
    lane3 = lax.broadcasted_iota(jnp.int32, (SUB, SUB, LANES), 2)
    dup = jnp.where((lane3 & (RW_HEAD - 1)) == v_hi * SUB + v_lo, 1.0, 0.0)
    blk_r = lax.broadcasted_iota(jnp.int32, (LANES, LANES), 0) // RW_HEAD
    blk_c = lax.broadcasted_iota(jnp.int32, (LANES, LANES), 1) // RW_HEAD
    head_ones = jnp.where(blk_r == blk_c, 1.0, 0.0).astype(BF16)
    slot_of_lane = lax.broadcasted_iota(jnp.int32, (SUB, LANES), 1) & (RW_HEAD - 1)
    step_dir = jnp.where(d == 0, 1, -1)

    y_ref[...] = jnp.zeros_like(y_ref)

    def row(ref, p, t, *lead):
        return ref[(*lead, p, pl.ds(t, SUB, stride=0), slice(None))]

    def as_rows(tile):
        return tile.reshape(RW_HEAD, LANES).astype(BF16)

    def head_sums(states, pairs, t_done, t_next):
        rows = []
        for p in pairs:
            rows.append(as_rows(states[p] * row(r_ref, p, t_done)))
            if t_next is not None:
                rows.append(as_rows(states[p] * row(kk_ref, p, t_next)))
                rows.append(as_rows(dup * row(v_ref, p, t_next)))
        return jnp.dot(jnp.concatenate(rows, axis=0), head_ones, preferred_element_type=F32)

    def slab(sums, q, n_slabs, i):
        lo = (q * n_slabs + i) * RW_HEAD
        return sums[lo:lo + RW_HEAD].reshape(SUB, SUB, LANES)

    def put_y(p, y_b, t, valid):
        window = t // RW_HEAD
        keep = (slot_of_lane == t % RW_HEAD) & valid
        y_ref[0, p, window] = jnp.where(keep, y_b, y_ref[0, p, window])

    groups = [range(g0, g0 + RW_GROUP) for g0 in range(0, RW_PAIRS, RW_GROUP)]

    def block(g, carry):
        first = g * RW_UNROLL
        first = jnp.where(d == 0, first, RW_CHUNK - 1 - first)
        times = [first + u * step_dir for u in range(RW_UNROLL)]
        t_before = jnp.clip(first - step_dir, 0, RW_CHUNK - 1)
        states = [s_ref[p] for p in range(RW_PAIRS)]
        sums = [head_sums(states, grp, t_before, times[0]) for grp in groups]
        for gi, grp in enumerate(groups):
            for q, p in enumerate(grp):
                put_y(p, slab(sums[gi], q, 3, 0), t_before, g > 0)
        for u in range(RW_UNROLL):
            t = times[u]
            for gi, grp in enumerate(groups):
                for q, p in enumerate(grp):
                    sa = slab(sums[gi], q, 3, 1)
                    vb = slab(sums[gi], q, 3, 2)
                    states[p] = (states[p] * row(w_ref, p, t, 0) - sa * row(b_ref, p, t, 0)
                                 + vb * row(k_ref, p, t, 0))
                if u + 1 < RW_UNROLL:
                    sums[gi] = head_sums(states, grp, t, times[u + 1])
                    for q, p in enumerate(grp):
                        put_y(p, slab(sums[gi], q, 3, 0), t, True)
        for p in range(RW_PAIRS):
            s_ref[p] = states[p]
        return carry

    lax.fori_loop(0, RW_CHUNK // RW_UNROLL, block, 0)

    t_last = jnp.where(d == 0, RW_CHUNK - 1, 0)
    states = [s_ref[p] for p in range(RW_PAIRS)]
    for grp in groups:
        last = head_sums(states, grp, t_last, None)
        for q, p in enumerate(grp):
            put_y(p, slab(last, q, 1, 0), t_last, True)

    first_head = lax.broadcasted_iota(jnp.int32, (RW_HEAD, LANES), 1) < RW_HEAD
    for p in range(0, RW_PAIRS, 2):
        for window in range(RW_CHUNK // RW_HEAD):
            both = jnp.concatenate([y_ref[0, p, window].reshape(RW_HEAD, LANES),
                                    y_ref[0, p + 1, window].reshape(RW_HEAD, LANES)], axis=0)
            by_row = both.T
            head0, head1 = by_row[:RW_HEAD], by_row[RW_HEAD:]
            rows = slice(window * RW_HEAD, (window + 1) * RW_HEAD)
            y_out[0, rows, p * LANES:(p + 1) * LANES] = jnp.where(
                first_head, head0, pltpu.roll(head1, RW_HEAD, axis=1))
            y_out[0, rows, (p + 1) * LANES:(p + 2) * LANES] = jnp.where(
                first_head, pltpu.roll(head0, RW_HEAD, axis=1), head1)


def _rwkv_scan(r, kk, v, w, k, b):
    n_ctx = CTX_LEN // RW_CHUNK
    n_lat = SEQ // RW_CHUNK
    ctx0 = N_LAT_ROWS // RW_CHUNK

    def chunk(d, bb, c):
        j = jnp.where(d == 0, c, jnp.where(c < n_ctx, n_ctx - 1 - c, n_ctx + n_lat - 1 - (c - n_ctx)))
        return jnp.where(j < n_ctx, ctx0 + bb * n_ctx + j, bb * n_lat + j - n_ctx)

    shared = pl.BlockSpec((RW_PAIRS, RW_CHUNK, LANES), lambda d, bb, c: (0, chunk(d, bb, c), 0))
    per_dir = pl.BlockSpec((1, RW_PAIRS, RW_CHUNK, LANES), lambda d, bb, c: (d, 0, chunk(d, bb, c), 0))
    return pl.pallas_call(
        _rwkv_kernel,
        grid=(2, BATCH, n_ctx + n_lat),
        in_specs=[shared, shared, shared, per_dir, per_dir, per_dir],
        out_specs=pl.BlockSpec((1, RW_CHUNK, RW_W), lambda d, bb, c: (d, chunk(d, bb, c), 0)),
        out_shape=jax.ShapeDtypeStruct((2, N_ROWS, RW_W), F32),
        scratch_shapes=[pltpu.VMEM((RW_PAIRS, RW_HEAD // SUB, SUB, LANES), F32),
                        pltpu.VMEM((1, RW_PAIRS, RW_CHUNK // RW_HEAD, SUB, SUB, LANES), F32)],
        compiler_params=_params("parallel", "parallel", "arbitrary"),
        name="rwkv7_scan",
    )(r, kk, v, w, k, b)


def _axial_rope_tables(n_tokens, rot_dim):
    n_rows = n_tokens // GRID_W
    row = jnp.repeat(jnp.arange(n_rows, dtype=F32), GRID_W)
    col = jnp.tile(jnp.arange(GRID_W, dtype=F32), n_rows)
    axis_dim = rot_dim // 2
    inv_freq = ROPE_THETA ** (-jnp.arange(0, axis_dim, 2, dtype=F32) / axis_dim)
    ang_r = row[:, None] * inv_freq
    ang_c = col[:, None] * inv_freq
    ang = jnp.concatenate([ang_r, ang_r, ang_c, ang_c], axis=-1)
    return jnp.cos(ang), jnp.sin(ang)


def _rotate_half(z):
    z1, z2 = jnp.split(z, 2, axis=-1)
    return jnp.concatenate([-z2, z1], axis=-1)


def _apply_rope(x, cos, sin):
    half = x.shape[-1] // 2
    rot = jnp.concatenate([_rotate_half(x[..., :half]), _rotate_half(x[..., half:])], axis=-1)
    return x * cos + rot * sin


def _rms_f32(x, g):
    return x * lax.rsqrt(jnp.mean(x * x, axis=-1, keepdims=True) + NORM_EPS) * g


def _rope_lat(x, rope):
    cos, sin = rope
    heads, dim = x.shape[1:]
    lat = x[:N_LAT_ROWS].reshape(BATCH, SEQ, heads, dim)
    lat = _apply_rope(lat, cos[None, :, None, :], sin[None, :, None, :]).reshape(N_LAT_ROWS, heads, dim)
    return lat if x.shape[0] == N_LAT_ROWS else jnp.concatenate([lat, x[N_LAT_ROWS:]], axis=0)


RW_SLAB = 512
RW_TAIL = GATE_RANK + 2 * DECAY_RANK + 2 * AAA_RANK
RW_TAIL_PAD = 768
RW_ROW_TILE = 256


def _head_lane_sums(x):
    rows = x.shape[0]
    blk_r = lax.broadcasted_iota(jnp.int32, (LANES, LANES), 0) // RW_HEAD
    blk_c = lax.broadcasted_iota(jnp.int32, (LANES, LANES), 1) // RW_HEAD
    head_ones = jnp.where(blk_r == blk_c, 1.0, 0.0).astype(BF16)
    hi = x.astype(BF16)
    rest = x - hi.astype(F32)
    mid = rest.astype(BF16)
    lo = (rest - mid.astype(F32)).astype(BF16)
    pieces = jnp.concatenate([hi, mid, lo], axis=0)
    out = []
    for s in range(x.shape[1] // LANES):
        part = jnp.dot(pieces[:, s * LANES:(s + 1) * LANES], head_ones, preferred_element_type=F32)
        out.append(part[:rows] + part[rows:2 * rows] + part[2 * rows:])
    return jnp.concatenate(out, axis=1)


def _rwkv_prep_kernel(*refs):
    slabs = [refs[4 * n:4 * n + 4] for n in range(4)]
    (up_w_ref, w0_ref, a0_ref, kk_gain_ref, ka_ref, rk_ref,
     r_out, v_out, kk_out, decay_out, kdir_out, b_out, g_out, bonus_out) = refs[16:30]
    stage_refs = refs[30:34]
    tile = RW_ROW_TILE
    row0 = pl.program_id(0) * tile
    seq_len = jnp.where(row0 < N_LAT_ROWS, SEQ, CTX_LEN)
    starts_seq = row0 % seq_len == 0
    ends_seq = (row0 + tile) % seq_len == 0

    def shifted(main_ref, before_ref, after_ref, mu_ref, stage_ref):
        x = main_ref[...]
        stage_ref[SUB:SUB + tile, :] = x
        stage_ref[SUB - 1:SUB, :] = jnp.where(starts_seq, 0.0, before_ref[SUB - 1:SUB, :])
        stage_ref[SUB + tile:SUB + tile + 1, :] = jnp.where(ends_seq, 0.0, after_ref[0:1, :])
        around = 0.5 * (stage_ref[SUB - 1:SUB - 1 + tile, :] + stage_ref[SUB + 1:SUB + 1 + tile, :])
        return x + (around - x) * mu_ref[...]

    def put_pairs(out_ref, x, *lead):
        for q in range(RW_SLAB // LANES):
            out_ref[(*lead, q)] = x[:, q * LANES:(q + 1) * LANES]

    r, k, v, tail = (shifted(*slab, stage) for slab, stage in zip(slabs, stage_refs))
    put_pairs(r_out, r)
    put_pairs(v_out, v)
    s_w = RW_SLAB
    mid = tail[:, GATE_RANK:2 * GATE_RANK]
    mid_lane = lax.broadcasted_iota(jnp.int32, (1, GATE_RANK), 1)
    act = jnp.concatenate([jax.nn.sigmoid(tail[:, :GATE_RANK]),
                           jnp.where(mid_lane < 2 * DECAY_RANK, jnp.tanh(mid), mid),
                           tail[:, 2 * GATE_RANK:]], axis=1)
    up = jnp.dot(act.astype(BF16), up_w_ref[...], preferred_element_type=F32)
    g_out[...] = up[:, :s_w]
    kq = k * kk_gain_ref[...]
    kk = kq / jnp.maximum(jnp.sqrt(_head_lane_sums(kq * kq)), 1e-12)
    put_pairs(kk_out, kk)
    kdir_sum = jnp.zeros_like(k)
    for d in range(2):
        x = w0_ref[d] + up[:, (1 + d) * s_w:(2 + d) * s_w]
        put_pairs(decay_out, jnp.exp(jax.nn.sigmoid(x) * -math.exp(-0.5)), d)
        a = jax.nn.sigmoid(a0_ref[d] + up[:, (3 + d) * s_w:(4 + d) * s_w])
        kd = k * (1.0 + (a - 1.0) * ka_ref[...])
        put_pairs(kdir_out, kd, d)
        put_pairs(b_out, kk * a, d)
        kdir_sum = kdir_sum + kd
    bonus_out[...] = _head_lane_sums(r * kdir_sum * rk_ref[...]) * v


def _rwkv_post_kernel(y_ref, g_ref, bonus_ref, lnw_ref, lnb_ref, o_ref):
    y = y_ref[0] + y_ref[1]
    mean = _head_lane_sums(y) * (1.0 / RW_HEAD)
    cen = y - mean
    var = _head_lane_sums(cen * cen) * (1.0 / RW_HEAD)
    yn = cen * lax.rsqrt(var + GN_EPS) * lnw_ref[...] + lnb_ref[...]
    o_ref[...] = ((yn + bonus_ref[...]) * g_ref[...]).astype(o_ref.dtype)


def _rwkv_mixer(p, tail_col, mu, w0, w_up, a0, a_up, g_up, k_k, k_a, r_k, ln_w, ln_b):
    assert tail_col % RW_TAIL_PAD == 0 and p.shape[1] >= tail_col + RW_TAIL_PAD
    n_slab = RW_W // RW_SLAB
    up_w = jnp.zeros((RW_TAIL_PAD, 5, RW_W), F32)
    o = GATE_RANK
    up_w = up_w.at[:o, 0].set(g_up)
    for d in range(2):
        up_w = up_w.at[o + d * DECAY_RANK:o + (d + 1) * DECAY_RANK, 1 + d].set(w_up[d])
    o += 2 * DECAY_RANK
    for d in range(2):
        up_w = up_w.at[o + d * AAA_RANK:o + (d + 1) * AAA_RANK, 3 + d].set(a_up[d])
    up_w = up_w.reshape(RW_TAIL_PAD, 5, n_slab, RW_SLAB).transpose(2, 0, 1, 3)
    up_w = up_w.reshape(n_slab, RW_TAIL_PAD, 5 * RW_SLAB).astype(BF16)
    mu_main = mu[:3 * RW_W].reshape(1, 3 * RW_W)
    mu_tail = jnp.pad(mu[3 * RW_W:], (0, RW_TAIL_PAD - RW_TAIL)).reshape(1, RW_TAIL_PAD)

    tile = RW_ROW_TILE
    col_blocks = RW_W // RW_SLAB
    halo = tile // SUB
    last_halo = N_ROWS // SUB - 1

    def with_halo(width, col):
        return [pl.BlockSpec((tile, width), lambda i, j: (i, col(j))),
                pl.BlockSpec((SUB, width), lambda i, j: (jnp.maximum(i * halo - 1, 0), col(j))),
                pl.BlockSpec((SUB, width), lambda i, j: (jnp.minimum((i + 1) * halo, last_halo), col(j)))]

    in_specs, args = [], []
    for n in range(3):
        in_specs += with_halo(RW_SLAB, lambda j, n=n: n * col_blocks + j)
        in_specs.append(pl.BlockSpec((1, RW_SLAB), lambda i, j, n=n: (0, n * col_blocks + j)))
        args += [p, p, p, mu_main]
    tail_blk = tail_col // RW_TAIL_PAD
    in_specs += with_halo(RW_TAIL_PAD, lambda j: tail_blk)
    in_specs.append(pl.BlockSpec((1, RW_TAIL_PAD), lambda i, j: (0, 0)))
    args += [p, p, p, mu_tail]

    vec = pl.BlockSpec((1, RW_SLAB), lambda i, j: (0, j))
    vec2 = pl.BlockSpec((2, 1, RW_SLAB), lambda i, j: (0, 0, j))
    out1 = pl.BlockSpec((tile, RW_SLAB), lambda i, j: (i, j))
    out2 = pl.BlockSpec((2, tile, RW_SLAB), lambda i, j: (0, i, j))
    one = jax.ShapeDtypeStruct((N_ROWS, RW_W), F32)
    slab_pairs = RW_SLAB // LANES
    pairs1 = pl.BlockSpec((slab_pairs, tile, LANES), lambda i, j: (j, i, 0))
    pairs2 = pl.BlockSpec((2, slab_pairs, tile, LANES), lambda i, j: (0, j, i, 0))
    one_pm = jax.ShapeDtypeStruct((RW_PAIRS, N_ROWS, LANES), F32)
    two_pm = jax.ShapeDtypeStruct((2, RW_PAIRS, N_ROWS, LANES), F32)
    in_specs += [pl.BlockSpec((None, RW_TAIL_PAD, 5 * RW_SLAB), lambda i, j: (j, 0, 0)), vec2, vec2, vec, vec, vec]
    args += [up_w, w0.reshape(2, 1, RW_W), a0.reshape(2, 1, RW_W), k_k.reshape(1, RW_W), k_a.reshape(1, RW_W),
             r_k.reshape(1, RW_W)]
    stage = lambda width: pltpu.VMEM((tile + 2 * SUB, width), F32)
    r, v, kk, decay, k_dir, b_dir, g, bonus = pl.pallas_call(
        _rwkv_prep_kernel,
        grid=(N_ROWS // tile, n_slab),
        in_specs=in_specs,
        out_specs=[pairs1, pairs1, pairs1, pairs2, pairs2, pairs2, out1, out1],
        out_shape=[one_pm, one_pm, one_pm, two_pm, two_pm, two_pm, one, one],
        scratch_shapes=[stage(RW_SLAB), stage(RW_SLAB), stage(RW_SLAB), stage(RW_TAIL_PAD)],
        compiler_params=_params("parallel", "parallel"),
        name="rwkv7_prep",
    )(*args)

    y2 = _rwkv_scan(r, kk, v, decay, k_dir, b_dir)

    return pl.pallas_call(
        _rwkv_post_kernel,
        grid=(N_ROWS // tile, n_slab),
        in_specs=[out2, out1, out1, vec, vec],
        out_specs=out1,
        out_shape=jax.ShapeDtypeStruct((N_ROWS, RW_W), BF16),
        compiler_params=_params("parallel", "parallel"),
        name="rwkv7_post",
    )(y2, g, bonus, ln_w.reshape(1, RW_W), ln_b.reshape(1, RW_W))


def _qkv_prep_kernel(x_ref, cos_ref, sin_ref, *rest, segments, quarter):
    gain_ref, o_ref = rest if len(rest) == 2 else (None, rest[0])
    _qkv_prep_body(x_ref, cos_ref, sin_ref, gain_ref, o_ref, segments, quarter)


def _qkv_prep_body(x_ref, cos_ref, sin_ref, gain_ref, o_ref, segments, quarter):
    tile = x_ref.shape[0]
    latent = pl.program_id(0) * tile < N_LAT_ROWS
    cos = jnp.where(latent, cos_ref[...], 1.0)
    sin = jnp.where(latent, sin_ref[...], 0.0)
    lane = lax.broadcasted_iota(jnp.int32, (1, LANES), 1)
    even_quarter = (lane // quarter) % 2 == 0
    for first, count, gain_row, rotary, scale in segments:
        for blk in range(first, first + count):
            cols = slice(blk * LANES, (blk + 1) * LANES)
            x = x_ref[:, cols]
            if gain_row is not None:
                x = x * lax.rsqrt(jnp.mean(x * x, axis=-1, keepdims=True) + NORM_EPS) * gain_ref[gain_row:gain_row + 1]
            if rotary:
                rot = jnp.where(even_quarter, -pltpu.roll(x, LANES - quarter, axis=1), pltpu.roll(x, quarter, axis=1))
                x = x * cos + rot * sin
            if scale != 1.0:
                x = x * scale
            o_ref[:, cols] = x.astype(o_ref.dtype)


def _qkv_prep(p, width, col_block, rope, gains, segments, quarter, name, tile=256):
    cos, sin = rope
    reps = LANES // cos.shape[1]
    cos, sin = jnp.tile(cos, (1, reps)), jnp.tile(sin, (1, reps))
    per_seq = SEQ // tile
    table = pl.BlockSpec((tile, LANES), lambda i: (i % per_seq, 0))
    return pl.pallas_call(
        functools.partial(_qkv_prep_kernel, segments=segments, quarter=quarter),
        grid=(N_ROWS // tile,),
        in_specs=[pl.BlockSpec((tile, width), lambda i: (i, col_block)), table, table]
        + ([] if gains is None else [pl.BlockSpec(gains.shape, lambda i: (0, 0))]),
        out_specs=pl.BlockSpec((tile, width), lambda i: (i, 0)),
        out_shape=jax.ShapeDtypeStruct((N_ROWS, width), BF16),
        compiler_params=_params("parallel"),
        name=name,
    )(p, cos, sin, *(() if gains is None else (gains,)))


def _even_mixer(a_all, rope, w_in, rw_args, q_g, k_g, need_ctx):
    rkv_w = 3 * RW_W
    att_w = (GQ_HEADS + 2 * GQ_KV_HEADS) * GQ_HEAD
    w_in = jnp.concatenate([w_in[:, :rkv_w], w_in[:, RW_IN:], w_in[:, rkv_w:RW_IN],
                            jnp.zeros((D_MODEL, RW_TAIL_PAD - RW_TAIL), w_in.dtype)], axis=1).astype(BF16)
    p = _matmul(a_all, w_in, out_dtype=F32, tm=1024, tn=768, tk=D_MODEL, name="even_w_in")
    rw = _rwkv_mixer(p, rkv_w + att_w, *rw_args)
    assert rkv_w % att_w == 0
    qkv = _qkv_prep(p, att_w, rkv_w // att_w, rope, jnp.stack([q_g, k_g]),
                    ((0, GQ_HEADS, 0, True, GQ_SCALE * LOG2E), (GQ_HEADS, GQ_KV_HEADS, 1, True, 1.0),
                     (GQ_HEADS + GQ_KV_HEADS, GQ_KV_HEADS, None, False, 1.0)), GQ_HEAD // 4, "gqa_prep")
    at = _gqa_attention(qkv, latent=True, tq=512)
    if need_ctx:
        at = jnp.concatenate([at, _gqa_attention(qkv, latent=False)], axis=0)
    return rw, at


def _odd_mixer(a_all, rope, w_in, q_norm, q_up, kv_norm, kv_up, lq1, lk1, lq2, lk2, subln, layer_idx):
    w_in = jnp.concatenate([w_in[:, MLA_IN:], w_in[:, :MLA_IN],
                            jnp.zeros((D_MODEL, IN_ODD_PADDED - IN_ODD), w_in.dtype)], axis=1).astype(BF16)
    p = _matmul(a_all, w_in, out_dtype=F32, tm=1024, tn=768, tk=D_MODEL, name="odd_w_in")
    lam_init = 0.8 - 0.6 * math.exp(-0.3 * layer_idx)
    lam = (jnp.exp(jnp.sum(lq1 * lk1).astype(F32)) - jnp.exp(jnp.sum(lq2 * lk2).astype(F32)) + lam_init)

    pm = p[:, DIFF_IN:DIFF_IN + MLA_IN]
    c_q = _rms_f32(pm[:N_LAT_ROWS, :Q_LORA], q_norm).astype(BF16)
    c_kv = _rms_f32(pm[:, Q_LORA:Q_LORA + KV_LORA], kv_norm).astype(BF16)
    k_pe = pm[:, Q_LORA + KV_LORA:MLA_IN]
    q_up_h = q_up.reshape(Q_LORA, MLA_HEADS, MLA_NOPE + MLA_ROPE) * (MLA_SCALE * LOG2E)
    q_up_nope = q_up_h[:, :, :MLA_NOPE].reshape(Q_LORA, -1).astype(BF16)
    q_up_pe = q_up_h[:, :, MLA_NOPE:].reshape(Q_LORA, -1).astype(BF16)
    kv_up_h = kv_up.reshape(KV_LORA, MLA_HEADS, MLA_NOPE + MLA_V)
    kv_up_r = jnp.concatenate([kv_up_h[:, :, :MLA_NOPE].reshape(KV_LORA, -1),
                               kv_up_h[:, :, MLA_NOPE:].reshape(KV_LORA, -1)], axis=1).astype(BF16)
    q_nope = _matmul(c_q, q_up_nope, out_dtype=BF16, tm=1024, tn=2048, tk=Q_LORA, name="mla_q_up_nope")
    q_pe = _matmul(c_q, q_up_pe, out_dtype=F32, tm=1024, tn=1024, tk=Q_LORA, name="mla_q_up_rope")
    kv = _matmul(c_kv, kv_up_r, out_dtype=BF16, tm=1024, tn=2048, tk=KV_LORA, name="mla_kv_up")
    q_pe = _rope_lat(q_pe.reshape(N_LAT_ROWS, MLA_HEADS, MLA_ROPE), rope)
    q_pe = q_pe.reshape(N_LAT_ROWS, MLA_HEADS * MLA_ROPE).astype(BF16)
    k_pe = _rope_lat(k_pe.reshape(N_ROWS, 1, MLA_ROPE), rope).reshape(N_ROWS, MLA_ROPE).astype(BF16)
    k_pe = jnp.tile(k_pe, (1, MLA_HEADS_PER_STEP))
    m_out = _mla_attention(q_nope, q_pe, kv, k_pe, tq=512)

    dqkv = _qkv_prep(p, DIFF_IN, 0, rope, None,
                     ((0, DIFF_HEADS, None, True, DIFF_SCALE * LOG2E), (DIFF_HEADS, DIFF_HEADS, None, True, 1.0),
                      (2 * DIFF_HEADS, DIFF_HEADS, None, False, 1.0)), DIFF_HEAD // 4, "diff_prep")
    d_out = _diff_attention(lam.reshape(1), dqkv, subln, 1.0 - lam_init, tq=512)
    return m_out, d_out


def kernel(x, c, ctx, c_ctx, ada_w, ada_b, norm1_g, norm2_g, mlp_w1, mlp_w2, final_g, ev_w_in, ev_w_out, rw_mu, rw_w0, rw_w_up, rw_a0, rw_a_up, rw_g_up, rw_k_k, rw_k_a, rw_r_k, rw_ln_w, rw_ln_b, gq_q_norm, gq_k_norm, od_w_in, od_w_out, mla_q_norm, mla_q_up, mla_kv_norm, mla_kv_up, diff_lq1, diff_lk1, diff_lq2, diff_lk2, diff_subln):
    rope_gq = _axial_rope_tables(SEQ, GQ_HEAD)
    rope_64 = _axial_rope_tables(SEQ, MLA_ROPE)
    h = jnp.concatenate([x.reshape(N_LAT_ROWS, D_MODEL), ctx.reshape(N_CTX_ROWS, D_MODEL)], axis=0)
    cond = jnp.concatenate([c_ctx[None], c, jnp.zeros((3, D_MODEL), F32)], axis=0)
    cond = jax.nn.silu(cond).astype(BF16)
    for i in range(DEPTH):
        last = i == DEPTH - 1
        mod = _matmul(cond, ada_w, layer=i, out_dtype=F32, tm=8, tn=2048, tk=1024, name="adaln_mod")
        mod = (mod + ada_b[i])[:BATCH + 1].reshape(BATCH + 1, N_MOD, 1, D_MODEL)
        sh1, sc1, g1, sh2, sc2, g2 = (mod[:, m] for m in range(N_MOD))
        a_all = _norm(h, norm1_g[i], sc1, sh1, rows=N_ROWS, out_dtype=BF16)
        j = i // 2
        if i % 2 == 0:
            rw_args = (rw_mu[j], rw_w0[j], rw_w_up[j], rw_a0[j], rw_a_up[j], rw_g_up[j], rw_k_k[j], rw_k_a[j],
                       rw_r_k[j], rw_ln_w[j], rw_ln_b[j])
            mix = _even_mixer(a_all, rope_gq, ev_w_in[j], rw_args, gq_q_norm[j], gq_k_norm[j], not last)
            w_out = ev_w_out[j]
        else:
            if not last:
                raise NotImplementedError("context rows of an odd layer are only needed when a layer follows")
            mix = _odd_mixer(a_all, rope_64, od_w_in[j], mla_q_norm[j], mla_q_up[j], mla_kv_norm[j],
                             mla_kv_up[j], diff_lq1[j], diff_lk1[j], diff_lq2[j], diff_lk2[j], diff_subln[j], i)
            w_out = od_w_out[j]
        rows = N_LAT_ROWS if last else N_ROWS
        h = _matmul(mix, w_out.astype(BF16), out_dtype=F32, tm=1024, tn=1024, tk=2048, rows=rows,
                    epilogue="gated_residual", res=h, gate=g1, name="mixer_w_out")
        a2 = _norm(h, norm2_g[i], sc2, sh2, rows=rows, out_dtype=BF16)
        hid = _matmul(a2, mlp_w1, layer=i, out_dtype=BF16, tm=1024, tn=512, tk=D_MODEL,
                      epilogue="relu2", name="mlp_w1")
        h = _matmul(hid, mlp_w2, layer=i, out_dtype=F32, tm=1024, tn=1024, tk=2048,
                    epilogue="gated_residual", res=h, gate=g2, name="mlp_w2")
    out = _norm(h, final_g, rows=N_LAT_ROWS, out_dtype=F32)
    return out.reshape(BATCH, SEQ, D_MODEL)
```

```python
import functools
import math

import jax
import jax.numpy as jnp
from jax import lax
from jax.experimental import pallas as pl
from jax.experimental.pallas import tpu as pltpu

D_MODEL = 4096
BATCH = 4
SEQ = 4096
DEPTH = 2
CTX_LEN = 256
GRID_W = 64
ROPE_THETA = 10000.0
NORM_EPS = 1e-6
N_MOD = 6
HALF = D_MODEL // 2

RW_HEAD = 64
RW_W = HALF
RW_HEADS = RW_W // RW_HEAD
DECAY_RANK = 96
AAA_RANK = 96
GATE_RANK = 256
GN_EPS = 64e-5
RW_IN = 3 * RW_W + GATE_RANK + 2 * DECAY_RANK + 2 * AAA_RANK

GQ_HEAD = 128
GQ_HEADS = HALF // GQ_HEAD
GQ_KV_HEADS = GQ_HEADS // 4
GQ_GROUP = GQ_HEADS // GQ_KV_HEADS
GQ_SCALE = GQ_HEAD ** -0.5

MLA_NOPE = 128
MLA_ROPE = 64
MLA_V = 128
MLA_HEADS = HALF // MLA_V
Q_LORA = 768
KV_LORA = 512
MLA_SCALE = (MLA_NOPE + MLA_ROPE) ** -0.5
MLA_IN = Q_LORA + KV_LORA + MLA_ROPE

DIFF_HEAD = 64
DIFF_V = 2 * DIFF_HEAD
DIFF_HEADS = HALF // DIFF_V
DIFF_SCALE = DIFF_HEAD ** -0.5
DIFF_IN = DIFF_HEADS * (4 * DIFF_HEAD + DIFF_V)
IN_ODD = MLA_IN + DIFF_IN
IN_ODD_PADDED = 7680

N_LAT_ROWS = BATCH * SEQ
N_CTX_ROWS = BATCH * CTX_LEN
N_ROWS = N_LAT_ROWS + N_CTX_ROWS
CTX_BLOCK0 = N_LAT_ROWS // CTX_LEN

V7X_VMEM_LIMIT_BYTES = 56 * 1024 * 1024

BF16 = jnp.bfloat16
F32 = jnp.float32


def _params(*sem):
    return pltpu.CompilerParams(dimension_semantics=sem, vmem_limit_bytes=V7X_VMEM_LIMIT_BYTES)


def _row_group(row_tile, tile_rows):
    start = row_tile * tile_rows
    return jnp.where(start < N_LAT_ROWS, 1 + start // SEQ, 0)


def _split_rows_specs(block, tile_rows, col):
    n_lat = N_LAT_ROWS // tile_rows

    def lat_map(i, *rest):
        used = i < n_lat
        return jnp.where(used, i, n_lat - 1), jnp.where(used, col(i, *rest), 0)

    def ctx_map(i, *rest):
        used = i >= n_lat
        return jnp.where(used, i - n_lat, 0), jnp.where(used, col(i, *rest), 0)

    return [pl.BlockSpec(block, lat_map), pl.BlockSpec(block, ctx_map)]


def _pick_rows(lat_ref, ctx_ref, tile_rows):
    return jnp.where(pl.program_id(0) * tile_rows < N_LAT_ROWS, lat_ref[...], ctx_ref[...])


def _norm_kernel(*refs, modulate, split):
    x_refs, (g_ref, *rest) = refs[:1 + split], refs[1 + split:]
    if modulate:
        sc_ref, sh_ref, o_ref = rest
    else:
        (o_ref,) = rest
    x = _pick_rows(*x_refs, o_ref.shape[0]) if split else x_refs[0][...]
    ms = jnp.mean(x * x, axis=-1, keepdims=True)
    y = x * lax.rsqrt(ms + NORM_EPS) * g_ref[...]
    if modulate:
        y = y * (1.0 + sc_ref[0]) + sh_ref[0]
    o_ref[...] = y.astype(o_ref.dtype)


def _norm(x, g, sc=None, sh=None, *, rows, out_dtype, tile=256):
    split = isinstance(x, tuple)
    x_parts = x if split else (x,)
    d = x_parts[0].shape[1]
    modulate = sc is not None
    x_specs = (_split_rows_specs((tile, d), tile, lambda i: 0) if split else
               [pl.BlockSpec((tile, d), lambda i: (i, 0))])
    in_specs = x_specs + [pl.BlockSpec((1, d), lambda i: (0, 0))]
    args = [*x_parts, g.reshape(1, d)]
    if modulate:
        mod_spec = pl.BlockSpec((1, 1, d), lambda i: (_row_group(i, tile), 0, 0))
        in_specs += [mod_spec, mod_spec]
        args += [sc, sh]
    return pl.pallas_call(
        functools.partial(_norm_kernel, modulate=modulate, split=split),
        grid=(rows // tile,),
        in_specs=in_specs,
        out_specs=pl.BlockSpec((tile, d), lambda i: (i, 0)),
        out_shape=jax.ShapeDtypeStruct((rows, d), out_dtype),
        compiler_params=_params("parallel"),
        name="rmsnorm_mod" if modulate else "rmsnorm",
    )(*args)


def _mm_kernel(*refs, epilogue, nk, n_a, n_res):
    a_refs, w_ref, rest = refs[:n_a], refs[n_a], refs[n_a + 1:]
    if epilogue == "gated_residual":
        res_refs, (gate_ref, o_ref, acc_ref) = rest[:n_res], rest[n_res:]
    else:
        o_ref, acc_ref = rest
    k = pl.program_id(2)

    def product(a_ref):
        return jnp.dot(a_ref[...].astype(BF16), w_ref[...].astype(BF16), preferred_element_type=F32)

    def store(part, first):
        if first:
            acc_ref[...] = part
        else:
            acc_ref[...] += part

    if n_a == 1:
        part = product(a_refs[0])
        pl.when(k == 0)(lambda: store(part, True))
        pl.when(k > 0)(lambda: store(part, False))
    else:
        for idx, a_ref in enumerate(a_refs):
            pl.when(k == idx)(lambda a_ref=a_ref, idx=idx: store(product(a_ref), idx == 0))

    @pl.when(k == nk - 1)
    def _():
        acc = acc_ref[...]
        if epilogue == "relu2":
            acc = jnp.square(jnp.maximum(acc, 0.0))
        elif epilogue == "gated_residual":
            res = _pick_rows(*res_refs, o_ref.shape[0]) if n_res == 2 else res_refs[0][...]
            acc = res + gate_ref[0] * acc
        o_ref[...] = acc.astype(o_ref.dtype)


def _matmul(a, w, *, out_dtype, tm, tn, tk, rows=None, layer=None, epilogue=None, res=None, gate=None, name):
    kdim, n = w.shape[-2:]
    a_parts = a if isinstance(a, tuple) else (a,)
    m = a_parts[0].shape[0] if rows is None else rows
    assert sum(part.shape[1] for part in a_parts) == kdim and m % tm == 0 and n % tn == 0 and kdim % tk == 0
    assert (w.ndim == 3) == (layer is not None)
    nk = kdim // tk
    w_spec = (pl.BlockSpec((tk, tn), lambda i, j, k: (k, j)) if layer is None else
              pl.BlockSpec((None, tk, tn), lambda i, j, k: (layer, k, j)))
    if len(a_parts) == 1:
        in_specs = [pl.BlockSpec((tm, tk), lambda i, j, k: (i, k)), w_spec]
    else:
        assert len(a_parts) == nk and all(part.shape[1] == tk for part in a_parts)
        in_specs = [pl.BlockSpec((tm, tk), lambda i, j, k: (i, 0))] * nk + [w_spec]
    args = [*a_parts, w]
    res_parts = res if isinstance(res, tuple) else (res,)
    if epilogue == "gated_residual":
        in_specs += (_split_rows_specs((tm, tn), tm, lambda i, j, k: j) if len(res_parts) == 2 else
                     [pl.BlockSpec((tm, tn), lambda i, j, k: (i, j))])
        in_specs.append(pl.BlockSpec((1, 1, tn), lambda i, j, k: (_row_group(i, tm), 0, j)))
        args += [*res_parts, gate]
    return pl.pallas_call(
        functools.partial(_mm_kernel, epilogue=epilogue, nk=nk, n_a=len(a_parts), n_res=len(res_parts)),
        grid=(m // tm, n // tn, nk),
        in_specs=in_specs,
        out_specs=pl.BlockSpec((tm, tn), lambda i, j, k: (i, j)),
        out_shape=jax.ShapeDtypeStruct((m, n), out_dtype),
        scratch_shapes=[pltpu.VMEM((tm, tn), F32)],
        compiler_params=_params("parallel", "parallel", "arbitrary"),
        name=name,
    )(*args)


def _nt_dot(a, b):
    return lax.dot_general(a, b, (((1,), (1,)), ((), ())), preferred_element_type=F32)


LOG2E = math.log2(math.e)


def _fill_values_and_ones(v_ref, vo_ref):
    d = v_ref.shape[1]
    vo_ref[:, :d] = v_ref[...]
    vo_ref[:, d:] = jnp.ones_like(v_ref)


def _softmax_times_values(s_list, vo_refs):
    d = vo_refs[0].shape[1] // 2
    m = functools.reduce(jnp.maximum, [jnp.max(s, axis=-1, keepdims=True) for s in s_list])
    acc = functools.reduce(jnp.add, [jnp.dot(jnp.exp2(s - m).astype(BF16), vo_ref[...], preferred_element_type=F32)
                                     for s, vo_ref in zip(s_list, vo_refs)])
    return acc[:, :d] / acc[:, d:]


def _attend_each(items):
    outs = []
    pending = None
    for q, k_refs, vo_refs in items:
        s_list = [_nt_dot(q, k_ref[...]) for k_ref in k_refs]
        if pending is not None:
            outs.append(_softmax_times_values(*pending))
        pending = (s_list, vo_refs)
    outs.append(_softmax_times_values(*pending))
    return outs


def _gqa_kernel(q_ref, *rest, with_lat):
    if with_lat:
        kc_ref, vc_ref, kl_ref, vl_ref, o_ref, voc_ref, vol_ref = rest
        k_refs, v_refs, vo_refs = (kc_ref, kl_ref), (vc_ref, vl_ref), (voc_ref, vol_ref)
    else:
        kc_ref, vc_ref, o_ref, voc_ref = rest
        k_refs, v_refs, vo_refs = (kc_ref,), (vc_ref,), (voc_ref,)

    @pl.when(pl.program_id(2) == 0)
    def _():
        for v_ref, vo_ref in zip(v_refs, vo_refs):
            _fill_values_and_ones(v_ref, vo_ref)

    outs = _attend_each([(q_ref[:, g * GQ_HEAD:(g + 1) * GQ_HEAD], k_refs, vo_refs) for g in range(GQ_GROUP)])
    for g, o in enumerate(outs):
        o_ref[:, g * GQ_HEAD:(g + 1) * GQ_HEAD] = o.astype(o_ref.dtype)


def _gqa_attention(qkv, *, latent, tq=256):
    gw = GQ_GROUP * GQ_HEAD
    k0, v0 = GQ_HEADS, GQ_HEADS + GQ_KV_HEADS
    ctx_k = pl.BlockSpec((CTX_LEN, GQ_HEAD), lambda b, n, i: (CTX_BLOCK0 + b, k0 + n))
    ctx_v = pl.BlockSpec((CTX_LEN, GQ_HEAD), lambda b, n, i: (CTX_BLOCK0 + b, v0 + n))
    scratch = [pltpu.VMEM((CTX_LEN, 2 * GQ_HEAD), BF16)]
    if latent:
        per_b = SEQ // tq
        lat_k = pl.BlockSpec((SEQ, GQ_HEAD), lambda b, n, i: (b, k0 + n))
        lat_v = pl.BlockSpec((SEQ, GQ_HEAD), lambda b, n, i: (b, v0 + n))
        in_specs = [pl.BlockSpec((tq, gw), lambda b, n, i: (b * per_b + i, n)), ctx_k, ctx_v, lat_k, lat_v]
        args = (qkv,) * 5
        rows = N_LAT_ROWS
        scratch.append(pltpu.VMEM((SEQ, 2 * GQ_HEAD), BF16))
    else:
        per_b = CTX_LEN // tq
        q_blk0 = N_LAT_ROWS // tq
        in_specs = [pl.BlockSpec((tq, gw), lambda b, n, i: (q_blk0 + b * per_b + i, n)), ctx_k, ctx_v]
        args = (qkv,) * 3
        rows = N_CTX_ROWS
    return pl.pallas_call(
        functools.partial(_gqa_kernel, with_lat=latent),
        grid=(BATCH, GQ_KV_HEADS, per_b),
        in_specs=in_specs,
        out_specs=pl.BlockSpec((tq, gw), lambda b, n, i: (b * per_b + i, n)),
        out_shape=jax.ShapeDtypeStruct((rows, GQ_HEADS * GQ_HEAD), BF16),
        scratch_shapes=scratch,
        compiler_params=_params("parallel", "parallel", "arbitrary"),
        name="gqa_attention_lat" if latent else "gqa_attention_ctx",
    )(*args)


def _mla_kernel(qn_ref, qp_ref, knc_ref, kpc_ref, vc_ref, knl_ref, kpl_ref, vl_ref, o_ref,
                kc_ref, kl_ref, voc_ref, vol_ref):
    heads = range(MLA_HEADS_PER_STEP)
    cols = [slice(j * MLA_NOPE, (j + 1) * MLA_NOPE) for j in heads]

    def own_rope_lanes(x, j):
        lane = lax.broadcasted_iota(jnp.int32, x.shape, 1)
        return jnp.where(lane // MLA_ROPE == j, x, jnp.zeros_like(x))

    @pl.when(pl.program_id(2) == 0)
    def _():
        for kn_ref, kp_ref, v_ref, k_ref, vo_ref in ((knc_ref, kpc_ref, vc_ref, kc_ref, voc_ref),
                                                     (knl_ref, kpl_ref, vl_ref, kl_ref, vol_ref)):
            for j in heads:
                k_ref[j, :, :MLA_NOPE] = kn_ref[:, cols[j]]
                k_ref[j, :, MLA_NOPE:] = own_rope_lanes(kp_ref[...], j)
                vo_ref[j, :, :MLA_V] = v_ref[:, cols[j]]
                vo_ref[j, :, MLA_V:] = jnp.ones_like(kp_ref)

    qp = qp_ref[...]
    items = [(jnp.concatenate([qn_ref[:, cols[j]], own_rope_lanes(qp, j)], axis=1), (kc_ref.at[j], kl_ref.at[j]),
              (voc_ref.at[j], vol_ref.at[j])) for j in heads]
    for j, o in enumerate(_attend_each(items)):
        o_ref[:, cols[j]] = o.astype(o_ref.dtype)


MLA_HEADS_PER_STEP = 2


def _mla_attention(q_nope, q_pe, kv, k_pe, *, tq=256):
    per_b = SEQ // tq
    hs = MLA_HEADS_PER_STEP
    assert hs * MLA_ROPE == MLA_NOPE
    w = hs * MLA_NOPE
    v_blk0 = MLA_HEADS // hs
    in_specs = [
        pl.BlockSpec((tq, w), lambda b, h, i: (b * per_b + i, h)),
        pl.BlockSpec((tq, hs * MLA_ROPE), lambda b, h, i: (b * per_b + i, h)),
        pl.BlockSpec((CTX_LEN, w), lambda b, h, i: (CTX_BLOCK0 + b, h)),
        pl.BlockSpec((CTX_LEN, MLA_NOPE), lambda b, h, i: (CTX_BLOCK0 + b, 0)),
        pl.BlockSpec((CTX_LEN, w), lambda b, h, i: (CTX_BLOCK0 + b, v_blk0 + h)),
        pl.BlockSpec((SEQ, w), lambda b, h, i: (b, h)),
        pl.BlockSpec((SEQ, MLA_NOPE), lambda b, h, i: (b, 0)),
        pl.BlockSpec((SEQ, w), lambda b, h, i: (b, v_blk0 + h)),
    ]
    return pl.pallas_call(
        _mla_kernel,
        grid=(BATCH, MLA_HEADS // hs, per_b),
        in_specs=in_specs,
        out_specs=pl.BlockSpec((tq, w), lambda b, h, i: (b * per_b + i, h)),
        out_shape=jax.ShapeDtypeStruct((N_LAT_ROWS, MLA_HEADS * MLA_V), BF16),
        scratch_shapes=[pltpu.VMEM((hs, CTX_LEN, 2 * MLA_NOPE), BF16), pltpu.VMEM((hs, SEQ, 2 * MLA_NOPE), BF16),
                        pltpu.VMEM((hs, CTX_LEN, 2 * MLA_V), BF16), pltpu.VMEM((hs, SEQ, 2 * MLA_V), BF16)],
        compiler_params=_params("parallel", "parallel", "arbitrary"),
        name="mla_attention",
    )(q_nope, q_pe, kv, k_pe, kv, kv, k_pe, kv)


def _diff_kernel(lam_ref, q_ref, kc_ref, vc_ref, kl_ref, vl_ref, g_ref, o_ref, voc_ref, vol_ref, *, out_scale):
    heads = range(DIFF_HEADS_PER_STEP)
    cols = [slice(j * DIFF_V, (j + 1) * DIFF_V) for j in heads]

    @pl.when(pl.program_id(2) == 0)
    def _():
        for v_ref, vo_ref in ((vc_ref, voc_ref), (vl_ref, vol_ref)):
            for j in heads:
                vo_ref[j, :, :DIFF_V] = v_ref[:, cols[j]]
                vo_ref[j, :, DIFF_V:] = jnp.ones((v_ref.shape[0], DIFF_V), BF16)

    first = lax.broadcasted_iota(jnp.int32, (q_ref.shape[0], DIFF_V), 1) < DIFF_HEAD
    items = []
    for j in heads:
        q = q_ref[:, cols[j]]
        zero = jnp.zeros_like(q)
        keys = (kc_ref[:, cols[j]], kl_ref[:, cols[j]])
        for qh in (jnp.where(first, q, zero), jnp.where(first, zero, q)):
            items.append((qh, keys, (voc_ref.at[j], vol_ref.at[j])))
    outs = _attend_each(items)
    for j in heads:
        o = outs[2 * j] - lam_ref[0] * outs[2 * j + 1]
        ms = jnp.mean(o * o, axis=-1, keepdims=True)
        o = o * lax.rsqrt(ms + NORM_EPS) * g_ref[...]
        o_ref[:, cols[j]] = (o * out_scale).astype(o_ref.dtype)


DIFF_HEADS_PER_STEP = 2


def _diff_attention(lam, qkv, subln, out_scale, *, tq=256):
    per_b = SEQ // tq
    hs = DIFF_HEADS_PER_STEP
    w = hs * DIFF_V
    k0, v0 = DIFF_HEADS // hs, 2 * DIFF_HEADS // hs
    ctx_k = pl.BlockSpec((CTX_LEN, w), lambda b, h, i: (CTX_BLOCK0 + b, k0 + h))
    ctx_v = pl.BlockSpec((CTX_LEN, w), lambda b, h, i: (CTX_BLOCK0 + b, v0 + h))
    lat_k = pl.BlockSpec((SEQ, w), lambda b, h, i: (b, k0 + h))
    lat_v = pl.BlockSpec((SEQ, w), lambda b, h, i: (b, v0 + h))
    in_specs = [
        pl.BlockSpec(memory_space=pltpu.SMEM),
        pl.BlockSpec((tq, w), lambda b, h, i: (b * per_b + i, h)),
        ctx_k, ctx_v, lat_k, lat_v,
        pl.BlockSpec((1, DIFF_V), lambda b, h, i: (0, 0)),
    ]
    return pl.pallas_call(
        functools.partial(_diff_kernel, out_scale=out_scale),
        grid=(BATCH, DIFF_HEADS // hs, per_b),
        in_specs=in_specs,
        out_specs=pl.BlockSpec((tq, w), lambda b, h, i: (b * per_b + i, h)),
        out_shape=jax.ShapeDtypeStruct((N_LAT_ROWS, DIFF_HEADS * DIFF_V), BF16),
        scratch_shapes=[pltpu.VMEM((hs, CTX_LEN, 2 * DIFF_V), BF16), pltpu.VMEM((hs, SEQ, 2 * DIFF_V), BF16)],
        compiler_params=_params("parallel", "parallel", "arbitrary"),
        name="diff_attention",
    )(lam, qkv, qkv, qkv, qkv, qkv, subln.reshape(1, DIFF_V))


RW_CHUNK = 128
RW_PAIRS = RW_HEADS // 2
LANES = 2 * RW_HEAD
SUB = 8
RW_UNROLL = 16
RW_GROUP = 8


def _rwkv_kernel(r_ref, kk_ref, v_ref, w_ref, k_ref, b_ref, y_out, s_ref, y_ref):
    d = pl.program_id(0)
    c = pl.program_id(2)

    @pl.when(c == 0)
    def _():
        s_ref[...] = jnp.zeros_like(s_ref)

    v_hi = lax.broadcasted_iota(jnp.int32, (SUB, SUB, LANES), 0)
    v_lo = lax.broadcasted_iota(jnp.int32, (SUB, SUB, LANES), 1)
    lane3 = lax.broadcasted_iota(jnp.int32, (SUB, SUB, LANES), 2)
    dup = jnp.where((lane3 & (RW_HEAD - 1)) == v_hi * SUB + v_lo, 1.0, 0.0)
    blk_r = lax.broadcasted_iota(jnp.int32, (LANES, LANES), 0) // RW_HEAD
    blk_c = lax.broadcasted_iota(jnp.int32, (LANES, LANES), 1) // RW_HEAD
    head_ones = jnp.where(blk_r == blk_c, 1.0, 0.0).astype(BF16)
    slot_of_lane = lax.broadcasted_iota(jnp.int32, (SUB, LANES), 1) & (RW_HEAD - 1)
    step_dir = jnp.where(d == 0, 1, -1)

    y_ref[...] = jnp.zeros_like(y_ref)

    def row(ref, p, t, *lead):
        return ref[(*lead, p, pl.ds(t, SUB, stride=0), slice(None))]

    def as_rows(tile):
        return tile.reshape(RW_HEAD, LANES).astype(BF16)

    def head_sums(states, pairs, t_done, t_next):
        rows = []
        for p in pairs:
            rows.append(as_rows(states[p] * row(r_ref, p, t_done)))
            if t_next is not None:
                rows.append(as_rows(states[p] * row(kk_ref, p, t_next)))
                rows.append(as_rows(dup * row(v_ref, p, t_next)))
        return jnp.dot(jnp.concatenate(rows, axis=0), head_ones, preferred_element_type=F32)

    def slab(sums, q, n_slabs, i):
        lo = (q * n_slabs + i) * RW_HEAD
        return sums[lo:lo + RW_HEAD].reshape(SUB, SUB, LANES)

    def put_y(p, y_b, t, valid):
        window = t // RW_HEAD
        keep = (slot_of_lane == t % RW_HEAD) & valid
        y_ref[0, p, window] = jnp.where(keep, y_b, y_ref[0, p, window])

    groups = [range(g0, g0 + RW_GROUP) for g0 in range(0, RW_PAIRS, RW_GROUP)]

    def block(g, carry):
        first = g * RW_UNROLL
        first = jnp.where(d == 0, first, RW_CHUNK - 1 - first)
        times = [first + u * step_dir for u in range(RW_UNROLL)]
        t_before = jnp.clip(first - step_dir, 0, RW_CHUNK - 1)
        states = [s_ref[p] for p in range(RW_PAIRS)]
        sums = [head_sums(states, grp, t_before, times[0]) for grp in groups]
        for gi, grp in enumerate(groups):
            for q, p in enumerate(grp):
                put_y(p, slab(sums[gi], q, 3, 0), t_before, g > 0)
        for u in range(RW_UNROLL):
            t = times[u]
            for gi, grp in enumerate(groups):
                for q, p in enumerate(grp):
                    sa = slab(sums[gi], q, 3, 1)
                    vb = slab(sums[gi], q, 3, 2)
                    states[p] = (states[p] * row(w_ref, p, t, 0) - sa * row(b_ref, p, t, 0)
                                 + vb * row(k_ref, p, t, 0))
                if u + 1 < RW_UNROLL:
                    sums[gi] = head_sums(states, grp, t, times[u + 1])
                    for q, p in enumerate(grp):
                        put_y(p, slab(sums[gi], q, 3, 0), t, True)
        for p in range(RW_PAIRS):
            s_ref[p] = states[p]
        return carry

    lax.fori_loop(0, RW_CHUNK // RW_UNROLL, block, 0)

    t_last = jnp.where(d == 0, RW_CHUNK - 1, 0)
    states = [s_ref[p] for p in range(RW_PAIRS)]
    for grp in groups:
        last = head_sums(states, grp, t_last, None)
        for q, p in enumerate(grp):
            put_y(p, slab(last, q, 1, 0), t_last, True)

    first_head = lax.broadcasted_iota(jnp.int32, (RW_HEAD, LANES), 1) < RW_HEAD
    for p in range(0, RW_PAIRS, 2):
        for window in range(RW_CHUNK // RW_HEAD):
            both = jnp.concatenate([y_ref[0, p, window].reshape(RW_HEAD, LANES),
                                    y_ref[0, p + 1, window].reshape(RW_HEAD, LANES)], axis=0)
            by_row = both.T
            head0, head1 = by_row[:RW_HEAD], by_row[RW_HEAD:]
            rows = slice(window * RW_HEAD, (window + 1) * RW_HEAD)
            y_out[0, rows, p * LANES:(p + 1) * LANES] = jnp.where(
                first_head, head0, pltpu.roll(head1, RW_HEAD, axis=1))
            y_out[0, rows, (p + 1) * LANES:(p + 2) * LANES] = jnp.where(
                first_head, pltpu.roll(head0, RW_HEAD, axis=1), head1)


def _rwkv_scan(r, kk, v, w, k, b):
    n_ctx = CTX_LEN // RW_CHUNK
    n_lat = SEQ // RW_CHUNK
    ctx0 = N_LAT_ROWS // RW_CHUNK

    def chunk(d, bb, c):
        j = jnp.where(d == 0, c, jnp.where(c < n_ctx, n_ctx - 1 - c, n_ctx + n_lat - 1 - (c - n_ctx)))
        return jnp.where(j < n_ctx, ctx0 + bb * n_ctx + j, bb * n_lat + j - n_ctx)

    shared = pl.BlockSpec((RW_PAIRS, RW_CHUNK, LANES), lambda d, bb, c: (0, chunk(d, bb, c), 0))
    per_dir = pl.BlockSpec((1, RW_PAIRS, RW_CHUNK, LANES), lambda d, bb, c: (d, 0, chunk(d, bb, c), 0))
    return pl.pallas_call(
        _rwkv_kernel,
        grid=(2, BATCH, n_ctx + n_lat),
        in_specs=[shared, shared, shared, per_dir, per_dir, per_dir],
        out_specs=pl.BlockSpec((1, RW_CHUNK, RW_W), lambda d, bb, c: (d, chunk(d, bb, c), 0)),
        out_shape=jax.ShapeDtypeStruct((2, N_ROWS, RW_W), F32),
        scratch_shapes=[pltpu.VMEM((RW_PAIRS, RW_HEAD // SUB, SUB, LANES), F32),
                        pltpu.VMEM((1, RW_PAIRS, RW_CHUNK // RW_HEAD, SUB, SUB, LANES), F32)],
        compiler_params=_params("parallel", "parallel", "arbitrary"),
        name="rwkv7_scan",
    )(r, kk, v, w, k, b)


def _axial_rope_tables(n_tokens, rot_dim):
    n_rows = n_tokens // GRID_W
    row = jnp.repeat(jnp.arange(n_rows, dtype=F32), GRID_W)
    col = jnp.tile(jnp.arange(GRID_W, dtype=F32), n_rows)
    axis_dim = rot_dim // 2
    inv_freq = ROPE_THETA ** (-jnp.arange(0, axis_dim, 2, dtype=F32) / axis_dim)
    ang_r = row[:, None] * inv_freq
    ang_c = col[:, None] * inv_freq
    ang = jnp.concatenate([ang_r, ang_r, ang_c, ang_c], axis=-1)
    return jnp.cos(ang), jnp.sin(ang)


def _rotate_half(z):
    z1, z2 = jnp.split(z, 2, axis=-1)
    return jnp.concatenate([-z2, z1], axis=-1)


def _apply_rope(x, cos, sin):
    half = x.shape[-1] // 2
    rot = jnp.concatenate([_rotate_half(x[..., :half]), _rotate_half(x[..., half:])], axis=-1)
    return x * cos + rot * sin


def _rms_f32(x, g):
    return x * lax.rsqrt(jnp.mean(x * x, axis=-1, keepdims=True) + NORM_EPS) * g


def _rope_lat(x, rope):
    cos, sin = rope
    heads, dim = x.shape[1:]
    lat = x[:N_LAT_ROWS].reshape(BATCH, SEQ, heads, dim)
    lat = _apply_rope(lat, cos[None, :, None, :], sin[None, :, None, :]).reshape(N_LAT_ROWS, heads, dim)
    return lat if x.shape[0] == N_LAT_ROWS else jnp.concatenate([lat, x[N_LAT_ROWS:]], axis=0)


RW_SLAB = 512
RW_TAIL = GATE_RANK + 2 * DECAY_RANK + 2 * AAA_RANK
RW_TAIL_PAD = 768
RW_ROW_TILE = 256


def _head_lane_sums(x):
    rows = x.shape[0]
    blk_r = lax.broadcasted_iota(jnp.int32, (LANES, LANES), 0) // RW_HEAD
    blk_c = lax.broadcasted_iota(jnp.int32, (LANES, LANES), 1) // RW_HEAD
    head_ones = jnp.where(blk_r == blk_c, 1.0, 0.0).astype(BF16)
    hi = x.astype(BF16)
    rest = x - hi.astype(F32)
    mid = rest.astype(BF16)
    lo = (rest - mid.astype(F32)).astype(BF16)
    pieces = jnp.concatenate([hi, mid, lo], axis=0)
    out = []
    for s in range(x.shape[1] // LANES):
        part = jnp.dot(pieces[:, s * LANES:(s + 1) * LANES], head_ones, preferred_element_type=F32)
        out.append(part[:rows] + part[rows:2 * rows] + part[2 * rows:])
    return jnp.concatenate(out, axis=1)


def _rwkv_prep_kernel(*refs):
    slabs = [refs[4 * n:4 * n + 4] for n in range(4)]
    (up_w_ref, w0_ref, a0_ref, kk_gain_ref, ka_ref, rk_ref,
     r_out, v_out, kk_out, decay_out, kdir_out, b_out, g_out, bonus_out) = refs[16:30]
    stage_refs = refs[30:34]
    tile = RW_ROW_TILE
    row0 = pl.program_id(0) * tile
    seq_len = jnp.where(row0 < N_LAT_ROWS, SEQ, CTX_LEN)
    starts_seq = row0 % seq_len == 0
    ends_seq = (row0 + tile) % seq_len == 0

    def shifted(main_ref, before_ref, after_ref, mu_ref, stage_ref):
        x = main_ref[...]
        stage_ref[SUB:SUB + tile, :] = x
        stage_ref[SUB - 1:SUB, :] = jnp.where(starts_seq, 0.0, before_ref[SUB - 1:SUB, :])
        stage_ref[SUB + tile:SUB + tile + 1, :] = jnp.where(ends_seq, 0.0, after_ref[0:1, :])
        around = 0.5 * (stage_ref[SUB - 1:SUB - 1 + tile, :] + stage_ref[SUB + 1:SUB + 1 + tile, :])
        return x + (around - x) * mu_ref[...]

    def put_pairs(out_ref, x, *lead):
        for q in range(RW_SLAB // LANES):
            out_ref[(*lead, q)] = x[:, q * LANES:(q + 1) * LANES]

    r, k, v, tail = (shifted(*slab, stage) for slab, stage in zip(slabs, stage_refs))
    put_pairs(r_out, r)
    put_pairs(v_out, v)
    s_w = RW_SLAB
    mid = tail[:, GATE_RANK:2 * GATE_RANK]
    mid_lane = lax.broadcasted_iota(jnp.int32, (1, GATE_RANK), 1)
    act = jnp.concatenate([jax.nn.sigmoid(tail[:, :GATE_RANK]),
                           jnp.where(mid_lane < 2 * DECAY_RANK, jnp.tanh(mid), mid),
                           tail[:, 2 * GATE_RANK:]], axis=1)
    up = jnp.dot(act.astype(BF16), up_w_ref[...], preferred_element_type=F32)
    g_out[...] = up[:, :s_w]
    kq = k * kk_gain_ref[...]
    kk = kq / jnp.maximum(jnp.sqrt(_head_lane_sums(kq * kq)), 1e-12)
    put_pairs(kk_out, kk)
    kdir_sum = jnp.zeros_like(k)
    for d in range(2):
        x = w0_ref[d] + up[:, (1 + d) * s_w:(2 + d) * s_w]
        put_pairs(decay_out, jnp.exp(jax.nn.sigmoid(x) * -math.exp(-0.5)), d)
        a = jax.nn.sigmoid(a0_ref[d] + up[:, (3 + d) * s_w:(4 + d) * s_w])
        kd = k * (1.0 + (a - 1.0) * ka_ref[...])
        put_pairs(kdir_out, kd, d)
        put_pairs(b_out, kk * a, d)
        kdir_sum = kdir_sum + kd
    bonus_out[...] = _head_lane_sums(r * kdir_sum * rk_ref[...]) * v


def _rwkv_post_kernel(y_ref, g_ref, bonus_ref, lnw_ref, lnb_ref, o_ref):
    y = y_ref[0] + y_ref[1]
    mean = _head_lane_sums(y) * (1.0 / RW_HEAD)
    cen = y - mean
    var = _head_lane_sums(cen * cen) * (1.0 / RW_HEAD)
    yn = cen * lax.rsqrt(var + GN_EPS) * lnw_ref[...] + lnb_ref[...]
    o_ref[...] = ((yn + bonus_ref[...]) * g_ref[...]).astype(o_ref.dtype)


def _rwkv_mixer(p, tail_col, mu, w0, w_up, a0, a_up, g_up, k_k, k_a, r_k, ln_w, ln_b):
    assert tail_col % RW_TAIL_PAD == 0 and p.shape[1] >= tail_col + RW_TAIL_PAD
    n_slab = RW_W // RW_SLAB
    up_w = jnp.zeros((RW_TAIL_PAD, 5, RW_W), F32)
    o = GATE_RANK
    up_w = up_w.at[:o, 0].set(g_up)
    for d in range(2):
        up_w = up_w.at[o + d * DECAY_RANK:o + (d + 1) * DECAY_RANK, 1 + d].set(w_up[d])
    o += 2 * DECAY_RANK
    for d in range(2):
        up_w = up_w.at[o + d * AAA_RANK:o + (d + 1) * AAA_RANK, 3 + d].set(a_up[d])
    up_w = up_w.reshape(RW_TAIL_PAD, 5, n_slab, RW_SLAB).transpose(2, 0, 1, 3)
    up_w = up_w.reshape(n_slab, RW_TAIL_PAD, 5 * RW_SLAB).astype(BF16)
    mu_main = mu[:3 * RW_W].reshape(1, 3 * RW_W)
    mu_tail = jnp.pad(mu[3 * RW_W:], (0, RW_TAIL_PAD - RW_TAIL)).reshape(1, RW_TAIL_PAD)

    tile = RW_ROW_TILE
    col_blocks = RW_W // RW_SLAB
    halo = tile // SUB
    last_halo = N_ROWS // SUB - 1

    def with_halo(width, col):
        return [pl.BlockSpec((tile, width), lambda i, j: (i, col(j))),
                pl.BlockSpec((SUB, width), lambda i, j: (jnp.maximum(i * halo - 1, 0), col(j))),
                pl.BlockSpec((SUB, width), lambda i, j: (jnp.minimum((i + 1) * halo, last_halo), col(j)))]

    in_specs, args = [], []
    for n in range(3):
        in_specs += with_halo(RW_SLAB, lambda j, n=n: n * col_blocks + j)
        in_specs.append(pl.BlockSpec((1, RW_SLAB), lambda i, j, n=n: (0, n * col_blocks + j)))
        args += [p, p, p, mu_main]
    tail_blk = tail_col // RW_TAIL_PAD
    in_specs += with_halo(RW_TAIL_PAD, lambda j: tail_blk)
    in_specs.append(pl.BlockSpec((1, RW_TAIL_PAD), lambda i, j: (0, 0)))
    args += [p, p, p, mu_tail]

    vec = pl.BlockSpec((1, RW_SLAB), lambda i, j: (0, j))
    vec2 = pl.BlockSpec((2, 1, RW_SLAB), lambda i, j: (0, 0, j))
    out1 = pl.BlockSpec((tile, RW_SLAB), lambda i, j: (i, j))
    out2 = pl.BlockSpec((2, tile, RW_SLAB), lambda i, j: (0, i, j))
    one = jax.ShapeDtypeStruct((N_ROWS, RW_W), F32)
    slab_pairs = RW_SLAB // LANES
    pairs1 = pl.BlockSpec((slab_pairs, tile, LANES), lambda i, j: (j, i, 0))
    pairs2 = pl.BlockSpec((2, slab_pairs, tile, LANES), lambda i, j: (0, j, i, 0))
    one_pm = jax.ShapeDtypeStruct((RW_PAIRS, N_ROWS, LANES), F32)
    two_pm = jax.ShapeDtypeStruct((2, RW_PAIRS, N_ROWS, LANES), F32)
    in_specs += [pl.BlockSpec((None, RW_TAIL_PAD, 5 * RW_SLAB), lambda i, j: (j, 0, 0)), vec2, vec2, vec, vec, vec]
    args += [up_w, w0.reshape(2, 1, RW_W), a0.reshape(2, 1, RW_W), k_k.reshape(1, RW_W), k_a.reshape(1, RW_W),
             r_k.reshape(1, RW_W)]
    stage = lambda width: pltpu.VMEM((tile + 2 * SUB, width), F32)
    r, v, kk, decay, k_dir, b_dir, g, bonus = pl.pallas_call(
        _rwkv_prep_kernel,
        grid=(N_ROWS // tile, n_slab),
        in_specs=in_specs,
        out_specs=[pairs1, pairs1, pairs1, pairs2, pairs2, pairs2, out1, out1],
        out_shape=[one_pm, one_pm, one_pm, two_pm, two_pm, two_pm, one, one],
        scratch_shapes=[stage(RW_SLAB), stage(RW_SLAB), stage(RW_SLAB), stage(RW_TAIL_PAD)],
        compiler_params=_params("parallel", "parallel"),
        name="rwkv7_prep",
    )(*args)

    y2 = _rwkv_scan(r, kk, v, decay, k_dir, b_dir)

    return pl.pallas_call(
        _rwkv_post_kernel,
        grid=(N_ROWS // tile, n_slab),
        in_specs=[out2, out1, out1, vec, vec],
        out_specs=out1,
        out_shape=jax.ShapeDtypeStruct((N_ROWS, RW_W), BF16),
        compiler_params=_params("parallel", "parallel"),
        name="rwkv7_post",
    )(y2, g, bonus, ln_w.reshape(1, RW_W), ln_b.reshape(1, RW_W))


def _qkv_prep_kernel(x_ref, cos_ref, sin_ref, *rest, segments, quarter):
    gain_ref, o_ref = rest if len(rest) == 2 else (None, rest[0])
    _qkv_prep_body(x_ref, cos_ref, sin_ref, gain_ref, o_ref, segments, quarter)


def _qkv_prep_body(x_ref, cos_ref, sin_ref, gain_ref, o_ref, segments, quarter):
    tile = x_ref.shape[0]
    latent = pl.program_id(0) * tile < N_LAT_ROWS
    cos = jnp.where(latent, cos_ref[...], 1.0)
    sin = jnp.where(latent, sin_ref[...], 0.0)
    lane = lax.broadcasted_iota(jnp.int32, (1, LANES), 1)
    even_quarter = (lane // quarter) % 2 == 0
    for first, count, gain_row, rotary, scale in segments:
        for blk in range(first, first + count):
            cols = slice(blk * LANES, (blk + 1) * LANES)
            x = x_ref[:, cols]
            if gain_row is not None:
                x = x * lax.rsqrt(jnp.mean(x * x, axis=-1, keepdims=True) + NORM_EPS) * gain_ref[gain_row:gain_row + 1]
            if rotary:
                rot = jnp.where(even_quarter, -pltpu.roll(x, LANES - quarter, axis=1), pltpu.roll(x, quarter, axis=1))
                x = x * cos + rot * sin
            if scale != 1.0:
                x = x * scale
            o_ref[:, cols] = x.astype(o_ref.dtype)


def _qkv_prep(p, width, col_block, rope, gains, segments, quarter, name, tile=256):
    cos, sin = rope
    reps = LANES // cos.shape[1]
    cos, sin = jnp.tile(cos, (1, reps)), jnp.tile(sin, (1, reps))
    per_seq = SEQ // tile
    table = pl.BlockSpec((tile, LANES), lambda i: (i % per_seq, 0))
    return pl.pallas_call(
        functools.partial(_qkv_prep_kernel, segments=segments, quarter=quarter),
        grid=(N_ROWS // tile,),
        in_specs=[pl.BlockSpec((tile, width), lambda i: (i, col_block)), table, table]
        + ([] if gains is None else [pl.BlockSpec(gains.shape, lambda i: (0, 0))]),
        out_specs=pl.BlockSpec((tile, width), lambda i: (i, 0)),
        out_shape=jax.ShapeDtypeStruct((N_ROWS, width), BF16),
        compiler_params=_params("parallel"),
        name=name,
    )(p, cos, sin, *(() if gains is None else (gains,)))


def _even_mixer(a_all, rope, w_in, rw_args, q_g, k_g, need_ctx):
    rkv_w = 3 * RW_W
    att_w = (GQ_HEADS + 2 * GQ_KV_HEADS) * GQ_HEAD
    w_in = jnp.concatenate([w_in[:, :rkv_w], w_in[:, RW_IN:], w_in[:, rkv_w:RW_IN],
                            jnp.zeros((D_MODEL, RW_TAIL_PAD - RW_TAIL), w_in.dtype)], axis=1).astype(BF16)
    p = _matmul(a_all, w_in, out_dtype=F32, tm=1024, tn=768, tk=D_MODEL, name="even_w_in")
    rw = _rwkv_mixer(p, rkv_w + att_w, *rw_args)
    assert rkv_w % att_w == 0
    qkv = _qkv_prep(p, att_w, rkv_w // att_w, rope, jnp.stack([q_g, k_g]),
                    ((0, GQ_HEADS, 0, True, GQ_SCALE * LOG2E), (GQ_HEADS, GQ_KV_HEADS, 1, True, 1.0),
                     (GQ_HEADS + GQ_KV_HEADS, GQ_KV_HEADS, None, False, 1.0)), GQ_HEAD // 4, "gqa_prep")
    at = _gqa_attention(qkv, latent=True, tq=512)
    if need_ctx:
        at = jnp.concatenate([at, _gqa_attention(qkv, latent=False)], axis=0)
    return rw, at


def _odd_mixer(a_all, rope, w_in, q_norm, q_up, kv_norm, kv_up, lq1, lk1, lq2, lk2, subln, layer_idx):
    w_in = jnp.concatenate([w_in[:, MLA_IN:], w_in[:, :MLA_IN],
                            jnp.zeros((D_MODEL, IN_ODD_PADDED - IN_ODD), w_in.dtype)], axis=1).astype(BF16)
    p = _matmul(a_all, w_in, out_dtype=F32, tm=1024, tn=768, tk=D_MODEL, name="odd_w_in")
    lam_init = 0.8 - 0.6 * math.exp(-0.3 * layer_idx)
    lam = (jnp.exp(jnp.sum(lq1 * lk1).astype(F32)) - jnp.exp(jnp.sum(lq2 * lk2).astype(F32)) + lam_init)

    pm = p[:, DIFF_IN:DIFF_IN + MLA_IN]
    c_q = _rms_f32(pm[:N_LAT_ROWS, :Q_LORA], q_norm).astype(BF16)
    c_kv = _rms_f32(pm[:, Q_LORA:Q_LORA + KV_LORA], kv_norm).astype(BF16)
    k_pe = pm[:, Q_LORA + KV_LORA:MLA_IN]
    q_up_h = q_up.reshape(Q_LORA, MLA_HEADS, MLA_NOPE + MLA_ROPE) * (MLA_SCALE * LOG2E)
    q_up_nope = q_up_h[:, :, :MLA_NOPE].reshape(Q_LORA, -1).astype(BF16)
    q_up_pe = q_up_h[:, :, MLA_NOPE:].reshape(Q_LORA, -1).astype(BF16)
    kv_up_h = kv_up.reshape(KV_LORA, MLA_HEADS, MLA_NOPE + MLA_V)
    kv_up_r = jnp.concatenate([kv_up_h[:, :, :MLA_NOPE].reshape(KV_LORA, -1),
                               kv_up_h[:, :, MLA_NOPE:].reshape(KV_LORA, -1)], axis=1).astype(BF16)
    q_nope = _matmul(c_q, q_up_nope, out_dtype=BF16, tm=1024, tn=2048, tk=Q_LORA, name="mla_q_up_nope")
    q_pe = _matmul(c_q, q_up_pe, out_dtype=F32, tm=1024, tn=1024, tk=Q_LORA, name="mla_q_up_rope")
    kv = _matmul(c_kv, kv_up_r, out_dtype=BF16, tm=1024, tn=2048, tk=KV_LORA, name="mla_kv_up")
    q_pe = _rope_lat(q_pe.reshape(N_LAT_ROWS, MLA_HEADS, MLA_ROPE), rope)
    q_pe = q_pe.reshape(N_LAT_ROWS, MLA_HEADS * MLA_ROPE).astype(BF16)
    k_pe = _rope_lat(k_pe.reshape(N_ROWS, 1, MLA_ROPE), rope).reshape(N_ROWS, MLA_ROPE).astype(BF16)
    k_pe = jnp.tile(k_pe, (1, MLA_HEADS_PER_STEP))
    m_out = _mla_attention(q_nope, q_pe, kv, k_pe, tq=512)

    dqkv = _qkv_prep(p, DIFF_IN, 0, rope, None,
                     ((0, DIFF_HEADS, None, True, DIFF_SCALE * LOG2E), (DIFF_HEADS, DIFF_HEADS, None, True, 1.0),
                      (2 * DIFF_HEADS, DIFF_HEADS, None, False, 1.0)), DIFF_HEAD // 4, "diff_prep")
    d_out = _diff_attention(lam.reshape(1), dqkv, subln, 1.0 - lam_init, tq=512)
    return m_out, d_out


def kernel(x, c, ctx, c_ctx, ada_w, ada_b, norm1_g, norm2_g, mlp_w1, mlp_w2, final_g, ev_w_in, ev_w_out, rw_mu, rw_w0, rw_w_up, rw_a0, rw_a_up, rw_g_up, rw_k_k, rw_k_a, rw_r_k, rw_ln_w, rw_ln_b, gq_q_norm, gq_k_norm, od_w_in, od_w_out, mla_q_norm, mla_q_up, mla_kv_norm, mla_kv_up, diff_lq1, diff_lk1, diff_lq2, diff_lk2, diff_subln):
    rope_gq = _axial_rope_tables(SEQ, GQ_HEAD)
    rope_64 = _axial_rope_tables(SEQ, MLA_ROPE)
    h = (x.reshape(N_LAT_ROWS, D_MODEL), ctx.reshape(N_CTX_ROWS, D_MODEL))
    cond = jnp.concatenate([c_ctx[None], c, jnp.zeros((3, D_MODEL), F32)], axis=0)
    cond = jax.nn.silu(cond).astype(BF16)
    for i in range(DEPTH):
        last = i == DEPTH - 1
        mod = _matmul(cond, ada_w, layer=i, out_dtype=F32, tm=8, tn=2048, tk=1024, name="adaln_mod")
        mod = (mod + ada_b[i])[:BATCH + 1].reshape(BATCH + 1, N_MOD, 1, D_MODEL)
        sh1, sc1, g1, sh2, sc2, g2 = (mod[:, m] for m in range(N_MOD))
        a_all = _norm(h, norm1_g[i], sc1, sh1, rows=N_ROWS, out_dtype=BF16)
        j = i // 2
        if i % 2 == 0:
            rw_args = (rw_mu[j], rw_w0[j], rw_w_up[j], rw_a0[j], rw_a_up[j], rw_g_up[j], rw_k_k[j], rw_k_a[j],
                       rw_r_k[j], rw_ln_w[j], rw_ln_b[j])
            mix = _even_mixer(a_all, rope_gq, ev_w_in[j], rw_args, gq_q_norm[j], gq_k_norm[j], not last)
            w_out = ev_w_out[j]
        else:
            if not last:
                raise NotImplementedError("context rows of an odd layer are only needed when a layer follows")
            mix = _odd_mixer(a_all, rope_64, od_w_in[j], mla_q_norm[j], mla_q_up[j], mla_kv_norm[j],
                             mla_kv_up[j], diff_lq1[j], diff_lk1[j], diff_lq2[j], diff_lk2[j], diff_subln[j], i)
            w_out = od_w_out[j]
        rows = N_LAT_ROWS if last else N_ROWS
        h = _matmul(mix, w_out.astype(BF16), out_dtype=F32, tm=1024, tn=512 if isinstance(h, tuple) else 1024,
                    tk=2048, rows=rows, epilogue="gated_residual", res=h, gate=g1, name="mixer_w_out")
        a2 = _norm(h, norm2_g[i], sc2, sh2, rows=rows, out_dtype=BF16)
        hid = _matmul(a2, mlp_w1, layer=i, out_dtype=BF16, tm=1024, tn=512, tk=D_MODEL,
                      epilogue="relu2", name="mlp_w1")
        h = _matmul(hid, mlp_w2, layer=i, out_dtype=F32, tm=1024, tn=1024, tk=2048,
                    epilogue="gated_residual", res=h, gate=g2, name="mlp_w2")
    out = _norm(h, final_g, rows=N_LAT_ROWS, out_dtype=F32)
    return out.reshape(BATCH, SEQ, D_MODEL)
```

```python
import functools
import math

import jax
import jax.numpy as jnp
from jax import lax
from jax.experimental import pallas as pl
from jax.experimental.pallas import tpu as pltpu

D_MODEL = 4096
BATCH = 4
SEQ = 4096
DEPTH = 2
CTX_LEN = 256
GRID_W = 64
ROPE_THETA = 10000.0
NORM_EPS = 1e-6
N_MOD = 6
HALF = D_MODEL // 2

RW_HEAD = 64
RW_W = HALF
RW_HEADS = RW_W // RW_HEAD
DECAY_RANK = 96
AAA_RANK = 96
GATE_RANK = 256
GN_EPS = 64e-5
RW_IN = 3 * RW_W + GATE_RANK + 2 * DECAY_RANK + 2 * AAA_RANK

GQ_HEAD = 128
GQ_HEADS = HALF // GQ_HEAD
GQ_KV_HEADS = GQ_HEADS // 4
GQ_GROUP = GQ_HEADS // GQ_KV_HEADS
GQ_SCALE = GQ_HEAD ** -0.5

MLA_NOPE = 128
MLA_ROPE = 64
MLA_V = 128
MLA_HEADS = HALF // MLA_V
Q_LORA = 768
KV_LORA = 512
MLA_SCALE = (MLA_NOPE + MLA_ROPE) ** -0.5
MLA_IN = Q_LORA + KV_LORA + MLA_ROPE

DIFF_HEAD = 64
DIFF_V = 2 * DIFF_HEAD
DIFF_HEADS = HALF // DIFF_V
DIFF_SCALE = DIFF_HEAD ** -0.5
DIFF_IN = DIFF_HEADS * (4 * DIFF_HEAD + DIFF_V)
IN_ODD = MLA_IN + DIFF_IN
IN_ODD_PADDED = 7680

N_LAT_ROWS = BATCH * SEQ
N_CTX_ROWS = BATCH * CTX_LEN
N_ROWS = N_LAT_ROWS + N_CTX_ROWS
CTX_BLOCK0 = N_LAT_ROWS // CTX_LEN

V7X_VMEM_LIMIT_BYTES = 56 * 1024 * 1024

BF16 = jnp.bfloat16
F32 = jnp.float32


def _params(*sem):
    return pltpu.CompilerParams(dimension_semantics=sem, vmem_limit_bytes=V7X_VMEM_LIMIT_BYTES)


def _row_group(row_tile, tile_rows):
    start = row_tile * tile_rows
    return jnp.where(start < N_LAT_ROWS, 1 + start // SEQ, 0)


def _split_rows_specs(block, tile_rows, col):
    n_lat = N_LAT_ROWS // tile_rows

    def lat_map(i, *rest):
        used = i < n_lat
        return jnp.where(used, i, n_lat - 1), jnp.where(used, col(i, *rest), 0)

    def ctx_map(i, *rest):
        used = i >= n_lat
        return jnp.where(used, i - n_lat, 0), jnp.where(used, col(i, *rest), 0)

    return [pl.BlockSpec(block, lat_map), pl.BlockSpec(block, ctx_map)]


def _pick_rows(lat_ref, ctx_ref, tile_rows):
    return jnp.where(pl.program_id(0) * tile_rows < N_LAT_ROWS, lat_ref[...], ctx_ref[...])


def _norm_kernel(*refs, modulate, split):
    x_refs, (g_ref, *rest) = refs[:1 + split], refs[1 + split:]
    if modulate:
        sc_ref, sh_ref, o_ref = rest
    else:
        (o_ref,) = rest
    x = _pick_rows(*x_refs, o_ref.shape[0]) if split else x_refs[0][...]
    ms = jnp.mean(x * x, axis=-1, keepdims=True)
    y = x * lax.rsqrt(ms + NORM_EPS) * g_ref[...]
    if modulate:
        y = y * (1.0 + sc_ref[0]) + sh_ref[0]
    o_ref[...] = y.astype(o_ref.dtype)


def _norm(x, g, sc=None, sh=None, *, rows, out_dtype, tile=256):
    split = isinstance(x, tuple)
    x_parts = x if split else (x,)
    d = x_parts[0].shape[1]
    modulate = sc is not None
    x_specs = (_split_rows_specs((tile, d), tile, lambda i: 0) if split else
               [pl.BlockSpec((tile, d), lambda i: (i, 0))])
    in_specs = x_specs + [pl.BlockSpec((1, d), lambda i: (0, 0))]
    args = [*x_parts, g.reshape(1, d)]
    if modulate:
        mod_spec = pl.BlockSpec((1, 1, d), lambda i: (_row_group(i, tile), 0, 0))
        in_specs += [mod_spec, mod_spec]
        args += [sc, sh]
    return pl.pallas_call(
        functools.partial(_norm_kernel, modulate=modulate, split=split),
        grid=(rows // tile,),
        in_specs=in_specs,
        out_specs=pl.BlockSpec((tile, d), lambda i: (i, 0)),
        out_shape=jax.ShapeDtypeStruct((rows, d), out_dtype),
        compiler_params=_params("parallel"),
        name="rmsnorm_mod" if modulate else "rmsnorm",
    )(*args)


def _mm_kernel(*refs, epilogue, nk, n_a, n_res):
    a_refs, w_ref, rest = refs[:n_a], refs[n_a], refs[n_a + 1:]
    if epilogue == "gated_residual":
        res_refs, (gate_ref, o_ref, acc_ref) = rest[:n_res], rest[n_res:]
    else:
        o_ref, acc_ref = rest
    k = pl.program_id(2)

    def product(a_ref):
        return jnp.dot(a_ref[...].astype(BF16), w_ref[...].astype(BF16), preferred_element_type=F32)

    def store(part, first):
        if first:
            acc_ref[...] = part
        else:
            acc_ref[...] += part

    if n_a == 1:
        part = product(a_refs[0])
        pl.when(k == 0)(lambda: store(part, True))
        pl.when(k > 0)(lambda: store(part, False))
    else:
        for idx, a_ref in enumerate(a_refs):
            pl.when(k == idx)(lambda a_ref=a_ref, idx=idx: store(product(a_ref), idx == 0))

    @pl.when(k == nk - 1)
    def _():
        acc = acc_ref[...]
        if epilogue == "relu2":
            acc = jnp.square(jnp.maximum(acc, 0.0))
        elif epilogue == "gated_residual":
            res = _pick_rows(*res_refs, o_ref.shape[0]) if n_res == 2 else res_refs[0][...]
            acc = res + gate_ref[0] * acc
        o_ref[...] = acc.astype(o_ref.dtype)


def _matmul(a, w, *, out_dtype, tm, tn, tk, rows=None, layer=None, epilogue=None, res=None, gate=None, name):
    kdim, n = w.shape[-2:]
    a_parts = a if isinstance(a, tuple) else (a,)
    m = a_parts[0].shape[0] if rows is None else rows
    assert sum(part.shape[1] for part in a_parts) == kdim and m % tm == 0 and n % tn == 0 and kdim % tk == 0
    assert (w.ndim == 3) == (layer is not None)
    nk = kdim // tk
    w_spec = (pl.BlockSpec((tk, tn), lambda i, j, k: (k, j)) if layer is None else
              pl.BlockSpec((None, tk, tn), lambda i, j, k: (layer, k, j)))
    if len(a_parts) == 1:
        in_specs = [pl.BlockSpec((tm, tk), lambda i, j, k: (i, k)), w_spec]
    else:
        assert len(a_parts) == nk and all(part.shape[1] == tk for part in a_parts)
        in_specs = [pl.BlockSpec((tm, tk), lambda i, j, k: (i, 0))] * nk + [w_spec]
    args = [*a_parts, w]
    res_parts = res if isinstance(res, tuple) else (res,)
    if epilogue == "gated_residual":
        in_specs += (_split_rows_specs((tm, tn), tm, lambda i, j, k: j) if len(res_parts) == 2 else
                     [pl.BlockSpec((tm, tn), lambda i, j, k: (i, j))])
        in_specs.append(pl.BlockSpec((1, 1, tn), lambda i, j, k: (_row_group(i, tm), 0, j)))
        args += [*res_parts, gate]
    return pl.pallas_call(
        functools.partial(_mm_kernel, epilogue=epilogue, nk=nk, n_a=len(a_parts), n_res=len(res_parts)),
        grid=(m // tm, n // tn, nk),
        in_specs=in_specs,
        out_specs=pl.BlockSpec((tm, tn), lambda i, j, k: (i, j)),
        out_shape=jax.ShapeDtypeStruct((m, n), out_dtype),
        scratch_shapes=[pltpu.VMEM((tm, tn), F32)],
        compiler_params=_params("parallel", "parallel", "arbitrary"),
        name=name,
    )(*args)


def _nt_dot(a, b):
    return lax.dot_general(a, b, (((1,), (1,)), ((), ())), preferred_element_type=F32)


LOG2E = math.log2(math.e)


def _fill_values_and_ones(v_ref, vo_ref):
    d = v_ref.shape[1]
    vo_ref[:, :d] = v_ref[...]
    vo_ref[:, d:] = jnp.ones_like(v_ref)


def _softmax_times_values(s_list, vo_refs):
    d = vo_refs[0].shape[1] // 2
    m = functools.reduce(jnp.maximum, [jnp.max(s, axis=-1, keepdims=True) for s in s_list])
    acc = functools.reduce(jnp.add, [jnp.dot(jnp.exp2(s - m).astype(BF16), vo_ref[...], preferred_element_type=F32)
                                     for s, vo_ref in zip(s_list, vo_refs)])
    return acc[:, :d] / acc[:, d:]


def _attend_each(items):
    outs = []
    pending = None
    for q, k_refs, vo_refs in items:
        s_list = [_nt_dot(q, k_ref[...]) for k_ref in k_refs]
        if pending is not None:
            outs.append(_softmax_times_values(*pending))
        pending = (s_list, vo_refs)
    outs.append(_softmax_times_values(*pending))
    return outs


def _gqa_kernel(q_ref, *rest, with_lat):
    if with_lat:
        kc_ref, vc_ref, kl_ref, vl_ref, o_ref, voc_ref, vol_ref = rest
        k_refs, v_refs, vo_refs = (kc_ref, kl_ref), (vc_ref, vl_ref), (voc_ref, vol_ref)
    else:
        kc_ref, vc_ref, o_ref, voc_ref = rest
        k_refs, v_refs, vo_refs = (kc_ref,), (vc_ref,), (voc_ref,)

    @pl.when(pl.program_id(2) == 0)
    def _():
        for v_ref, vo_ref in zip(v_refs, vo_refs):
            _fill_values_and_ones(v_ref, vo_ref)

    outs = _attend_each([(q_ref[:, g * GQ_HEAD:(g + 1) * GQ_HEAD], k_refs, vo_refs) for g in range(GQ_GROUP)])
    for g, o in enumerate(outs):
        o_ref[:, g * GQ_HEAD:(g + 1) * GQ_HEAD] = o.astype(o_ref.dtype)


def _gqa_attention(qkv, *, latent, tq=256):
    gw = GQ_GROUP * GQ_HEAD
    k0, v0 = GQ_HEADS, GQ_HEADS + GQ_KV_HEADS
    ctx_k = pl.BlockSpec((CTX_LEN, GQ_HEAD), lambda b, n, i: (CTX_BLOCK0 + b, k0 + n))
    ctx_v = pl.BlockSpec((CTX_LEN, GQ_HEAD), lambda b, n, i: (CTX_BLOCK0 + b, v0 + n))
    scratch = [pltpu.VMEM((CTX_LEN, 2 * GQ_HEAD), BF16)]
    if latent:
        per_b = SEQ // tq
        lat_k = pl.BlockSpec((SEQ, GQ_HEAD), lambda b, n, i: (b, k0 + n))
        lat_v = pl.BlockSpec((SEQ, GQ_HEAD), lambda b, n, i: (b, v0 + n))
        in_specs = [pl.BlockSpec((tq, gw), lambda b, n, i: (b * per_b + i, n)), ctx_k, ctx_v, lat_k, lat_v]
        args = (qkv,) * 5
        rows = N_LAT_ROWS
        scratch.append(pltpu.VMEM((SEQ, 2 * GQ_HEAD), BF16))
    else:
        per_b = CTX_LEN // tq
        q_blk0 = N_LAT_ROWS // tq
        in_specs = [pl.BlockSpec((tq, gw), lambda b, n, i: (q_blk0 + b * per_b + i, n)), ctx_k, ctx_v]
        args = (qkv,) * 3
        rows = N_CTX_ROWS
    return pl.pallas_call(
        functools.partial(_gqa_kernel, with_lat=latent),
        grid=(BATCH, GQ_KV_HEADS, per_b),
        in_specs=in_specs,
        out_specs=pl.BlockSpec((tq, gw), lambda b, n, i: (b * per_b + i, n)),
        out_shape=jax.ShapeDtypeStruct((rows, GQ_HEADS * GQ_HEAD), BF16),
        scratch_shapes=scratch,
        compiler_params=_params("parallel", "parallel", "arbitrary"),
        name="gqa_attention_lat" if latent else "gqa_attention_ctx",
    )(*args)


def _mla_kernel(qn_ref, qp_ref, knc_ref, kpc_ref, vc_ref, knl_ref, kpl_ref, vl_ref, o_ref,
                kc_ref, kl_ref, voc_ref, vol_ref):
    heads = range(MLA_HEADS_PER_STEP)
    cols = [slice(j * MLA_NOPE, (j + 1) * MLA_NOPE) for j in heads]

    def own_rope_lanes(x, j):
        lane = lax.broadcasted_iota(jnp.int32, x.shape, 1)
        return jnp.where(lane // MLA_ROPE == j, x, jnp.zeros_like(x))

    @pl.when(pl.program_id(2) == 0)
    def _():
        for kn_ref, kp_ref, v_ref, k_ref, vo_ref in ((knc_ref, kpc_ref, vc_ref, kc_ref, voc_ref),
                                                     (knl_ref, kpl_ref, vl_ref, kl_ref, vol_ref)):
            for j in heads:
                k_ref[j, :, :MLA_NOPE] = kn_ref[:, cols[j]]
                k_ref[j, :, MLA_NOPE:] = own_rope_lanes(kp_ref[...], j)
                vo_ref[j, :, :MLA_V] = v_ref[:, cols[j]]
                vo_ref[j, :, MLA_V:] = jnp.ones_like(kp_ref)

    qp = qp_ref[...]
    items = [(jnp.concatenate([qn_ref[:, cols[j]], own_rope_lanes(qp, j)], axis=1), (kc_ref.at[j], kl_ref.at[j]),
              (voc_ref.at[j], vol_ref.at[j])) for j in heads]
    for j, o in enumerate(_attend_each(items)):
        o_ref[:, cols[j]] = o.astype(o_ref.dtype)


MLA_HEADS_PER_STEP = 2


def _mla_attention(q_nope, q_pe, kv, k_pe, *, tq=256):
    per_b = SEQ // tq
    hs = MLA_HEADS_PER_STEP
    assert hs * MLA_ROPE == MLA_NOPE
    w = hs * MLA_NOPE
    v_blk0 = MLA_HEADS // hs
    in_specs = [
        pl.BlockSpec((tq, w), lambda b, h, i: (b * per_b + i, h)),
        pl.BlockSpec((tq, hs * MLA_ROPE), lambda b, h, i: (b * per_b + i, h)),
        pl.BlockSpec((CTX_LEN, w), lambda b, h, i: (CTX_BLOCK0 + b, h)),
        pl.BlockSpec((CTX_LEN, MLA_NOPE), lambda b, h, i: (CTX_BLOCK0 + b, 0)),
        pl.BlockSpec((CTX_LEN, w), lambda b, h, i: (CTX_BLOCK0 + b, v_blk0 + h)),
        pl.BlockSpec((SEQ, w), lambda b, h, i: (b, h)),
        pl.BlockSpec((SEQ, MLA_NOPE), lambda b, h, i: (b, 0)),
        pl.BlockSpec((SEQ, w), lambda b, h, i: (b, v_blk0 + h)),
    ]
    return pl.pallas_call(
        _mla_kernel,
        grid=(BATCH, MLA_HEADS // hs, per_b),
        in_specs=in_specs,
        out_specs=pl.BlockSpec((tq, w), lambda b, h, i: (b * per_b + i, h)),
        out_shape=jax.ShapeDtypeStruct((N_LAT_ROWS, MLA_HEADS * MLA_V), BF16),
        scratch_shapes=[pltpu.VMEM((hs, CTX_LEN, 2 * MLA_NOPE), BF16), pltpu.VMEM((hs, SEQ, 2 * MLA_NOPE), BF16),
                        pltpu.VMEM((hs, CTX_LEN, 2 * MLA_V), BF16), pltpu.VMEM((hs, SEQ, 2 * MLA_V), BF16)],
        compiler_params=_params("parallel", "parallel", "arbitrary"),
        name="mla_attention",
    )(q_nope, q_pe, kv, k_pe, kv, kv, k_pe, kv)


def _diff_kernel(lam_ref, q_ref, kc_ref, vc_ref, kl_ref, vl_ref, g_ref, o_ref, voc_ref, vol_ref, *, out_scale):
    heads = range(DIFF_HEADS_PER_STEP)
    cols = [slice(j * DIFF_V, (j + 1) * DIFF_V) for j in heads]

    @pl.when(pl.program_id(2) == 0)
    def _():
        for v_ref, vo_ref in ((vc_ref, voc_ref), (vl_ref, vol_ref)):
            for j in heads:
                vo_ref[j, :, :DIFF_V] = v_ref[:, cols[j]]
                vo_ref[j, :, DIFF_V:] = jnp.ones((v_ref.shape[0], DIFF_V), BF16)

    first = lax.broadcasted_iota(jnp.int32, (q_ref.shape[0], DIFF_V), 1) < DIFF_HEAD
    items = []
    for j in heads:
        q = q_ref[:, cols[j]]
        zero = jnp.zeros_like(q)
        keys = (kc_ref[:, cols[j]], kl_ref[:, cols[j]])
        for qh in (jnp.where(first, q, zero), jnp.where(first, zero, q)):
            items.append((qh, keys, (voc_ref.at[j], vol_ref.at[j])))
    outs = _attend_each(items)
    for j in heads:
        o = outs[2 * j] - lam_ref[0] * outs[2 * j + 1]
        ms = jnp.mean(o * o, axis=-1, keepdims=True)
        o = o * lax.rsqrt(ms + NORM_EPS) * g_ref[...]
        o_ref[:, cols[j]] = (o * out_scale).astype(o_ref.dtype)


DIFF_HEADS_PER_STEP = 2


def _diff_attention(lam, qkv, subln, out_scale, *, tq=256):
    per_b = SEQ // tq
    hs = DIFF_HEADS_PER_STEP
    w = hs * DIFF_V
    k0, v0 = DIFF_HEADS // hs, 2 * DIFF_HEADS // hs
    ctx_k = pl.BlockSpec((CTX_LEN, w), lambda b, h, i: (CTX_BLOCK0 + b, k0 + h))
    ctx_v = pl.BlockSpec((CTX_LEN, w), lambda b, h, i: (CTX_BLOCK0 + b, v0 + h))
    lat_k = pl.BlockSpec((SEQ, w), lambda b, h, i: (b, k0 + h))
    lat_v = pl.BlockSpec((SEQ, w), lambda b, h, i: (b, v0 + h))
    in_specs = [
        pl.BlockSpec(memory_space=pltpu.SMEM),
        pl.BlockSpec((tq, w), lambda b, h, i: (b * per_b + i, h)),
        ctx_k, ctx_v, lat_k, lat_v,
        pl.BlockSpec((1, DIFF_V), lambda b, h, i: (0, 0)),
    ]
    return pl.pallas_call(
        functools.partial(_diff_kernel, out_scale=out_scale),
        grid=(BATCH, DIFF_HEADS // hs, per_b),
        in_specs=in_specs,
        out_specs=pl.BlockSpec((tq, w), lambda b, h, i: (b * per_b + i, h)),
        out_shape=jax.ShapeDtypeStruct((N_LAT_ROWS, DIFF_HEADS * DIFF_V), BF16),
        scratch_shapes=[pltpu.VMEM((hs, CTX_LEN, 2 * DIFF_V), BF16), pltpu.VMEM((hs, SEQ, 2 * DIFF_V), BF16)],
        compiler_params=_params("parallel", "parallel", "arbitrary"),
        name="diff_attention",
    )(lam, qkv, qkv, qkv, qkv, qkv, subln.reshape(1, DIFF_V))


RW_CHUNK = 256
RW_PAIRS = RW_HEADS // 2
LANES = 2 * RW_HEAD
SUB = 8
RW_UNROLL = 16
RW_GROUP = 8


def _rwkv_kernel(r_ref, kk_ref, v_ref, w_ref, k_ref, b_ref, y_out, s_ref, y_ref):
    d = pl.program_id(0)
    c = pl.program_id(2)

    @pl.when(c == 0)
    def _():
        s_ref[...] = jnp.zeros_like(s_ref)

    v_hi = lax.broadcasted_iota(jnp.int32, (SUB, SUB, LANES), 0)
    v_lo = lax.broadcasted_iota(jnp.int32, (SUB, SUB, LANES), 1)
    lane3 = lax.broadcasted_iota(jnp.int32, (SUB, SUB, LANES), 2)
    dup = jnp.where((lane3 & (RW_HEAD - 1)) == v_hi * SUB + v_lo, 1.0, 0.0)
    blk_r = lax.broadcasted_iota(jnp.int32, (LANES, LANES), 0) // RW_HEAD
    blk_c = lax.broadcasted_iota(jnp.int32, (LANES, LANES), 1) // RW_HEAD
    head_ones = jnp.where(blk_r == blk_c, 1.0, 0.0).astype(BF16)
    slot_of_lane = lax.broadcasted_iota(jnp.int32, (SUB, LANES), 1) & (RW_HEAD - 1)
    step_dir = jnp.where(d == 0, 1, -1)

    y_ref[...] = jnp.zeros_like(y_ref)

    def row(ref, p, t, *lead):
        return ref[(*lead, p, pl.ds(t, SUB, stride=0), slice(None))]

    def as_rows(tile):
        return tile.reshape(RW_HEAD, LANES).astype(BF16)

    def head_sums(states, pairs, t_done, t_next):
        rows = []
        for p in pairs:
            rows.append(as_rows(states[p] * row(r_ref, p, t_done)))
            if t_next is not None:
                rows.append(as_rows(states[p] * row(kk_ref, p, t_next)))
                rows.append(as_rows(dup * row(v_ref, p, t_next)))
        return jnp.dot(jnp.concatenate(rows, axis=0), head_ones, preferred_element_type=F32)

    def slab(sums, q, n_slabs, i):
        lo = (q * n_slabs + i) * RW_HEAD
        return sums[lo:lo + RW_HEAD].reshape(SUB, SUB, LANES)

    def put_y(p, y_b, t, valid):
        window = t // RW_HEAD
        keep = (slot_of_lane == t % RW_HEAD) & valid
        y_ref[0, p, window] = jnp.where(keep, y_b, y_ref[0, p, window])

    groups = [range(g0, g0 + RW_GROUP) for g0 in range(0, RW_PAIRS, RW_GROUP)]

    def block(g, carry):
        first = g * RW_UNROLL
        first = jnp.where(d == 0, first, RW_CHUNK - 1 - first)
        times = [first + u * step_dir for u in range(RW_UNROLL)]
        t_before = jnp.clip(first - step_dir, 0, RW_CHUNK - 1)
        states = [s_ref[p] for p in range(RW_PAIRS)]
        sums = [head_sums(states, grp, t_before, times[0]) for grp in groups]
        for gi, grp in enumerate(groups):
            for q, p in enumerate(grp):
                put_y(p, slab(sums[gi], q, 3, 0), t_before, g > 0)
        for u in range(RW_UNROLL):
            t = times[u]
            for gi, grp in enumerate(groups):
                for q, p in enumerate(grp):
                    sa = slab(sums[gi], q, 3, 1)
                    vb = slab(sums[gi], q, 3, 2)
                    states[p] = (states[p] * row(w_ref, p, t, 0) - sa * row(b_ref, p, t, 0)
                                 + vb * row(k_ref, p, t, 0))
                if u + 1 < RW_UNROLL:
                    sums[gi] = head_sums(states, grp, t, times[u + 1])
                    for q, p in enumerate(grp):
                        put_y(p, slab(sums[gi], q, 3, 0), t, True)
        for p in range(RW_PAIRS):
            s_ref[p] = states[p]
        return carry

    lax.fori_loop(0, RW_CHUNK // RW_UNROLL, block, 0)

    t_last = jnp.where(d == 0, RW_CHUNK - 1, 0)
    states = [s_ref[p] for p in range(RW_PAIRS)]
    for grp in groups:
        last = head_sums(states, grp, t_last, None)
        for q, p in enumerate(grp):
            put_y(p, slab(last, q, 1, 0), t_last, True)

    first_head = lax.broadcasted_iota(jnp.int32, (RW_HEAD, LANES), 1) < RW_HEAD
    for p in range(0, RW_PAIRS, 2):
        for window in range(RW_CHUNK // RW_HEAD):
            both = jnp.concatenate([y_ref[0, p, window].reshape(RW_HEAD, LANES),
                                    y_ref[0, p + 1, window].reshape(RW_HEAD, LANES)], axis=0)
            by_row = both.T
            head0, head1 = by_row[:RW_HEAD], by_row[RW_HEAD:]
            rows = slice(window * RW_HEAD, (window + 1) * RW_HEAD)
            y_out[0, rows, p * LANES:(p + 1) * LANES] = jnp.where(
                first_head, head0, pltpu.roll(head1, RW_HEAD, axis=1))
            y_out[0, rows, (p + 1) * LANES:(p + 2) * LANES] = jnp.where(
                first_head, pltpu.roll(head0, RW_HEAD, axis=1), head1)


def _rwkv_scan(r, kk, v, w, k, b):
    n_ctx = CTX_LEN // RW_CHUNK
    n_lat = SEQ // RW_CHUNK
    ctx0 = N_LAT_ROWS // RW_CHUNK

    def chunk(d, bb, c):
        j = jnp.where(d == 0, c, jnp.where(c < n_ctx, n_ctx - 1 - c, n_ctx + n_lat - 1 - (c - n_ctx)))
        return jnp.where(j < n_ctx, ctx0 + bb * n_ctx + j, bb * n_lat + j - n_ctx)

    shared = pl.BlockSpec((RW_PAIRS, RW_CHUNK, LANES), lambda d, bb, c: (0, chunk(d, bb, c), 0))
    per_dir = pl.BlockSpec((1, RW_PAIRS, RW_CHUNK, LANES), lambda d, bb, c: (d, 0, chunk(d, bb, c), 0))
    return pl.pallas_call(
        _rwkv_kernel,
        grid=(2, BATCH, n_ctx + n_lat),
        in_specs=[shared, shared, shared, per_dir, per_dir, per_dir],
        out_specs=pl.BlockSpec((1, RW_CHUNK, RW_W), lambda d, bb, c: (d, chunk(d, bb, c), 0)),
        out_shape=jax.ShapeDtypeStruct((2, N_ROWS, RW_W), F32),
        scratch_shapes=[pltpu.VMEM((RW_PAIRS, RW_HEAD // SUB, SUB, LANES), F32),
                        pltpu.VMEM((1, RW_PAIRS, RW_CHUNK // RW_HEAD, SUB, SUB, LANES), F32)],
        compiler_params=_params("parallel", "parallel", "arbitrary"),
        name="rwkv7_scan",
    )(r, kk, v, w, k, b)


def _axial_rope_tables(n_tokens, rot_dim):
    n_rows = n_tokens // GRID_W
    row = jnp.repeat(jnp.arange(n_rows, dtype=F32), GRID_W)
    col = jnp.tile(jnp.arange(GRID_W, dtype=F32), n_rows)
    axis_dim = rot_dim // 2
    inv_freq = ROPE_THETA ** (-jnp.arange(0, axis_dim, 2, dtype=F32) / axis_dim)
    ang_r = row[:, None] * inv_freq
    ang_c = col[:, None] * inv_freq
    ang = jnp.concatenate([ang_r, ang_r, ang_c, ang_c], axis=-1)
    return jnp.cos(ang), jnp.sin(ang)


def _rotate_half(z):
    z1, z2 = jnp.split(z, 2, axis=-1)
    return jnp.concatenate([-z2, z1], axis=-1)


def _apply_rope(x, cos, sin):
    half = x.shape[-1] // 2
    rot = jnp.concatenate([_rotate_half(x[..., :half]), _rotate_half(x[..., half:])], axis=-1)
    return x * cos + rot * sin


def _rms_f32(x, g):
    return x * lax.rsqrt(jnp.mean(x * x, axis=-1, keepdims=True) + NORM_EPS) * g


def _rope_lat(x, rope):
    cos, sin = rope
    heads, dim = x.shape[1:]
    lat = x[:N_LAT_ROWS].reshape(BATCH, SEQ, heads, dim)
    lat = _apply_rope(lat, cos[None, :, None, :], sin[None, :, None, :]).reshape(N_LAT_ROWS, heads, dim)
    return lat if x.shape[0] == N_LAT_ROWS else jnp.concatenate([lat, x[N_LAT_ROWS:]], axis=0)


RW_SLAB = 512
RW_TAIL = GATE_RANK + 2 * DECAY_RANK + 2 * AAA_RANK
RW_TAIL_PAD = 768
RW_ROW_TILE = 256


def _head_lane_sums(x):
    rows = x.shape[0]
    blk_r = lax.broadcasted_iota(jnp.int32, (LANES, LANES), 0) // RW_HEAD
    blk_c = lax.broadcasted_iota(jnp.int32, (LANES, LANES), 1) // RW_HEAD
    head_ones = jnp.where(blk_r == blk_c, 1.0, 0.0).astype(BF16)
    hi = x.astype(BF16)
    rest = x - hi.astype(F32)
    mid = rest.astype(BF16)
    lo = (rest - mid.astype(F32)).astype(BF16)
    pieces = jnp.concatenate([hi, mid, lo], axis=0)
    out = []
    for s in range(x.shape[1] // LANES):
        part = jnp.dot(pieces[:, s * LANES:(s + 1) * LANES], head_ones, preferred_element_type=F32)
        out.append(part[:rows] + part[rows:2 * rows] + part[2 * rows:])
    return jnp.concatenate(out, axis=1)


def _rwkv_prep_kernel(*refs):
    slabs = [refs[4 * n:4 * n + 4] for n in range(4)]
    (up_w_ref, w0_ref, a0_ref, kk_gain_ref, ka_ref, rk_ref,
     r_out, v_out, kk_out, decay_out, kdir_out, b_out, g_out, bonus_out) = refs[16:30]
    stage_refs = refs[30:34]
    tile = RW_ROW_TILE
    row0 = pl.program_id(0) * tile
    seq_len = jnp.where(row0 < N_LAT_ROWS, SEQ, CTX_LEN)
    starts_seq = row0 % seq_len == 0
    ends_seq = (row0 + tile) % seq_len == 0

    def shifted(main_ref, before_ref, after_ref, mu_ref, stage_ref):
        x = main_ref[...]
        stage_ref[SUB:SUB + tile, :] = x
        stage_ref[SUB - 1:SUB, :] = jnp.where(starts_seq, 0.0, before_ref[SUB - 1:SUB, :])
        stage_ref[SUB + tile:SUB + tile + 1, :] = jnp.where(ends_seq, 0.0, after_ref[0:1, :])
        around = 0.5 * (stage_ref[SUB - 1:SUB - 1 + tile, :] + stage_ref[SUB + 1:SUB + 1 + tile, :])
        return x + (around - x) * mu_ref[...]

    def put_pairs(out_ref, x, *lead):
        for q in range(RW_SLAB // LANES):
            out_ref[(*lead, q)] = x[:, q * LANES:(q + 1) * LANES]

    r, k, v, tail = (shifted(*slab, stage) for slab, stage in zip(slabs, stage_refs))
    put_pairs(r_out, r)
    put_pairs(v_out, v)
    s_w = RW_SLAB
    mid = tail[:, GATE_RANK:2 * GATE_RANK]
    mid_lane = lax.broadcasted_iota(jnp.int32, (1, GATE_RANK), 1)
    act = jnp.concatenate([jax.nn.sigmoid(tail[:, :GATE_RANK]),
                           jnp.where(mid_lane < 2 * DECAY_RANK, jnp.tanh(mid), mid),
                           tail[:, 2 * GATE_RANK:]], axis=1)
    up = jnp.dot(act.astype(BF16), up_w_ref[...], preferred_element_type=F32)
    g_out[...] = up[:, :s_w]
    kq = k * kk_gain_ref[...]
    kk = kq / jnp.maximum(jnp.sqrt(_head_lane_sums(kq * kq)), 1e-12)
    put_pairs(kk_out, kk)
    kdir_sum = jnp.zeros_like(k)
    for d in range(2):
        x = w0_ref[d] + up[:, (1 + d) * s_w:(2 + d) * s_w]
        put_pairs(decay_out, jnp.exp(jax.nn.sigmoid(x) * -math.exp(-0.5)), d)
        a = jax.nn.sigmoid(a0_ref[d] + up[:, (3 + d) * s_w:(4 + d) * s_w])
        kd = k * (1.0 + (a - 1.0) * ka_ref[...])
        put_pairs(kdir_out, kd, d)
        put_pairs(b_out, kk * a, d)
        kdir_sum = kdir_sum + kd
    bonus_out[...] = _head_lane_sums(r * kdir_sum * rk_ref[...]) * v


def _rwkv_post_kernel(y_ref, g_ref, bonus_ref, lnw_ref, lnb_ref, o_ref):
    y = y_ref[0] + y_ref[1]
    mean = _head_lane_sums(y) * (1.0 / RW_HEAD)
    cen = y - mean
    var = _head_lane_sums(cen * cen) * (1.0 / RW_HEAD)
    yn = cen * lax.rsqrt(var + GN_EPS) * lnw_ref[...] + lnb_ref[...]
    o_ref[...] = ((yn + bonus_ref[...]) * g_ref[...]).astype(o_ref.dtype)


def _rwkv_mixer(p, tail_col, mu, w0, w_up, a0, a_up, g_up, k_k, k_a, r_k, ln_w, ln_b):
    assert tail_col % RW_TAIL_PAD == 0 and p.shape[1] >= tail_col + RW_TAIL_PAD
    n_slab = RW_W // RW_SLAB
    up_w = jnp.zeros((RW_TAIL_PAD, 5, RW_W), F32)
    o = GATE_RANK
    up_w = up_w.at[:o, 0].set(g_up)
    for d in range(2):
        up_w = up_w.at[o + d * DECAY_RANK:o + (d + 1) * DECAY_RANK, 1 + d].set(w_up[d])
    o += 2 * DECAY_RANK
    for d in range(2):
        up_w = up_w.at[o + d * AAA_RANK:o + (d + 1) * AAA_RANK, 3 + d].set(a_up[d])
    up_w = up_w.reshape(RW_TAIL_PAD, 5, n_slab, RW_SLAB).transpose(2, 0, 1, 3)
    up_w = up_w.reshape(n_slab, RW_TAIL_PAD, 5 * RW_SLAB).astype(BF16)
    mu_main = mu[:3 * RW_W].reshape(1, 3 * RW_W)
    mu_tail = jnp.pad(mu[3 * RW_W:], (0, RW_TAIL_PAD - RW_TAIL)).reshape(1, RW_TAIL_PAD)

    tile = RW_ROW_TILE
    col_blocks = RW_W // RW_SLAB
    halo = tile // SUB
    last_halo = N_ROWS // SUB - 1

    def with_halo(width, col):
        return [pl.BlockSpec((tile, width), lambda i, j: (i, col(j))),
                pl.BlockSpec((SUB, width), lambda i, j: (jnp.maximum(i * halo - 1, 0), col(j))),
                pl.BlockSpec((SUB, width), lambda i, j: (jnp.minimum((i + 1) * halo, last_halo), col(j)))]

    in_specs, args = [], []
    for n in range(3):
        in_specs += with_halo(RW_SLAB, lambda j, n=n: n * col_blocks + j)
        in_specs.append(pl.BlockSpec((1, RW_SLAB), lambda i, j, n=n: (0, n * col_blocks + j)))
        args += [p, p, p, mu_main]
    tail_blk = tail_col // RW_TAIL_PAD
    in_specs += with_halo(RW_TAIL_PAD, lambda j: tail_blk)
    in_specs.append(pl.BlockSpec((1, RW_TAIL_PAD), lambda i, j: (0, 0)))
    args += [p, p, p, mu_tail]

    vec = pl.BlockSpec((1, RW_SLAB), lambda i, j: (0, j))
    vec2 = pl.BlockSpec((2, 1, RW_SLAB), lambda i, j: (0, 0, j))
    out1 = pl.BlockSpec((tile, RW_SLAB), lambda i, j: (i, j))
    out2 = pl.BlockSpec((2, tile, RW_SLAB), lambda i, j: (0, i, j))
    one = jax.ShapeDtypeStruct((N_ROWS, RW_W), F32)
    slab_pairs = RW_SLAB // LANES
    pairs1 = pl.BlockSpec((slab_pairs, tile, LANES), lambda i, j: (j, i, 0))
    pairs2 = pl.BlockSpec((2, slab_pairs, tile, LANES), lambda i, j: (0, j, i, 0))
    one_pm = jax.ShapeDtypeStruct((RW_PAIRS, N_ROWS, LANES), F32)
    two_pm = jax.ShapeDtypeStruct((2, RW_PAIRS, N_ROWS, LANES), F32)
    in_specs += [pl.BlockSpec((None, RW_TAIL_PAD, 5 * RW_SLAB), lambda i, j: (j, 0, 0)), vec2, vec2, vec, vec, vec]
    args += [up_w, w0.reshape(2, 1, RW_W), a0.reshape(2, 1, RW_W), k_k.reshape(1, RW_W), k_a.reshape(1, RW_W),
             r_k.reshape(1, RW_W)]
    stage = lambda width: pltpu.VMEM((tile + 2 * SUB, width), F32)
    r, v, kk, decay, k_dir, b_dir, g, bonus = pl.pallas_call(
        _rwkv_prep_kernel,
        grid=(N_ROWS // tile, n_slab),
        in_specs=in_specs,
        out_specs=[pairs1, pairs1, pairs1, pairs2, pairs2, pairs2, out1, out1],
        out_shape=[one_pm, one_pm, one_pm, two_pm, two_pm, two_pm, one, one],
        scratch_shapes=[stage(RW_SLAB), stage(RW_SLAB), stage(RW_SLAB), stage(RW_TAIL_PAD)],
        compiler_params=_params("parallel", "parallel"),
        name="rwkv7_prep",
    )(*args)

    y2 = _rwkv_scan(r, kk, v, decay, k_dir, b_dir)

    return pl.pallas_call(
        _rwkv_post_kernel,
        grid=(N_ROWS // tile, n_slab),
        in_specs=[out2, out1, out1, vec, vec],
        out_specs=out1,
        out_shape=jax.ShapeDtypeStruct((N_ROWS, RW_W), BF16),
        compiler_params=_params("parallel", "parallel"),
        name="rwkv7_post",
    )(y2, g, bonus, ln_w.reshape(1, RW_W), ln_b.reshape(1, RW_W))


def _qkv_prep_kernel(x_ref, cos_ref, sin_ref, *rest, segments, quarter):
    gain_ref, o_ref = rest if len(rest) == 2 else (None, rest[0])
    _qkv_prep_body(x_ref, cos_ref, sin_ref, gain_ref, o_ref, segments, quarter)


def _qkv_prep_body(x_ref, cos_ref, sin_ref, gain_ref, o_ref, segments, quarter):
    tile = x_ref.shape[0]
    latent = pl.program_id(0) * tile < N_LAT_ROWS
    cos = jnp.where(latent, cos_ref[...], 1.0)
    sin = jnp.where(latent, sin_ref[...], 0.0)
    lane = lax.broadcasted_iota(jnp.int32, (1, LANES), 1)
    even_quarter = (lane // quarter) % 2 == 0
    for first, count, gain_row, rotary, scale in segments:
        for blk in range(first, first + count):
            cols = slice(blk * LANES, (blk + 1) * LANES)
            x = x_ref[:, cols]
            if gain_row is not None:
                x = x * lax.rsqrt(jnp.mean(x * x, axis=-1, keepdims=True) + NORM_EPS) * gain_ref[gain_row:gain_row + 1]
            if rotary:
                rot = jnp.where(even_quarter, -pltpu.roll(x, LANES - quarter, axis=1), pltpu.roll(x, quarter, axis=1))
                x = x * cos + rot * sin
            if scale != 1.0:
                x = x * scale
            o_ref[:, cols] = x.astype(o_ref.dtype)


def _qkv_prep(p, width, col_block, rope, gains, segments, quarter, name, tile=256):
    cos, sin = rope
    reps = LANES // cos.shape[1]
    cos, sin = jnp.tile(cos, (1, reps)), jnp.tile(sin, (1, reps))
    per_seq = SEQ // tile
    table = pl.BlockSpec((tile, LANES), lambda i: (i % per_seq, 0))
    return pl.pallas_call(
        functools.partial(_qkv_prep_kernel, segments=segments, quarter=quarter),
        grid=(N_ROWS // tile,),
        in_specs=[pl.BlockSpec((tile, width), lambda i: (i, col_block)), table, table]
        + ([] if gains is None else [pl.BlockSpec(gains.shape, lambda i: (0, 0))]),
        out_specs=pl.BlockSpec((tile, width), lambda i: (i, 0)),
        out_shape=jax.ShapeDtypeStruct((N_ROWS, width), BF16),
        compiler_params=_params("parallel"),
        name=name,
    )(p, cos, sin, *(() if gains is None else (gains,)))


def _even_mixer(a_all, rope, w_in, rw_args, q_g, k_g, need_ctx):
    rkv_w = 3 * RW_W
    att_w = (GQ_HEADS + 2 * GQ_KV_HEADS) * GQ_HEAD
    w_in = jnp.concatenate([w_in[:, :rkv_w], w_in[:, RW_IN:], w_in[:, rkv_w:RW_IN],
                            jnp.zeros((D_MODEL, RW_TAIL_PAD - RW_TAIL), w_in.dtype)], axis=1).astype(BF16)
    p = _matmul(a_all, w_in, out_dtype=F32, tm=1024, tn=768, tk=D_MODEL, name="even_w_in")
    rw = _rwkv_mixer(p, rkv_w + att_w, *rw_args)
    assert rkv_w % att_w == 0
    qkv = _qkv_prep(p, att_w, rkv_w // att_w, rope, jnp.stack([q_g, k_g]),
                    ((0, GQ_HEADS, 0, True, GQ_SCALE * LOG2E), (GQ_HEADS, GQ_KV_HEADS, 1, True, 1.0),
                     (GQ_HEADS + GQ_KV_HEADS, GQ_KV_HEADS, None, False, 1.0)), GQ_HEAD // 4, "gqa_prep")
    at = _gqa_attention(qkv, latent=True, tq=512)
    if need_ctx:
        at = jnp.concatenate([at, _gqa_attention(qkv, latent=False)], axis=0)
    return rw, at


def _odd_mixer(a_all, rope, w_in, q_norm, q_up, kv_norm, kv_up, lq1, lk1, lq2, lk2, subln, layer_idx):
    w_in = jnp.concatenate([w_in[:, MLA_IN:], w_in[:, :MLA_IN],
                            jnp.zeros((D_MODEL, IN_ODD_PADDED - IN_ODD), w_in.dtype)], axis=1).astype(BF16)
    p = _matmul(a_all, w_in, out_dtype=F32, tm=1024, tn=768, tk=D_MODEL, name="odd_w_in")
    lam_init = 0.8 - 0.6 * math.exp(-0.3 * layer_idx)
    lam = (jnp.exp(jnp.sum(lq1 * lk1).astype(F32)) - jnp.exp(jnp.sum(lq2 * lk2).astype(F32)) + lam_init)

    pm = p[:, DIFF_IN:DIFF_IN + MLA_IN]
    c_q = _rms_f32(pm[:N_LAT_ROWS, :Q_LORA], q_norm).astype(BF16)
    c_kv = _rms_f32(pm[:, Q_LORA:Q_LORA + KV_LORA], kv_norm).astype(BF16)
    k_pe = pm[:, Q_LORA + KV_LORA:MLA_IN]
    q_up_h = q_up.reshape(Q_LORA, MLA_HEADS, MLA_NOPE + MLA_ROPE) * (MLA_SCALE * LOG2E)
    q_up_nope = q_up_h[:, :, :MLA_NOPE].reshape(Q_LORA, -1).astype(BF16)
    q_up_pe = q_up_h[:, :, MLA_NOPE:].reshape(Q_LORA, -1).astype(BF16)
    kv_up_h = kv_up.reshape(KV_LORA, MLA_HEADS, MLA_NOPE + MLA_V)
    kv_up_r = jnp.concatenate([kv_up_h[:, :, :MLA_NOPE].reshape(KV_LORA, -1),
                               kv_up_h[:, :, MLA_NOPE:].reshape(KV_LORA, -1)], axis=1).astype(BF16)
    q_nope = _matmul(c_q, q_up_nope, out_dtype=BF16, tm=1024, tn=2048, tk=Q_LORA, name="mla_q_up_nope")
    q_pe = _matmul(c_q, q_up_pe, out_dtype=F32, tm=1024, tn=1024, tk=Q_LORA, name="mla_q_up_rope")
    kv = _matmul(c_kv, kv_up_r, out_dtype=BF16, tm=1024, tn=2048, tk=KV_LORA, name="mla_kv_up")
    q_pe = _rope_lat(q_pe.reshape(N_LAT_ROWS, MLA_HEADS, MLA_ROPE), rope)
    q_pe = q_pe.reshape(N_LAT_ROWS, MLA_HEADS * MLA_ROPE).astype(BF16)
    k_pe = _rope_lat(k_pe.reshape(N_ROWS, 1, MLA_ROPE), rope).reshape(N_ROWS, MLA_ROPE).astype(BF16)
    k_pe = jnp.tile(k_pe, (1, MLA_HEADS_PER_STEP))
    m_out = _mla_attention(q_nope, q_pe, kv, k_pe, tq=512)

    dqkv = _qkv_prep(p, DIFF_IN, 0, rope, None,
                     ((0, DIFF_HEADS, None, True, DIFF_SCALE * LOG2E), (DIFF_HEADS, DIFF_HEADS, None, True, 1.0),
                      (2 * DIFF_HEADS, DIFF_HEADS, None, False, 1.0)), DIFF_HEAD // 4, "diff_prep")
    d_out = _diff_attention(lam.reshape(1), dqkv, subln, 1.0 - lam_init, tq=512)
    return m_out, d_out


def kernel(x, c, ctx, c_ctx, ada_w, ada_b, norm1_g, norm2_g, mlp_w1, mlp_w2, final_g, ev_w_in, ev_w_out, rw_mu, rw_w0, rw_w_up, rw_a0, rw_a_up, rw_g_up, rw_k_k, rw_k_a, rw_r_k, rw_ln_w, rw_ln_b, gq_q_norm, gq_k_norm, od_w_in, od_w_out, mla_q_norm, mla_q_up, mla_kv_norm, mla_kv_up, diff_lq1, diff_lk1, diff_lq2, diff_lk2, diff_subln):
    rope_gq = _axial_rope_tables(SEQ, GQ_HEAD)
    rope_64 = _axial_rope_tables(SEQ, MLA_ROPE)
    h = (x.reshape(N_LAT_ROWS, D_MODEL), ctx.reshape(N_CTX_ROWS, D_MODEL))
    cond = jnp.concatenate([c_ctx[None], c, jnp.zeros((3, D_MODEL), F32)], axis=0)
    cond = jax.nn.silu(cond).astype(BF16)
    for i in range(DEPTH):
        last = i == DEPTH - 1
        mod = _matmul(cond, ada_w, layer=i, out_dtype=F32, tm=8, tn=2048, tk=1024, name="adaln_mod")
        mod = (mod + ada_b[i])[:BATCH + 1].reshape(BATCH + 1, N_MOD, 1, D_MODEL)
        sh1, sc1, g1, sh2, sc2, g2 = (mod[:, m] for m in range(N_MOD))
        a_all = _norm(h, norm1_g[i], sc1, sh1, rows=N_ROWS, out_dtype=BF16)
        j = i // 2
        if i % 2 == 0:
            rw_args = (rw_mu[j], rw_w0[j], rw_w_up[j], rw_a0[j], rw_a_up[j], rw_g_up[j], rw_k_k[j], rw_k_a[j],
                       rw_r_k[j], rw_ln_w[j], rw_ln_b[j])
            mix = _even_mixer(a_all, rope_gq, ev_w_in[j], rw_args, gq_q_norm[j], gq_k_norm[j], not last)
            w_out = ev_w_out[j]
        else:
            if not last:
                raise NotImplementedError("context rows of an odd layer are only needed when a layer follows")
            mix = _odd_mixer(a_all, rope_64, od_w_in[j], mla_q_norm[j], mla_q_up[j], mla_kv_norm[j],
                             mla_kv_up[j], diff_lq1[j], diff_lk1[j], diff_lq2[j], diff_lk2[j], diff_subln[j], i)
            w_out = od_w_out[j]
        rows = N_LAT_ROWS if last else N_ROWS
        h = _matmul(mix, w_out.astype(BF16), out_dtype=F32, tm=1024, tn=512 if isinstance(h, tuple) else 1024,
                    tk=2048, rows=rows, epilogue="gated_residual", res=h, gate=g1, name="mixer_w_out")
        a2 = _norm(h, norm2_g[i], sc2, sh2, rows=rows, out_dtype=BF16)
        hid = _matmul(a2, mlp_w1, layer=i, out_dtype=BF16, tm=1024, tn=512, tk=D_MODEL,
                      epilogue="relu2", name="mlp_w1")
        h = _matmul(hid, mlp_w2, layer=i, out_dtype=F32, tm=1024, tn=1024, tk=2048,
                    epilogue="gated_residual", res=h, gate=g2, name="mlp_w2")
    out = _norm(h, final_g, rows=N_LAT_ROWS, out_dtype=F32)
    return out.reshape(BATCH, SEQ, D_MODEL)
```

```python
import functools
import math

import jax
import jax.numpy as jnp
from jax import lax
from jax.experimental import pallas as pl
from jax.experimental.pallas import tpu as pltpu

D_MODEL = 4096
BATCH = 4
SEQ = 4096
DEPTH = 2
CTX_LEN = 256
GRID_W = 64
ROPE_THETA = 10000.0
NORM_EPS = 1e-6
N_MOD = 6
HALF = D_MODEL // 2

RW_HEAD = 64
RW_W = HALF
RW_HEADS = RW_W // RW_HEAD
DECAY_RANK = 96
AAA_RANK = 96
GATE_RANK = 256
GN_EPS = 64e-5
RW_IN = 3 * RW_W + GATE_RANK + 2 * DECAY_RANK + 2 * AAA_RANK

GQ_HEAD = 128
GQ_HEADS = HALF // GQ_HEAD
GQ_KV_HEADS = GQ_HEADS // 4
GQ_GROUP = GQ_HEADS // GQ_KV_HEADS
GQ_SCALE = GQ_HEAD ** -0.5

MLA_NOPE = 128
MLA_ROPE = 64
MLA_V = 128
MLA_HEADS = HALF // MLA_V
Q_LORA = 768
KV_LORA = 512
MLA_SCALE = (MLA_NOPE + MLA_ROPE) ** -0.5
MLA_IN = Q_LORA + KV_LORA + MLA_ROPE

DIFF_HEAD = 64
DIFF_V = 2 * DIFF_HEAD
DIFF_HEADS = HALF // DIFF_V
DIFF_SCALE = DIFF_HEAD ** -0.5
DIFF_IN = DIFF_HEADS * (4 * DIFF_HEAD + DIFF_V)
IN_ODD = MLA_IN + DIFF_IN
IN_ODD_PADDED = 7680

N_LAT_ROWS = BATCH * SEQ
N_CTX_ROWS = BATCH * CTX_LEN
N_ROWS = N_LAT_ROWS + N_CTX_ROWS
CTX_BLOCK0 = N_LAT_ROWS // CTX_LEN

V7X_VMEM_LIMIT_BYTES = 56 * 1024 * 1024

BF16 = jnp.bfloat16
F32 = jnp.float32


def _params(*sem):
    return pltpu.CompilerParams(dimension_semantics=sem, vmem_limit_bytes=V7X_VMEM_LIMIT_BYTES)


def _row_group(row_tile, tile_rows):
    start = row_tile * tile_rows
    return jnp.where(start < N_LAT_ROWS, 1 + start // SEQ, 0)


def _split_rows_specs(block, tile_rows, col):
    n_lat = N_LAT_ROWS // tile_rows

    def lat_map(i, *rest):
        used = i < n_lat
        return jnp.where(used, i, n_lat - 1), jnp.where(used, col(i, *rest), 0)

    def ctx_map(i, *rest):
        used = i >= n_lat
        return jnp.where(used, i - n_lat, 0), jnp.where(used, col(i, *rest), 0)

    return [pl.BlockSpec(block, lat_map), pl.BlockSpec(block, ctx_map)]


def _pick_rows(lat_ref, ctx_ref, tile_rows):
    return jnp.where(pl.program_id(0) * tile_rows < N_LAT_ROWS, lat_ref[...], ctx_ref[...])


def _norm_kernel(*refs, modulate, split):
    x_refs, (g_ref, *rest) = refs[:1 + split], refs[1 + split:]
    if modulate:
        sc_ref, sh_ref, o_ref = rest
    else:
        (o_ref,) = rest
    x = _pick_rows(*x_refs, o_ref.shape[0]) if split else x_refs[0][...]
    ms = jnp.mean(x * x, axis=-1, keepdims=True)
    y = x * lax.rsqrt(ms + NORM_EPS) * g_ref[...]
    if modulate:
        y = y * (1.0 + sc_ref[0]) + sh_ref[0]
    o_ref[...] = y.astype(o_ref.dtype)


def _norm(x, g, sc=None, sh=None, *, rows, out_dtype, tile=256):
    split = isinstance(x, tuple)
    x_parts = x if split else (x,)
    d = x_parts[0].shape[1]
    modulate = sc is not None
    x_specs = (_split_rows_specs((tile, d), tile, lambda i: 0) if split else
               [pl.BlockSpec((tile, d), lambda i: (i, 0))])
    in_specs = x_specs + [pl.BlockSpec((1, d), lambda i: (0, 0))]
    args = [*x_parts, g.reshape(1, d)]
    if modulate:
        mod_spec = pl.BlockSpec((1, 1, d), lambda i: (_row_group(i, tile), 0, 0))
        in_specs += [mod_spec, mod_spec]
        args += [sc, sh]
    return pl.pallas_call(
        functools.partial(_norm_kernel, modulate=modulate, split=split),
        grid=(rows // tile,),
        in_specs=in_specs,
        out_specs=pl.BlockSpec((tile, d), lambda i: (i, 0)),
        out_shape=jax.ShapeDtypeStruct((rows, d), out_dtype),
        compiler_params=_params("parallel"),
        name="rmsnorm_mod" if modulate else "rmsnorm",
    )(*args)


def _mm_kernel(*refs, epilogue, nk, n_a, n_res):
    a_refs, w_ref, rest = refs[:n_a], refs[n_a], refs[n_a + 1:]
    if epilogue == "gated_residual":
        res_refs, (gate_ref, o_ref, acc_ref) = rest[:n_res], rest[n_res:]
    else:
        o_ref, acc_ref = rest
    k = pl.program_id(2)

    def product(a_ref):
        return jnp.dot(a_ref[...].astype(BF16), w_ref[...].astype(BF16), preferred_element_type=F32)

    def store(part, first):
        if first:
            acc_ref[...] = part
        else:
            acc_ref[...] += part

    if n_a == 1:
        part = product(a_refs[0])
        pl.when(k == 0)(lambda: store(part, True))
        pl.when(k > 0)(lambda: store(part, False))
    else:
        for idx, a_ref in enumerate(a_refs):
            pl.when(k == idx)(lambda a_ref=a_ref, idx=idx: store(product(a_ref), idx == 0))

    @pl.when(k == nk - 1)
    def _():
        acc = acc_ref[...]
        if epilogue == "relu2":
            acc = jnp.square(jnp.maximum(acc, 0.0))
        elif epilogue == "gated_residual":
            res = _pick_rows(*res_refs, o_ref.shape[0]) if n_res == 2 else res_refs[0][...]
            acc = res + gate_ref[0] * acc
        o_ref[...] = acc.astype(o_ref.dtype)


def _matmul(a, w, *, out_dtype, tm, tn, tk, rows=None, layer=None, epilogue=None, res=None, gate=None, name):
    kdim, n = w.shape[-2:]
    a_parts = a if isinstance(a, tuple) else (a,)
    m = a_parts[0].shape[0] if rows is None else rows
    assert sum(part.shape[1] for part in a_parts) == kdim and m % tm == 0 and n % tn == 0 and kdim % tk == 0
    assert (w.ndim == 3) == (layer is not None)
    nk = kdim // tk
    w_spec = (pl.BlockSpec((tk, tn), lambda i, j, k: (k, j)) if layer is None else
              pl.BlockSpec((None, tk, tn), lambda i, j, k: (layer, k, j)))
    if len(a_parts) == 1:
        in_specs = [pl.BlockSpec((tm, tk), lambda i, j, k: (i, k)), w_spec]
    else:
        assert len(a_parts) == nk and all(part.shape[1] == tk for part in a_parts)
        in_specs = [pl.BlockSpec((tm, tk), lambda i, j, k: (i, 0))] * nk + [w_spec]
    args = [*a_parts, w]
    res_parts = res if isinstance(res, tuple) else (res,)
    if epilogue == "gated_residual":
        in_specs += (_split_rows_specs((tm, tn), tm, lambda i, j, k: j) if len(res_parts) == 2 else
                     [pl.BlockSpec((tm, tn), lambda i, j, k: (i, j))])
        in_specs.append(pl.BlockSpec((1, 1, tn), lambda i, j, k: (_row_group(i, tm), 0, j)))
        args += [*res_parts, gate]
    return pl.pallas_call(
        functools.partial(_mm_kernel, epilogue=epilogue, nk=nk, n_a=len(a_parts), n_res=len(res_parts)),
        grid=(m // tm, n // tn, nk),
        in_specs=in_specs,
        out_specs=pl.BlockSpec((tm, tn), lambda i, j, k: (i, j)),
        out_shape=jax.ShapeDtypeStruct((m, n), out_dtype),
        scratch_shapes=[pltpu.VMEM((tm, tn), F32)],
        compiler_params=_params("parallel", "parallel", "arbitrary"),
        name=name,
    )(*args)


def _nt_dot(a, b):
    return lax.dot_general(a, b, (((1,), (1,)), ((), ())), preferred_element_type=F32)


LOG2E = math.log2(math.e)


def _fill_values_and_ones(v_ref, vo_ref):
    d = v_ref.shape[1]
    vo_ref[:, :d] = v_ref[...]
    vo_ref[:, d:] = jnp.ones_like(v_ref)


def _softmax_times_values(s_list, vo_refs):
    d = vo_refs[0].shape[1] // 2
    m = functools.reduce(jnp.maximum, [jnp.max(s, axis=-1, keepdims=True) for s in s_list])
    acc = functools.reduce(jnp.add, [jnp.dot(jnp.exp2(s - m).astype(BF16), vo_ref[...], preferred_element_type=F32)
                                     for s, vo_ref in zip(s_list, vo_refs)])
    return acc[:, :d] / acc[:, d:]


def _attend_each(items):
    outs = []
    pending = None
    for q, k_refs, vo_refs in items:
        s_list = [_nt_dot(q, k_ref[...]) for k_ref in k_refs]
        if pending is not None:
            outs.append(_softmax_times_values(*pending))
        pending = (s_list, vo_refs)
    outs.append(_softmax_times_values(*pending))
    return outs


def _gqa_kernel(q_ref, *rest, with_lat):
    if with_lat:
        kc_ref, vc_ref, kl_ref, vl_ref, o_ref, voc_ref, vol_ref = rest
        k_refs, v_refs, vo_refs = (kc_ref, kl_ref), (vc_ref, vl_ref), (voc_ref, vol_ref)
    else:
        kc_ref, vc_ref, o_ref, voc_ref = rest
        k_refs, v_refs, vo_refs = (kc_ref,), (vc_ref,), (voc_ref,)

    @pl.when(pl.program_id(2) == 0)
    def _():
        for v_ref, vo_ref in zip(v_refs, vo_refs):
            _fill_values_and_ones(v_ref, vo_ref)

    outs = _attend_each([(q_ref[:, g * GQ_HEAD:(g + 1) * GQ_HEAD], k_refs, vo_refs) for g in range(GQ_GROUP)])
    for g, o in enumerate(outs):
        o_ref[:, g * GQ_HEAD:(g + 1) * GQ_HEAD] = o.astype(o_ref.dtype)


def _gqa_attention(qkv, *, latent, tq=256):
    gw = GQ_GROUP * GQ_HEAD
    k0, v0 = GQ_HEADS, GQ_HEADS + GQ_KV_HEADS
    ctx_k = pl.BlockSpec((CTX_LEN, GQ_HEAD), lambda b, n, i: (CTX_BLOCK0 + b, k0 + n))
    ctx_v = pl.BlockSpec((CTX_LEN, GQ_HEAD), lambda b, n, i: (CTX_BLOCK0 + b, v0 + n))
    scratch = [pltpu.VMEM((CTX_LEN, 2 * GQ_HEAD), BF16)]
    if latent:
        per_b = SEQ // tq
        lat_k = pl.BlockSpec((SEQ, GQ_HEAD), lambda b, n, i: (b, k0 + n))
        lat_v = pl.BlockSpec((SEQ, GQ_HEAD), lambda b, n, i: (b, v0 + n))
        in_specs = [pl.BlockSpec((tq, gw), lambda b, n, i: (b * per_b + i, n)), ctx_k, ctx_v, lat_k, lat_v]
        args = (qkv,) * 5
        rows = N_LAT_ROWS
        scratch.append(pltpu.VMEM((SEQ, 2 * GQ_HEAD), BF16))
    else:
        per_b = CTX_LEN // tq
        q_blk0 = N_LAT_ROWS // tq
        in_specs = [pl.BlockSpec((tq, gw), lambda b, n, i: (q_blk0 + b * per_b + i, n)), ctx_k, ctx_v]
        args = (qkv,) * 3
        rows = N_CTX_ROWS
    return pl.pallas_call(
        functools.partial(_gqa_kernel, with_lat=latent),
        grid=(BATCH, GQ_KV_HEADS, per_b),
        in_specs=in_specs,
        out_specs=pl.BlockSpec((tq, gw), lambda b, n, i: (b * per_b + i, n)),
        out_shape=jax.ShapeDtypeStruct((rows, GQ_HEADS * GQ_HEAD), BF16),
        scratch_shapes=scratch,
        compiler_params=_params("parallel", "parallel", "arbitrary"),
        name="gqa_attention_lat" if latent else "gqa_attention_ctx",
    )(*args)


def _mla_kernel(qn_ref, qp_ref, knc_ref, kpc_ref, vc_ref, knl_ref, kpl_ref, vl_ref, o_ref,
                kc_ref, kl_ref, voc_ref, vol_ref):
    heads = range(MLA_HEADS_PER_STEP)
    cols = [slice(j * MLA_NOPE, (j + 1) * MLA_NOPE) for j in heads]

    def own_rope_lanes(x, j):
        lane = lax.broadcasted_iota(jnp.int32, x.shape, 1)
        return jnp.where(lane // MLA_ROPE == j, x, jnp.zeros_like(x))

    @pl.when(pl.program_id(2) == 0)
    def _():
        for kn_ref, kp_ref, v_ref, k_ref, vo_ref in ((knc_ref, kpc_ref, vc_ref, kc_ref, voc_ref),
                                                     (knl_ref, kpl_ref, vl_ref, kl_ref, vol_ref)):
            for j in heads:
                k_ref[j, :, :MLA_NOPE] = kn_ref[:, cols[j]]
                k_ref[j, :, MLA_NOPE:] = own_rope_lanes(kp_ref[...], j)
                vo_ref[j, :, :MLA_V] = v_ref[:, cols[j]]
                vo_ref[j, :, MLA_V:] = jnp.ones_like(kp_ref)

    qp = qp_ref[...]
    items = [(jnp.concatenate([qn_ref[:, cols[j]], own_rope_lanes(qp, j)], axis=1), (kc_ref.at[j], kl_ref.at[j]),
              (voc_ref.at[j], vol_ref.at[j])) for j in heads]
    for j, o in enumerate(_attend_each(items)):
        o_ref[:, cols[j]] = o.astype(o_ref.dtype)


MLA_HEADS_PER_STEP = 2


def _mla_attention(q_nope, q_pe, kv, k_pe, *, tq=256):
    per_b = SEQ // tq
    hs = MLA_HEADS_PER_STEP
    assert hs * MLA_ROPE == MLA_NOPE
    w = hs * MLA_NOPE
    v_blk0 = MLA_HEADS // hs
    in_specs = [
        pl.BlockSpec((tq, w), lambda b, h, i: (b * per_b + i, h)),
        pl.BlockSpec((tq, hs * MLA_ROPE), lambda b, h, i: (b * per_b + i, h)),
        pl.BlockSpec((CTX_LEN, w), lambda b, h, i: (CTX_BLOCK0 + b, h)),
        pl.BlockSpec((CTX_LEN, MLA_NOPE), lambda b, h, i: (CTX_BLOCK0 + b, 0)),
        pl.BlockSpec((CTX_LEN, w), lambda b, h, i: (CTX_BLOCK0 + b, v_blk0 + h)),
        pl.BlockSpec((SEQ, w), lambda b, h, i: (b, h)),
        pl.BlockSpec((SEQ, MLA_NOPE), lambda b, h, i: (b, 0)),
        pl.BlockSpec((SEQ, w), lambda b, h, i: (b, v_blk0 + h)),
    ]
    return pl.pallas_call(
        _mla_kernel,
        grid=(BATCH, MLA_HEADS // hs, per_b),
        in_specs=in_specs,
        out_specs=pl.BlockSpec((tq, w), lambda b, h, i: (b * per_b + i, h)),
        out_shape=jax.ShapeDtypeStruct((N_LAT_ROWS, MLA_HEADS * MLA_V), BF16),
        scratch_shapes=[pltpu.VMEM((hs, CTX_LEN, 2 * MLA_NOPE), BF16), pltpu.VMEM((hs, SEQ, 2 * MLA_NOPE), BF16),
                        pltpu.VMEM((hs, CTX_LEN, 2 * MLA_V), BF16), pltpu.VMEM((hs, SEQ, 2 * MLA_V), BF16)],
        compiler_params=_params("parallel", "parallel", "arbitrary"),
        name="mla_attention",
    )(q_nope, q_pe, kv, k_pe, kv, kv, k_pe, kv)


def _diff_kernel(lam_ref, q_ref, kc_ref, vc_ref, kl_ref, vl_ref, g_ref, o_ref, voc_ref, vol_ref, *, out_scale):
    heads = range(DIFF_HEADS_PER_STEP)
    cols = [slice(j * DIFF_V, (j + 1) * DIFF_V) for j in heads]

    @pl.when(pl.program_id(2) == 0)
    def _():
        for v_ref, vo_ref in ((vc_ref, voc_ref), (vl_ref, vol_ref)):
            for j in heads:
                vo_ref[j, :, :DIFF_V] = v_ref[:, cols[j]]
                vo_ref[j, :, DIFF_V:] = jnp.ones((v_ref.shape[0], DIFF_V), BF16)

    first = lax.broadcasted_iota(jnp.int32, (q_ref.shape[0], DIFF_V), 1) < DIFF_HEAD
    items = []
    for j in heads:
        q = q_ref[:, cols[j]]
        zero = jnp.zeros_like(q)
        keys = (kc_ref[:, cols[j]], kl_ref[:, cols[j]])
        for qh in (jnp.where(first, q, zero), jnp.where(first, zero, q)):
            items.append((qh, keys, (voc_ref.at[j], vol_ref.at[j])))
    outs = _attend_each(items)
    for j in heads:
        o = outs[2 * j] - lam_ref[0] * outs[2 * j + 1]
        ms = jnp.mean(o * o, axis=-1, keepdims=True)
        o = o * lax.rsqrt(ms + NORM_EPS) * g_ref[...]
        o_ref[:, cols[j]] = (o * out_scale).astype(o_ref.dtype)


DIFF_HEADS_PER_STEP = 2


def _diff_attention(lam, qkv, subln, out_scale, *, tq=256):
    per_b = SEQ // tq
    hs = DIFF_HEADS_PER_STEP
    w = hs * DIFF_V
    k0, v0 = DIFF_HEADS // hs, 2 * DIFF_HEADS // hs
    ctx_k = pl.BlockSpec((CTX_LEN, w), lambda b, h, i: (CTX_BLOCK0 + b, k0 + h))
    ctx_v = pl.BlockSpec((CTX_LEN, w), lambda b, h, i: (CTX_BLOCK0 + b, v0 + h))
    lat_k = pl.BlockSpec((SEQ, w), lambda b, h, i: (b, k0 + h))
    lat_v = pl.BlockSpec((SEQ, w), lambda b, h, i: (b, v0 + h))
    in_specs = [
        pl.BlockSpec(memory_space=pltpu.SMEM),
        pl.BlockSpec((tq, w), lambda b, h, i: (b * per_b + i, h)),
        ctx_k, ctx_v, lat_k, lat_v,
        pl.BlockSpec((1, DIFF_V), lambda b, h, i: (0, 0)),
    ]
    return pl.pallas_call(
        functools.partial(_diff_kernel, out_scale=out_scale),
        grid=(BATCH, DIFF_HEADS // hs, per_b),
        in_specs=in_specs,
        out_specs=pl.BlockSpec((tq, w), lambda b, h, i: (b * per_b + i, h)),
        out_shape=jax.ShapeDtypeStruct((N_LAT_ROWS, DIFF_HEADS * DIFF_V), BF16),
        scratch_shapes=[pltpu.VMEM((hs, CTX_LEN, 2 * DIFF_V), BF16), pltpu.VMEM((hs, SEQ, 2 * DIFF_V), BF16)],
        compiler_params=_params("parallel", "parallel", "arbitrary"),
        name="diff_attention",
    )(lam, qkv, qkv, qkv, qkv, qkv, subln.reshape(1, DIFF_V))


RW_CHUNK = 256
RW_PAIRS = RW_HEADS // 2
LANES = 2 * RW_HEAD
SUB = 8
RW_UNROLL = 16
RW_GROUP = 8


def _rwkv_kernel(r_ref, kk_ref, v_ref, w_ref, k_ref, b_ref, y_out, s_ref, y_ref):
    d = pl.program_id(0)
    c = pl.program_id(2)

    @pl.when(c == 0)
    def _():
        s_ref[...] = jnp.zeros_like(s_ref)

    v_hi = lax.broadcasted_iota(jnp.int32, (SUB, SUB, LANES), 0)
    v_lo = lax.broadcasted_iota(jnp.int32, (SUB, SUB, LANES), 1)
    lane3 = lax.broadcasted_iota(jnp.int32, (SUB, SUB, LANES), 2)
    dup = jnp.where((lane3 & (RW_HEAD - 1)) == v_hi * SUB + v_lo, 1.0, 0.0)
    blk_r = lax.broadcasted_iota(jnp.int32, (LANES, LANES), 0) // RW_HEAD
    blk_c = lax.broadcasted_iota(jnp.int32, (LANES, LANES), 1) // RW_HEAD
    head_ones = jnp.where(blk_r == blk_c, 1.0, 0.0).astype(BF16)
    slot_of_lane = lax.broadcasted_iota(jnp.int32, (SUB, LANES), 1) & (RW_HEAD - 1)
    step_dir = jnp.where(d == 0, 1, -1)

    y_ref[...] = jnp.zeros_like(y_ref)

    def row(ref, p, t, *lead):
        return ref[(*lead, p, pl.ds(t, SUB, stride=0), slice(None))]

    def as_rows(tile):
        return tile.reshape(RW_HEAD, LANES).astype(BF16)

    def head_sums(states, pairs, t_done, t_next):
        rows = []
        for p in pairs:
            rows.append(as_rows(states[p] * row(r_ref, p, t_done)))
            if t_next is not None:
                rows.append(as_rows(states[p] * row(kk_ref, p, t_next)))
                rows.append((dup.astype(BF16) * row(v_ref, p, t_next).astype(BF16)).reshape(RW_HEAD, LANES))
        return jnp.dot(jnp.concatenate(rows, axis=0), head_ones, preferred_element_type=F32)

    def slab(sums, q, n_slabs, i):
        lo = (q * n_slabs + i) * RW_HEAD
        return sums[lo:lo + RW_HEAD].reshape(SUB, SUB, LANES)

    def put_y(p, y_b, t, valid):
        window = t // RW_HEAD
        keep = (slot_of_lane == t % RW_HEAD) & valid
        y_ref[0, p, window] = jnp.where(keep, y_b, y_ref[0, p, window])

    groups = [range(g0, g0 + RW_GROUP) for g0 in range(0, RW_PAIRS, RW_GROUP)]

    def block(g, carry):
        first = g * RW_UNROLL
        first = jnp.where(d == 0, first, RW_CHUNK - 1 - first)
        times = [first + u * step_dir for u in range(RW_UNROLL)]
        t_before = jnp.clip(first - step_dir, 0, RW_CHUNK - 1)
        states = [s_ref[p] for p in range(RW_PAIRS)]
        sums = [head_sums(states, grp, t_before, times[0]) for grp in groups]
        for gi, grp in enumerate(groups):
            for q, p in enumerate(grp):
                put_y(p, slab(sums[gi], q, 3, 0), t_before, g > 0)
        for u in range(RW_UNROLL):
            t = times[u]
            for gi, grp in enumerate(groups):
                for q, p in enumerate(grp):
                    sa = slab(sums[gi], q, 3, 1)
                    vb = slab(sums[gi], q, 3, 2)
                    states[p] = (states[p] * row(w_ref, p, t, 0) - sa * row(b_ref, p, t, 0)
                                 + vb * row(k_ref, p, t, 0))
                if u + 1 < RW_UNROLL:
                    sums[gi] = head_sums(states, grp, t, times[u + 1])
                    for q, p in enumerate(grp):
                        put_y(p, slab(sums[gi], q, 3, 0), t, True)
        for p in range(RW_PAIRS):
            s_ref[p] = states[p]
        return carry

    lax.fori_loop(0, RW_CHUNK // RW_UNROLL, block, 0)

    t_last = jnp.where(d == 0, RW_CHUNK - 1, 0)
    states = [s_ref[p] for p in range(RW_PAIRS)]
    for grp in groups:
        last = head_sums(states, grp, t_last, None)
        for q, p in enumerate(grp):
            put_y(p, slab(last, q, 1, 0), t_last, True)

    first_head = lax.broadcasted_iota(jnp.int32, (RW_HEAD, LANES), 1) < RW_HEAD
    for p in range(0, RW_PAIRS, 2):
        for window in range(RW_CHUNK // RW_HEAD):
            both = jnp.concatenate([y_ref[0, p, window].reshape(RW_HEAD, LANES),
                                    y_ref[0, p + 1, window].reshape(RW_HEAD, LANES)], axis=0)
            by_row = both.T
            head0, head1 = by_row[:RW_HEAD], by_row[RW_HEAD:]
            rows = slice(window * RW_HEAD, (window + 1) * RW_HEAD)
            y_out[0, rows, p * LANES:(p + 1) * LANES] = jnp.where(
                first_head, head0, pltpu.roll(head1, RW_HEAD, axis=1))
            y_out[0, rows, (p + 1) * LANES:(p + 2) * LANES] = jnp.where(
                first_head, pltpu.roll(head0, RW_HEAD, axis=1), head1)


def _rwkv_scan(r, kk, v, w, k, b):
    n_ctx = CTX_LEN // RW_CHUNK
    n_lat = SEQ // RW_CHUNK
    ctx0 = N_LAT_ROWS // RW_CHUNK

    def chunk(d, bb, c):
        j = jnp.where(d == 0, c, jnp.where(c < n_ctx, n_ctx - 1 - c, n_ctx + n_lat - 1 - (c - n_ctx)))
        return jnp.where(j < n_ctx, ctx0 + bb * n_ctx + j, bb * n_lat + j - n_ctx)

    shared = pl.BlockSpec((RW_PAIRS, RW_CHUNK, LANES), lambda d, bb, c: (0, chunk(d, bb, c), 0))
    per_dir = pl.BlockSpec((1, RW_PAIRS, RW_CHUNK, LANES), lambda d, bb, c: (d, 0, chunk(d, bb, c), 0))
    return pl.pallas_call(
        _rwkv_kernel,
        grid=(2, BATCH, n_ctx + n_lat),
        in_specs=[shared, shared, shared, per_dir, per_dir, per_dir],
        out_specs=pl.BlockSpec((1, RW_CHUNK, RW_W), lambda d, bb, c: (d, chunk(d, bb, c), 0)),
        out_shape=jax.ShapeDtypeStruct((2, N_ROWS, RW_W), F32),
        scratch_shapes=[pltpu.VMEM((RW_PAIRS, RW_HEAD // SUB, SUB, LANES), F32),
                        pltpu.VMEM((1, RW_PAIRS, RW_CHUNK // RW_HEAD, SUB, SUB, LANES), F32)],
        compiler_params=_params("parallel", "parallel", "arbitrary"),
        name="rwkv7_scan",
    )(r, kk, v, w, k, b)


def _axial_rope_tables(n_tokens, rot_dim):
    n_rows = n_tokens // GRID_W
    row = jnp.repeat(jnp.arange(n_rows, dtype=F32), GRID_W)
    col = jnp.tile(jnp.arange(GRID_W, dtype=F32), n_rows)
    axis_dim = rot_dim // 2
    inv_freq = ROPE_THETA ** (-jnp.arange(0, axis_dim, 2, dtype=F32) / axis_dim)
    ang_r = row[:, None] * inv_freq
    ang_c = col[:, None] * inv_freq
    ang = jnp.concatenate([ang_r, ang_r, ang_c, ang_c], axis=-1)
    return jnp.cos(ang), jnp.sin(ang)


def _rotate_half(z):
    z1, z2 = jnp.split(z, 2, axis=-1)
    return jnp.concatenate([-z2, z1], axis=-1)


def _apply_rope(x, cos, sin):
    half = x.shape[-1] // 2
    rot = jnp.concatenate([_rotate_half(x[..., :half]), _rotate_half(x[..., half:])], axis=-1)
    return x * cos + rot * sin


def _rms_f32(x, g):
    return x * lax.rsqrt(jnp.mean(x * x, axis=-1, keepdims=True) + NORM_EPS) * g


def _rope_lat(x, rope):
    cos, sin = rope
    heads, dim = x.shape[1:]
    lat = x[:N_LAT_ROWS].reshape(BATCH, SEQ, heads, dim)
    lat = _apply_rope(lat, cos[None, :, None, :], sin[None, :, None, :]).reshape(N_LAT_ROWS, heads, dim)
    return lat if x.shape[0] == N_LAT_ROWS else jnp.concatenate([lat, x[N_LAT_ROWS:]], axis=0)


RW_SLAB = 512
RW_TAIL = GATE_RANK + 2 * DECAY_RANK + 2 * AAA_RANK
RW_TAIL_PAD = 768
RW_ROW_TILE = 256


def _head_lane_sums(x):
    rows = x.shape[0]
    blk_r = lax.broadcasted_iota(jnp.int32, (LANES, LANES), 0) // RW_HEAD
    blk_c = lax.broadcasted_iota(jnp.int32, (LANES, LANES), 1) // RW_HEAD
    head_ones = jnp.where(blk_r == blk_c, 1.0, 0.0).astype(BF16)
    hi = x.astype(BF16)
    rest = x - hi.astype(F32)
    mid = rest.astype(BF16)
    lo = (rest - mid.astype(F32)).astype(BF16)
    pieces = jnp.concatenate([hi, mid, lo], axis=0)
    out = []
    for s in range(x.shape[1] // LANES):
        part = jnp.dot(pieces[:, s * LANES:(s + 1) * LANES], head_ones, preferred_element_type=F32)
        out.append(part[:rows] + part[rows:2 * rows] + part[2 * rows:])
    return jnp.concatenate(out, axis=1)


def _rwkv_prep_kernel(*refs):
    slabs = [refs[4 * n:4 * n + 4] for n in range(4)]
    (up_w_ref, w0_ref, a0_ref, kk_gain_ref, ka_ref, rk_ref,
     r_out, v_out, kk_out, decay_out, kdir_out, b_out, g_out, bonus_out) = refs[16:30]
    stage_refs = refs[30:34]
    tile = RW_ROW_TILE
    row0 = pl.program_id(0) * tile
    seq_len = jnp.where(row0 < N_LAT_ROWS, SEQ, CTX_LEN)
    starts_seq = row0 % seq_len == 0
    ends_seq = (row0 + tile) % seq_len == 0

    def shifted(main_ref, before_ref, after_ref, mu_ref, stage_ref):
        x = main_ref[...]
        stage_ref[SUB:SUB + tile, :] = x
        stage_ref[SUB - 1:SUB, :] = jnp.where(starts_seq, 0.0, before_ref[SUB - 1:SUB, :])
        stage_ref[SUB + tile:SUB + tile + 1, :] = jnp.where(ends_seq, 0.0, after_ref[0:1, :])
        around = 0.5 * (stage_ref[SUB - 1:SUB - 1 + tile, :] + stage_ref[SUB + 1:SUB + 1 + tile, :])
        return x + (around - x) * mu_ref[...]

    def put_pairs(out_ref, x, *lead):
        for q in range(RW_SLAB // LANES):
            out_ref[(*lead, q)] = x[:, q * LANES:(q + 1) * LANES]

    r, k, v, tail = (shifted(*slab, stage) for slab, stage in zip(slabs, stage_refs))
    put_pairs(r_out, r)
    put_pairs(v_out, v)
    s_w = RW_SLAB
    mid = tail[:, GATE_RANK:2 * GATE_RANK]
    mid_lane = lax.broadcasted_iota(jnp.int32, (1, GATE_RANK), 1)
    act = jnp.concatenate([jax.nn.sigmoid(tail[:, :GATE_RANK]),
                           jnp.where(mid_lane < 2 * DECAY_RANK, jnp.tanh(mid), mid),
                           tail[:, 2 * GATE_RANK:]], axis=1)
    up = jnp.dot(act.astype(BF16), up_w_ref[...], preferred_element_type=F32)
    g_out[...] = up[:, :s_w]
    kq = k * kk_gain_ref[...]
    kk = kq / jnp.maximum(jnp.sqrt(_head_lane_sums(kq * kq)), 1e-12)
    put_pairs(kk_out, kk)
    kdir_sum = jnp.zeros_like(k)
    for d in range(2):
        x = w0_ref[d] + up[:, (1 + d) * s_w:(2 + d) * s_w]
        put_pairs(decay_out, jnp.exp(jax.nn.sigmoid(x) * -math.exp(-0.5)), d)
        a = jax.nn.sigmoid(a0_ref[d] + up[:, (3 + d) * s_w:(4 + d) * s_w])
        kd = k * (1.0 + (a - 1.0) * ka_ref[...])
        put_pairs(kdir_out, kd, d)
        put_pairs(b_out, kk * a, d)
        kdir_sum = kdir_sum + kd
    bonus_out[...] = _head_lane_sums(r * kdir_sum * rk_ref[...]) * v


def _rwkv_post_kernel(y_ref, g_ref, bonus_ref, lnw_ref, lnb_ref, o_ref):
    y = y_ref[0] + y_ref[1]
    mean = _head_lane_sums(y) * (1.0 / RW_HEAD)
    cen = y - mean
    var = _head_lane_sums(cen * cen) * (1.0 / RW_HEAD)
    yn = cen * lax.rsqrt(var + GN_EPS) * lnw_ref[...] + lnb_ref[...]
    o_ref[...] = ((yn + bonus_ref[...]) * g_ref[...]).astype(o_ref.dtype)


def _rwkv_mixer(p, tail_col, mu, w0, w_up, a0, a_up, g_up, k_k, k_a, r_k, ln_w, ln_b):
    assert tail_col % RW_TAIL_PAD == 0 and p.shape[1] >= tail_col + RW_TAIL_PAD
    n_slab = RW_W // RW_SLAB
    up_w = jnp.zeros((RW_TAIL_PAD, 5, RW_W), F32)
    o = GATE_RANK
    up_w = up_w.at[:o, 0].set(g_up)
    for d in range(2):
        up_w = up_w.at[o + d * DECAY_RANK:o + (d + 1) * DECAY_RANK, 1 + d].set(w_up[d])
    o += 2 * DECAY_RANK
    for d in range(2):
        up_w = up_w.at[o + d * AAA_RANK:o + (d + 1) * AAA_RANK, 3 + d].set(a_up[d])
    up_w = up_w.reshape(RW_TAIL_PAD, 5, n_slab, RW_SLAB).transpose(2, 0, 1, 3)
    up_w = up_w.reshape(n_slab, RW_TAIL_PAD, 5 * RW_SLAB).astype(BF16)
    mu_main = mu[:3 * RW_W].reshape(1, 3 * RW_W)
    mu_tail = jnp.pad(mu[3 * RW_W:], (0, RW_TAIL_PAD - RW_TAIL)).reshape(1, RW_TAIL_PAD)

    tile = RW_ROW_TILE
    col_blocks = RW_W // RW_SLAB
    halo = tile // SUB
    last_halo = N_ROWS // SUB - 1

    def with_halo(width, col):
        return [pl.BlockSpec((tile, width), lambda i, j: (i, col(j))),
                pl.BlockSpec((SUB, width), lambda i, j: (jnp.maximum(i * halo - 1, 0), col(j))),
                pl.BlockSpec((SUB, width), lambda i, j: (jnp.minimum((i + 1) * halo, last_halo), col(j)))]

    in_specs, args = [], []
    for n in range(3):
        in_specs += with_halo(RW_SLAB, lambda j, n=n: n * col_blocks + j)
        in_specs.append(pl.BlockSpec((1, RW_SLAB), lambda i, j, n=n: (0, n * col_blocks + j)))
        args += [p, p, p, mu_main]
    tail_blk = tail_col // RW_TAIL_PAD
    in_specs += with_halo(RW_TAIL_PAD, lambda j: tail_blk)
    in_specs.append(pl.BlockSpec((1, RW_TAIL_PAD), lambda i, j: (0, 0)))
    args += [p, p, p, mu_tail]

    vec = pl.BlockSpec((1, RW_SLAB), lambda i, j: (0, j))
    vec2 = pl.BlockSpec((2, 1, RW_SLAB), lambda i, j: (0, 0, j))
    out1 = pl.BlockSpec((tile, RW_SLAB), lambda i, j: (i, j))
    out2 = pl.BlockSpec((2, tile, RW_SLAB), lambda i, j: (0, i, j))
    one = jax.ShapeDtypeStruct((N_ROWS, RW_W), F32)
    slab_pairs = RW_SLAB // LANES
    pairs1 = pl.BlockSpec((slab_pairs, tile, LANES), lambda i, j: (j, i, 0))
    pairs2 = pl.BlockSpec((2, slab_pairs, tile, LANES), lambda i, j: (0, j, i, 0))
    one_pm = jax.ShapeDtypeStruct((RW_PAIRS, N_ROWS, LANES), F32)
    two_pm = jax.ShapeDtypeStruct((2, RW_PAIRS, N_ROWS, LANES), F32)
    in_specs += [pl.BlockSpec((None, RW_TAIL_PAD, 5 * RW_SLAB), lambda i, j: (j, 0, 0)), vec2, vec2, vec, vec, vec]
    args += [up_w, w0.reshape(2, 1, RW_W), a0.reshape(2, 1, RW_W), k_k.reshape(1, RW_W), k_a.reshape(1, RW_W),
             r_k.reshape(1, RW_W)]
    stage = lambda width: pltpu.VMEM((tile + 2 * SUB, width), F32)
    r, v, kk, decay, k_dir, b_dir, g, bonus = pl.pallas_call(
        _rwkv_prep_kernel,
        grid=(N_ROWS // tile, n_slab),
        in_specs=in_specs,
        out_specs=[pairs1, pairs1, pairs1, pairs2, pairs2, pairs2, out1, out1],
        out_shape=[one_pm, one_pm, one_pm, two_pm, two_pm, two_pm, one, one],
        scratch_shapes=[stage(RW_SLAB), stage(RW_SLAB), stage(RW_SLAB), stage(RW_TAIL_PAD)],
        compiler_params=_params("parallel", "parallel"),
        name="rwkv7_prep",
    )(*args)

    y2 = _rwkv_scan(r, kk, v, decay, k_dir, b_dir)

    return pl.pallas_call(
        _rwkv_post_kernel,
        grid=(N_ROWS // tile, n_slab),
        in_specs=[out2, out1, out1, vec, vec],
        out_specs=out1,
        out_shape=jax.ShapeDtypeStruct((N_ROWS, RW_W), BF16),
        compiler_params=_params("parallel", "parallel"),
        name="rwkv7_post",
    )(y2, g, bonus, ln_w.reshape(1, RW_W), ln_b.reshape(1, RW_W))


def _qkv_prep_kernel(x_ref, cos_ref, sin_ref, *rest, segments, quarter):
    gain_ref, o_ref = rest if len(rest) == 2 else (None, rest[0])
    _qkv_prep_body(x_ref, cos_ref, sin_ref, gain_ref, o_ref, segments, quarter)


def _qkv_prep_body(x_ref, cos_ref, sin_ref, gain_ref, o_ref, segments, quarter):
    tile = x_ref.shape[0]
    latent = pl.program_id(0) * tile < N_LAT_ROWS
    cos = jnp.where(latent, cos_ref[...], 1.0)
    sin = jnp.where(latent, sin_ref[...], 0.0)
    lane = lax.broadcasted_iota(jnp.int32, (1, LANES), 1)
    even_quarter = (lane // quarter) % 2 == 0
    for first, count, gain_row, rotary, scale in segments:
        for blk in range(first, first + count):
            cols = slice(blk * LANES, (blk + 1) * LANES)
            x = x_ref[:, cols]
            if gain_row is not None:
                x = x * lax.rsqrt(jnp.mean(x * x, axis=-1, keepdims=True) + NORM_EPS) * gain_ref[gain_row:gain_row + 1]
            if rotary:
                rot = jnp.where(even_quarter, -pltpu.roll(x, LANES - quarter, axis=1), pltpu.roll(x, quarter, axis=1))
                x = x * cos + rot * sin
            if scale != 1.0:
                x = x * scale
            o_ref[:, cols] = x.astype(o_ref.dtype)


def _qkv_prep(p, width, col_block, rope, gains, segments, quarter, name, tile=256):
    cos, sin = rope
    reps = LANES // cos.shape[1]
    cos, sin = jnp.tile(cos, (1, reps)), jnp.tile(sin, (1, reps))
    per_seq = SEQ // tile
    table = pl.BlockSpec((tile, LANES), lambda i: (i % per_seq, 0))
    return pl.pallas_call(
        functools.partial(_qkv_prep_kernel, segments=segments, quarter=quarter),
        grid=(N_ROWS // tile,),
        in_specs=[pl.BlockSpec((tile, width), lambda i: (i, col_block)), table, table]
        + ([] if gains is None else [pl.BlockSpec(gains.shape, lambda i: (0, 0))]),
        out_specs=pl.BlockSpec((tile, width), lambda i: (i, 0)),
        out_shape=jax.ShapeDtypeStruct((N_ROWS, width), BF16),
        compiler_params=_params("parallel"),
        name=name,
    )(p, cos, sin, *(() if gains is None else (gains,)))


def _even_mixer(a_all, rope, w_in, rw_args, q_g, k_g, need_ctx):
    rkv_w = 3 * RW_W
    att_w = (GQ_HEADS + 2 * GQ_KV_HEADS) * GQ_HEAD
    w_in = jnp.concatenate([w_in[:, :rkv_w], w_in[:, RW_IN:], w_in[:, rkv_w:RW_IN],
                            jnp.zeros((D_MODEL, RW_TAIL_PAD - RW_TAIL), w_in.dtype)], axis=1).astype(BF16)
    p = _matmul(a_all, w_in, out_dtype=F32, tm=1024, tn=768, tk=D_MODEL, name="even_w_in")
    rw = _rwkv_mixer(p, rkv_w + att_w, *rw_args)
    assert rkv_w % att_w == 0
    qkv = _qkv_prep(p, att_w, rkv_w // att_w, rope, jnp.stack([q_g, k_g]),
                    ((0, GQ_HEADS, 0, True, GQ_SCALE * LOG2E), (GQ_HEADS, GQ_KV_HEADS, 1, True, 1.0),
                     (GQ_HEADS + GQ_KV_HEADS, GQ_KV_HEADS, None, False, 1.0)), GQ_HEAD // 4, "gqa_prep")
    at = _gqa_attention(qkv, latent=True, tq=512)
    if need_ctx:
        at = jnp.concatenate([at, _gqa_attention(qkv, latent=False)], axis=0)
    return rw, at


def _odd_mixer(a_all, rope, w_in, q_norm, q_up, kv_norm, kv_up, lq1, lk1, lq2, lk2, subln, layer_idx):
    w_in = jnp.concatenate([w_in[:, MLA_IN:], w_in[:, :MLA_IN],
                            jnp.zeros((D_MODEL, IN_ODD_PADDED - IN_ODD), w_in.dtype)], axis=1).astype(BF16)
    p = _matmul(a_all, w_in, out_dtype=F32, tm=1024, tn=768, tk=D_MODEL, name="odd_w_in")
    lam_init = 0.8 - 0.6 * math.exp(-0.3 * layer_idx)
    lam = (jnp.exp(jnp.sum(lq1 * lk1).astype(F32)) - jnp.exp(jnp.sum(lq2 * lk2).astype(F32)) + lam_init)

    pm = p[:, DIFF_IN:DIFF_IN + MLA_IN]
    c_q = _rms_f32(pm[:N_LAT_ROWS, :Q_LORA], q_norm).astype(BF16)
    c_kv = _rms_f32(pm[:, Q_LORA:Q_LORA + KV_LORA], kv_norm).astype(BF16)
    k_pe = pm[:, Q_LORA + KV_LORA:MLA_IN]
    q_up_h = q_up.reshape(Q_LORA, MLA_HEADS, MLA_NOPE + MLA_ROPE) * (MLA_SCALE * LOG2E)
    q_up_nope = q_up_h[:, :, :MLA_NOPE].reshape(Q_LORA, -1).astype(BF16)
    q_up_pe = q_up_h[:, :, MLA_NOPE:].reshape(Q_LORA, -1).astype(BF16)
    kv_up_h = kv_up.reshape(KV_LORA, MLA_HEADS, MLA_NOPE + MLA_V)
    kv_up_r = jnp.concatenate([kv_up_h[:, :, :MLA_NOPE].reshape(KV_LORA, -1),
                               kv_up_h[:, :, MLA_NOPE:].reshape(KV_LORA, -1)], axis=1).astype(BF16)
    q_nope = _matmul(c_q, q_up_nope, out_dtype=BF16, tm=1024, tn=2048, tk=Q_LORA, name="mla_q_up_nope")
    q_pe = _matmul(c_q, q_up_pe, out_dtype=F32, tm=1024, tn=1024, tk=Q_LORA, name="mla_q_up_rope")
    kv = _matmul(c_kv, kv_up_r, out_dtype=BF16, tm=1024, tn=2048, tk=KV_LORA, name="mla_kv_up")
    q_pe = _rope_lat(q_pe.reshape(N_LAT_ROWS, MLA_HEADS, MLA_ROPE), rope)
    q_pe = q_pe.reshape(N_LAT_ROWS, MLA_HEADS * MLA_ROPE).astype(BF16)
    k_pe = _rope_lat(k_pe.reshape(N_ROWS, 1, MLA_ROPE), rope).reshape(N_ROWS, MLA_ROPE).astype(BF16)
    k_pe = jnp.tile(k_pe, (1, MLA_HEADS_PER_STEP))
    m_out = _mla_attention(q_nope, q_pe, kv, k_pe, tq=512)

    dqkv = _qkv_prep(p, DIFF_IN, 0, rope, None,
                     ((0, DIFF_HEADS, None, True, DIFF_SCALE * LOG2E), (DIFF_HEADS, DIFF_HEADS, None, True, 1.0),
                      (2 * DIFF_HEADS, DIFF_HEADS, None, False, 1.0)), DIFF_HEAD // 4, "diff_prep")
    d_out = _diff_attention(lam.reshape(1), dqkv, subln, 1.0 - lam_init, tq=512)
    return m_out, d_out


def kernel(x, c, ctx, c_ctx, ada_w, ada_b, norm1_g, norm2_g, mlp_w1, mlp_w2, final_g, ev_w_in, ev_w_out, rw_mu, rw_w0, rw_w_up, rw_a0, rw_a_up, rw_g_up, rw_k_k, rw_k_a, rw_r_k, rw_ln_w, rw_ln_b, gq_q_norm, gq_k_norm, od_w_in, od_w_out, mla_q_norm, mla_q_up, mla_kv_norm, mla_kv_up, diff_lq1, diff_lk1, diff_lq2, diff_lk2, diff_subln):
    rope_gq = _axial_rope_tables(SEQ, GQ_HEAD)
    rope_64 = _axial_rope_tables(SEQ, MLA_ROPE)
    h = (x.reshape(N_LAT_ROWS, D_MODEL), ctx.reshape(N_CTX_ROWS, D_MODEL))
    cond = jnp.concatenate([c_ctx[None], c, jnp.zeros((3, D_MODEL), F32)], axis=0)
    cond = jax.nn.silu(cond).astype(BF16)
    for i in range(DEPTH):
        last = i == DEPTH - 1
        mod = _matmul(cond, ada_w, layer=i, out_dtype=F32, tm=8, tn=2048, tk=1024, name="adaln_mod")
        mod = (mod + ada_b[i])[:BATCH + 1].reshape(BATCH + 1, N_MOD, 1, D_MODEL)
        sh1, sc1, g1, sh2, sc2, g2 = (mod[:, m] for m in range(N_MOD))
        a_all = _norm(h, norm1_g[i], sc1, sh1, rows=N_ROWS, out_dtype=BF16)
        j = i // 2
        if i % 2 == 0:
            rw_args = (rw_mu[j], rw_w0[j], rw_w_up[j], rw_a0[j], rw_a_up[j], rw_g_up[j], rw_k_k[j], rw_k_a[j],
                       rw_r_k[j], rw_ln_w[j], rw_ln_b[j])
            mix = _even_mixer(a_all, rope_gq, ev_w_in[j], rw_args, gq_q_norm[j], gq_k_norm[j], not last)
            w_out = ev_w_out[j]
        else:
            if not last:
                raise NotImplementedError("context rows of an odd layer are only needed when a layer follows")
            mix = _odd_mixer(a_all, rope_64, od_w_in[j], mla_q_norm[j], mla_q_up[j], mla_kv_norm[j],
                             mla_kv_up[j], diff_lq1[j], diff_lk1[j], diff_lq2[j], diff_lk2[j], diff_subln[j], i)
            w_out = od_w_out[j]
        rows = N_LAT_ROWS if last else N_ROWS
        h = _matmul(mix, w_out.astype(BF16), out_dtype=F32, tm=1024, tn=512 if isinstance(h, tuple) else 1024,
                    tk=2048, rows=rows, epilogue="gated_residual", res=h, gate=g1, name="mixer_w_out")
        a2 = _norm(h, norm2_g[i], sc2, sh2, rows=rows, out_dtype=BF16)
        hid = _matmul(a2, mlp_w1, layer=i, out_dtype=BF16, tm=1024, tn=512, tk=D_MODEL,
                      epilogue="relu2", name="mlp_w1")
        h = _matmul(hid, mlp_w2, layer=i, out_dtype=F32, tm=1024, tn=1024, tk=2048,
                    epilogue="gated_residual", res=h, gate=g2, name="mlp_w2")
    out = _norm(h, final_g, rows=N_LAT_ROWS, out_dtype=F32)
    return out.reshape(BATCH, SEQ, D_MODEL)
```
